```python
import math
import jax
import jax.numpy as jnp
from jax import lax
import numpy as np

D_MODEL = 2048
BATCH = 2
SEQ = 4096
DEPTH = 2

N_EVEN = (DEPTH + 1) // 2
N_ODD = DEPTH // 2
EPS = 1e-6
MIX_WIDTH = D_MODEL
A_QK_DIM = 64
A_V_DIM = 2 * A_QK_DIM
A_WIDTH = MIX_WIDTH // 2
A_HEADS = A_WIDTH // A_V_DIM
Q_BLOCK = 128
N_BUCKETS = 32
MAX_EXACT = 16
MAX_DISTANCE = 128
B_WIDTH = MIX_WIDTH - A_WIDTH
B_HEADS = 4
B_VAL_DIM = B_WIDTH // B_HEADS
B_KEY_DIM = B_VAL_DIM // 2
GATE_RANK = 16
GATE_TEMP = 16.0
GLA_CHUNK = 64
IN_AB_SIZES = (A_HEADS * 2 * A_QK_DIM, A_HEADS * 2 * A_QK_DIM, A_WIDTH,
               B_HEADS * B_KEY_DIM, B_HEADS * B_KEY_DIM, B_WIDTH, B_WIDTH, GATE_RANK)
IN_AB = sum(IN_AB_SIZES)
C_WIDTH = D_MODEL
C_GROUPS = 8
C_GROUP_DIM = C_WIDTH // C_GROUPS
C_CHUNK = 128
N_GROUPS = 4
EXPERTS_PER_GROUP = 8
N_EXPERTS = N_GROUPS * EXPERTS_PER_GROUP
TOP_K = 2
EXPERT_FF = D_MODEL // 4
DISPATCH_BLOCK = 128

kernel_name = "hybrid_diffattn_gla_gmlp_hmoe"


def rmsnorm(x, g):
    xf = x.astype(jnp.float32)
    y = xf * lax.rsqrt(jnp.mean(xf * xf, axis=-1, keepdims=True) + EPS)
    return (y * g.astype(jnp.float32)).astype(x.dtype)


def layernorm(x, g, b):
    xf = x.astype(jnp.float32)
    mu = jnp.mean(xf, axis=-1, keepdims=True)
    var = jnp.mean(jnp.square(xf - mu), axis=-1, keepdims=True)
    y = (xf - mu) * lax.rsqrt(var + EPS)
    return (y * g.astype(jnp.float32) + b.astype(jnp.float32)).astype(x.dtype)


def t5_bucket(rel):
    n = jnp.maximum(rel, 0)
    nf = jnp.maximum(n, 1).astype(jnp.float32)
    large = MAX_EXACT + (jnp.log(nf / MAX_EXACT) / math.log(MAX_DISTANCE / MAX_EXACT)
                         * (N_BUCKETS - MAX_EXACT)).astype(jnp.int32)
    large = jnp.minimum(large, N_BUCKETS - 1)
    return jnp.where(n < MAX_EXACT, n, large)


def diff_attention(q, k, v, lam_vec, subln_g, rel_bias, lam_init):
    bsz, s_len = q.shape[0], q.shape[1]
    nb = s_len // Q_BLOCK
    lv = lam_vec.astype(jnp.float32)
    lam = jnp.exp(jnp.sum(lv[0] * lv[1])) - jnp.exp(jnp.sum(lv[2] * lv[3])) + lam_init
    scale = A_QK_DIM ** -0.5
    qb_all = (q.astype(jnp.float32) * scale).reshape(
        bsz, nb, Q_BLOCK, A_HEADS, 2, A_QK_DIM).transpose(1, 0, 3, 4, 2, 5)
    kf = k.astype(jnp.float32).transpose(0, 2, 3, 1, 4)
    vf = v.astype(jnp.float32).transpose(0, 2, 1, 3)
    tbl = rel_bias.astype(jnp.float32)
    k_pos = jnp.arange(s_len, dtype=jnp.int32)

    def block(args):
        qb, i = args
        q_pos = i * Q_BLOCK + jnp.arange(Q_BLOCK, dtype=jnp.int32)
        rel = q_pos[:, None] - k_pos[None, :]
        bias = tbl[t5_bucket(rel)].transpose(2, 0, 1)
        s = jnp.einsum('bhmqd,bhmkd->bhmqk', qb, kf) + bias[None, :, None]
        s = jnp.where((rel >= 0)[None, None, None], s, -jnp.inf)
        p = jax.nn.softmax(s, axis=-1)
        w = p[:, :, 0] - lam * p[:, :, 1]
        return jnp.einsum('bhqk,bhkd->bhqd', w, vf)

    o = lax.map(block, (qb_all, jnp.arange(nb, dtype=jnp.int32)))
    o = o.transpose(1, 0, 3, 2, 4).reshape(bsz, s_len, A_HEADS, A_V_DIM)
    o = rmsnorm(o, subln_g) * (1.0 - lam_init)
    return o.reshape(bsz, s_len, A_WIDTH)


def gla(q, k, v, g_log):
    bsz, s_len = q.shape[0], q.shape[1]
    nc = s_len // GLA_CHUNK

    def to_chunks(t):
        return t.astype(jnp.float32).reshape(
            bsz, nc, GLA_CHUNK, B_HEADS, t.shape[-1]).transpose(1, 0, 3, 2, 4)

    qc = to_chunks(q) * (B_KEY_DIM ** -0.5)
    kc, vc, gc = to_chunks(k), to_chunks(v), to_chunks(g_log)
    causal = jnp.tril(jnp.ones((GLA_CHUNK, GLA_CHUNK), dtype=bool))

    def chunk(state, inp):
        qi, ki, vi, gi = inp
        b = jnp.cumsum(gi, axis=2)
        o_inter = jnp.einsum('bhcd,bhde->bhce', qi * jnp.exp(b), state)
        diff = b[:, :, :, None, :] - b[:, :, None, :, :]
        decay = jnp.exp(jnp.where(causal[None, None, :, :, None], diff, -jnp.inf))
        att = jnp.einsum('bhid,bhjd,bhijd->bhij', qi, ki, decay)
        o_intra = jnp.einsum('bhij,bhje->bhie', att, vi)
        b_last = b[:, :, -1:, :]
        new_state = (jnp.exp(b_last[:, :, 0, :, None]) * state
                     + jnp.einsum('bhjd,bhje->bhde', ki * jnp.exp(b_last - b), vi))
        return new_state, o_inter + o_intra

    s0 = jnp.zeros((bsz, B_HEADS, B_KEY_DIM, B_VAL_DIM), jnp.float32)
    _, o = lax.scan(chunk, s0, (qc, kc, vc, gc))
    return o.transpose(1, 0, 3, 2, 4).reshape(bsz, s_len, B_HEADS, B_VAL_DIM)


def mix_ab(h, w_in, lam_vec, subln_g, w_gk, b_gk, gnorm_g, w_out, rel_bias, lam_init):
    bsz, s_len, _ = h.shape
    proj = h @ w_in
    cuts = [int(c) for c in np.cumsum(IN_AB_SIZES)[:-1]]
    qa, ka, va, qb, kb, vb, rb, gb = jnp.split(proj, cuts, axis=-1)
    o_a = diff_attention(qa.reshape(bsz, s_len, A_HEADS, 2, A_QK_DIM),
                         ka.reshape(bsz, s_len, A_HEADS, 2, A_QK_DIM),
                         va.reshape(bsz, s_len, A_HEADS, A_V_DIM),
                         lam_vec, subln_g, rel_bias, lam_init).astype(h.dtype)
    g_log = jax.nn.log_sigmoid((gb @ w_gk + b_gk).astype(jnp.float32)) / GATE_TEMP
    o_b = gla(qb.reshape(bsz, s_len, B_HEADS, B_KEY_DIM),
              kb.reshape(bsz, s_len, B_HEADS, B_KEY_DIM),
              vb.reshape(bsz, s_len, B_HEADS, B_VAL_DIM),
              g_log.reshape(bsz, s_len, B_HEADS, B_KEY_DIM))
    o_b = rmsnorm(o_b, gnorm_g).reshape(bsz, s_len, B_WIDTH).astype(h.dtype) * jax.nn.silu(rb)
    return jnp.concatenate([o_a, o_b], axis=-1) @ w_out


def mix_c(h, w_in, ln_g, ln_b, w_s, b_s, w_out):
    bsz, s_len, _ = h.shape
    nc = s_len // C_CHUNK
    u, v = jnp.split(jax.nn.gelu(h @ w_in), 2, axis=-1)
    v = layernorm(v, ln_g, ln_b).reshape(bsz, nc, C_CHUNK, C_GROUPS, C_GROUP_DIM)
    ws = jnp.where(jnp.tril(jnp.ones((C_CHUNK, C_CHUNK), dtype=bool))[None], w_s, 0.0)
    z = jnp.einsum('gts,bcsgd->bctgd', ws, v) + b_s.T[None, None, :, :, None]
    return (u * z.reshape(bsz, s_len, C_WIDTH)) @ w_out


def moe_ffn(h, w_rg, w_re, w_gate, w_up, w_down):
    n_tok = h.shape[0]
    grp_prob = jax.nn.softmax((h @ w_rg).astype(jnp.float32), axis=-1)
    p_grp, g_idx = lax.top_k(grp_prob, 1)
    e_logits = (h @ w_re).astype(jnp.float32).reshape(n_tok, N_GROUPS, EXPERTS_PER_GROUP)
    idx = jnp.broadcast_to(g_idx[:, :, None], (n_tok, 1, EXPERTS_PER_GROUP))
    e_logits = jnp.take_along_axis(e_logits, idx, axis=1)[:, 0]
    p_in, e_in = lax.top_k(jax.nn.softmax(e_logits, axis=-1), TOP_K)
    gates = p_grp * p_in / jnp.sum(p_in, axis=-1, keepdims=True)
    expert_id = g_idx * EXPERTS_PER_GROUP + e_in

    n_assign = n_tok * TOP_K
    flat_e = expert_id.reshape(-1).astype(jnp.int32)
    flat_tok = jnp.repeat(jnp.arange(n_tok, dtype=jnp.int32), TOP_K)
    order = jnp.argsort(flat_e)
    se, stok, sgate = flat_e[order], flat_tok[order], gates.reshape(-1)[order]
    counts = jnp.bincount(flat_e, length=N_EXPERTS)
    ends = jnp.cumsum(counts)
    starts = ends - counts
    first_blk = starts // DISPATCH_BLOCK
    last_blk = (ends - 1) // DISPATCH_BLOCK
    n_items = jnp.where(counts > 0, last_blk - first_blk + 1, 0)
    item_end = jnp.cumsum(n_items)
    item_start = item_end - n_items
    n_slots = -(-n_assign // DISPATCH_BLOCK) + N_EXPERTS
    slot = jnp.arange(n_slots, dtype=jnp.int32)
    item_e = jnp.minimum(jnp.searchsorted(item_end, slot, side='right'), N_EXPERTS - 1)
    item_valid = slot < item_end[-1]
    item_blk = first_blk[item_e] + slot - item_start[item_e]
    rows = item_blk[:, None] * DISPATCH_BLOCK + jnp.arange(DISPATCH_BLOCK, dtype=jnp.int32)[None]
    row_ok = item_valid[:, None] & (rows >= 0) & (rows < n_assign)
    rows = jnp.clip(rows, 0, n_assign - 1)
    row_ok = row_ok & (se[rows] == item_e[:, None])
    xs = h[stok]

    def run_item(args):
        e, r, ok = args
        xb = xs[r]
        act = jax.nn.silu(xb @ w_gate[e]) * (xb @ w_up[e])
        return jnp.where(ok[:, None], act @ w_down[e], 0.0).astype(h.dtype)

    y = lax.map(run_item, (item_e, rows, row_ok))
    y_sorted = jnp.zeros_like(xs).at[rows.reshape(-1)].add(y.reshape(-1, D_MODEL))
    return jax.ops.segment_sum(y_sorted * sgate[:, None].astype(h.dtype), stok,
                               num_segments=n_tok)


def setup_inputs(seed: int = 0) -> dict:
    key = jax.random.key(seed)
    ks = jax.random.split(key, 24)

    def nrm(k, shape, scale):
        return jax.random.normal(k, shape, jnp.float32) * scale

    return {
        "x": nrm(ks[0], (BATCH, SEQ, D_MODEL), 1.0),
        "rel_bias": nrm(ks[1], (N_BUCKETS, A_HEADS), 0.5),
        "norm_mix": 1.0 + nrm(ks[2], (DEPTH, D_MODEL), 0.02),
        "norm_ffn": 1.0 + nrm(ks[3], (DEPTH, D_MODEL), 0.02),
        "norm_final": 1.0 + nrm(ks[4], (D_MODEL,), 0.02),
        "w_in_ab": nrm(ks[5], (N_EVEN, D_MODEL, IN_AB), D_MODEL ** -0.5),
        "lam_ab": nrm(ks[6], (N_EVEN, 4, A_QK_DIM), 0.1),
        "subln_ab": 1.0 + nrm(ks[7], (N_EVEN, A_V_DIM), 0.02),
        "w_gk_ab": nrm(ks[8], (N_EVEN, GATE_RANK, B_HEADS * B_KEY_DIM), GATE_RANK ** -0.5),
        "b_gk_ab": nrm(ks[9], (N_EVEN, B_HEADS * B_KEY_DIM), 0.02),
        "gnorm_ab": 1.0 + nrm(ks[10], (N_EVEN, B_VAL_DIM), 0.02),
        "w_out_ab": nrm(ks[11], (N_EVEN, MIX_WIDTH, D_MODEL), MIX_WIDTH ** -0.5),
        "w_in_c": nrm(ks[12], (N_ODD, D_MODEL, 2 * C_WIDTH), D_MODEL ** -0.5),
        "ln_v_g": 1.0 + nrm(ks[13], (N_ODD, C_WIDTH), 0.02),
        "ln_v_b": nrm(ks[14], (N_ODD, C_WIDTH), 0.02),
        "w_spatial": nrm(ks[15], (N_ODD, C_GROUPS, C_CHUNK, C_CHUNK), C_CHUNK ** -0.5),
        "b_spatial": 1.0 + nrm(ks[16], (N_ODD, C_GROUPS, C_CHUNK), 0.02),
        "w_out_c": nrm(ks[17], (N_ODD, C_WIDTH, D_MODEL), C_WIDTH ** -0.5),
        "w_router_group": nrm(ks[18], (DEPTH, D_MODEL, N_GROUPS), D_MODEL ** -0.5),
        "w_router_expert": nrm(ks[19], (DEPTH, D_MODEL, N_EXPERTS), D_MODEL ** -0.5),
        "w_exp_gate": nrm(ks[20], (DEPTH, N_EXPERTS, D_MODEL, EXPERT_FF), D_MODEL ** -0.5),
        "w_exp_up": nrm(ks[21], (DEPTH, N_EXPERTS, D_MODEL, EXPERT_FF), D_MODEL ** -0.5),
        "w_exp_down": nrm(ks[22], (DEPTH, N_EXPERTS, EXPERT_FF, D_MODEL), EXPERT_FF ** -0.5),
    }


def reference(x, rel_bias, norm_mix, norm_ffn, norm_final, w_in_ab, lam_ab, subln_ab,
              w_gk_ab, b_gk_ab, gnorm_ab, w_out_ab, w_in_c, ln_v_g, ln_v_b, w_spatial,
              b_spatial, w_out_c, w_router_group, w_router_expert, w_exp_gate, w_exp_up,
              w_exp_down):
    bsz, s_len, d = x.shape
    for layer in range(DEPTH):
        h = rmsnorm(x, norm_mix[layer])
        i = layer // 2
        if layer % 2 == 0:
            lam_init = 0.8 - 0.6 * math.exp(-0.3 * layer)
            x = x + mix_ab(h, w_in_ab[i], lam_ab[i], subln_ab[i], w_gk_ab[i], b_gk_ab[i],
                           gnorm_ab[i], w_out_ab[i], rel_bias, lam_init)
        else:
            x = x + mix_c(h, w_in_c[i], ln_v_g[i], ln_v_b[i], w_spatial[i], b_spatial[i],
                          w_out_c[i])
        h = rmsnorm(x, norm_ffn[layer]).reshape(bsz * s_len, d)
        x = x + moe_ffn(h, w_router_group[layer], w_router_expert[layer], w_exp_gate[layer],
                        w_exp_up[layer], w_exp_down[layer]).reshape(bsz, s_len, d)
    return rmsnorm(x, norm_final)
```

```python
import functools
import math

import jax
import jax.numpy as jnp
import numpy as np
from jax import lax
from jax.experimental import pallas as pl
from jax.experimental.pallas import tpu as pltpu

F32 = jnp.float32
BF16 = jnp.bfloat16
I32 = jnp.int32
HIGHEST = lax.Precision.HIGHEST

EPS = 1e-6
NEG = -1e30
A_HEADS = 8
A_QK_DIM = 64
A_V_DIM = 128
A_WIDTH = A_HEADS * A_V_DIM
N_BUCKETS = 32
MAX_EXACT = 16
MAX_DISTANCE = 128
B_HEADS = 4
B_KEY_DIM = 128
B_VAL_DIM = 256
B_WIDTH = B_HEADS * B_VAL_DIM
GATE_RANK = 16
GATE_TEMP = 16.0
GLA_CHUNK = 64
C_GROUPS = 8
C_CHUNK = 128
N_GROUPS = 4
EXPERTS_PER_GROUP = 8
N_EXPERTS = N_GROUPS * EXPERTS_PER_GROUP
TOP_K = 2
LANES = 128
ROUTER_LANE0 = N_GROUPS
MOE_BLOCK = 256
VMEM_LIMIT = 56 * 1024 * 1024

NT_DIMS = (((1,), (1,)), ((), ()))
TN_DIMS = (((0,), (0,)), ((), ()))


def _t5_thresholds():
    n = np.arange(0, 2 * MAX_DISTANCE)
    nf = np.maximum(n, 1).astype(np.float64)
    large = MAX_EXACT + (np.log(nf / MAX_EXACT) / math.log(MAX_DISTANCE / MAX_EXACT)
                         * (N_BUCKETS - MAX_EXACT)).astype(np.int64)
    bucket = np.where(n < MAX_EXACT, n, np.minimum(large, N_BUCKETS - 1))
    return [int(np.min(n[bucket >= b])) for b in range(N_BUCKETS)]


T5_THRESHOLDS = _t5_thresholds()


def _params(*sem):
    return pltpu.CompilerParams(dimension_semantics=sem, vmem_limit_bytes=VMEM_LIMIT)


def _rms(x, g):
    ms = jnp.mean(x * x, axis=-1, keepdims=True)
    return x * lax.rsqrt(ms + EPS) * g


def _gelu_tanh(x):
    c = math.sqrt(2.0 / math.pi)
    return 0.5 * x * (1.0 + jnp.tanh(c * (x + 0.044715 * (x * x * x))))


def _sigmoid(x):
    return 1.0 / (1.0 + jnp.exp(-x))


def _norm_proj_kernel(*refs, act, has_extra):
    if has_extra:
        x_ref, g_ref, w_ref, we_ref, o_ref, oe_ref, h_scr = refs
    else:
        x_ref, g_ref, w_ref, o_ref, h_scr = refs

    @pl.when(pl.program_id(1) == 0)
    def _():
        hb = _rms(x_ref[...], g_ref[...]).astype(BF16)
        h_scr[...] = hb
        if has_extra:
            oe_ref[...] = jnp.dot(hb, we_ref[...].astype(BF16), preferred_element_type=F32)

    y = jnp.dot(h_scr[...], w_ref[...].astype(BF16), preferred_element_type=F32)
    if act == "gelu":
        y = _gelu_tanh(y)
    o_ref[...] = y.astype(o_ref.dtype)


def norm_proj(x, g, w, n_cols, *, act=None, w_extra=None, tm=512, tn=512):
    t, d = x.shape
    tm = min(tm, t)
    grid = (t // tm, n_cols // tn)
    in_specs = [pl.BlockSpec((tm, d), lambda i, j: (i, 0)),
                pl.BlockSpec((1, d), lambda i, j: (0, 0)),
                pl.BlockSpec((d, tn), lambda i, j: (0, j))]
    out_specs = [pl.BlockSpec((tm, tn), lambda i, j: (i, j))]
    out_shape = [jax.ShapeDtypeStruct((t, n_cols), BF16)]
    args = [x, g.reshape(1, d), w]
    if w_extra is not None:
        ne = w_extra.shape[1]
        in_specs.append(pl.BlockSpec((d, ne), lambda i, j: (0, 0)))
        out_specs.append(pl.BlockSpec((tm, ne), lambda i, j: (i, 0)))
        out_shape.append(jax.ShapeDtypeStruct((t, ne), F32))
        args.append(w_extra)
    outs = pl.pallas_call(
        functools.partial(_norm_proj_kernel, act=act, has_extra=w_extra is not None),
        grid=grid, in_specs=in_specs, out_specs=out_specs, out_shape=out_shape,
        scratch_shapes=[pltpu.VMEM((tm, d), BF16)],
        compiler_params=_params("parallel", "arbitrary"),
        name="norm_proj_" + (act or "lin"),
    )(*args)
    return outs if w_extra is not None else outs[0]


def _out_proj_kernel(a0_ref, a1_ref, w_ref, r_ref, o_ref):
    kh = a0_ref.shape[1]
    w = w_ref[...].astype(BF16)
    y = jnp.dot(a0_ref[...], w[:kh], preferred_element_type=F32)
    y = y + jnp.dot(a1_ref[...], w[kh:], preferred_element_type=F32)
    o_ref[...] = r_ref[...] + y


def out_proj(a0, a0_blk, a1, a1_blk, w, res, *, tm=512, tn=512):
    t, n = res.shape
    k = w.shape[0]
    kh = k // 2
    tm = min(tm, t)
    return pl.pallas_call(
        _out_proj_kernel,
        grid=(t // tm, n // tn),
        in_specs=[pl.BlockSpec((tm, kh), lambda i, j: (i, a0_blk)),
                  pl.BlockSpec((tm, kh), lambda i, j: (i, a1_blk)),
                  pl.BlockSpec((k, tn), lambda i, j: (0, j)),
                  pl.BlockSpec((tm, tn), lambda i, j: (i, j))],
        out_specs=pl.BlockSpec((tm, tn), lambda i, j: (i, j)),
        out_shape=jax.ShapeDtypeStruct((t, n), F32),
        compiler_params=_params("parallel", "arbitrary"),
        name="out_proj",
    )(a0, a1, w, res)


def _diff_attn_kernel(tbl_ref, q_ref, k_ref, v_ref, lam_ref, sg_ref, o_ref,
                      m_scr, l_scr, acc_scr, bias_scr, *, t, lam_init):
    h = pl.program_id(1)
    qi = pl.program_id(2)
    ki = pl.program_id(3)
    scale = A_QK_DIM ** -0.5

    @pl.when((qi == 0) & (ki == 0))
    def _build_bias():
        row = lax.broadcasted_iota(I32, (t, t), 0)
        col = lax.broadcasted_iota(I32, (t, t), 1)
        for d in range(2):
            rel = row - col + d * t
            acc = jnp.full((t, t), tbl_ref[h], F32)
            for b in range(1, N_BUCKETS):
                acc = jnp.where(rel >= T5_THRESHOLDS[b], tbl_ref[b * A_HEADS + h], acc)
            if d == 0:
                acc = jnp.where(rel >= 0, acc, NEG)
            bias_scr[d] = acc

    @pl.when(ki == 0)
    def _init():
        m_scr[...] = jnp.full(m_scr.shape, NEG, F32)
        l_scr[...] = jnp.zeros(l_scr.shape, F32)
        acc_scr[...] = jnp.zeros(acc_scr.shape, F32)

    def step(bias):
        q = q_ref[...]
        k = k_ref[...]
        v = v_ref[...]
        lane = lax.broadcasted_iota(I32, q.shape, 1)
        for m in range(2):
            sel = (lane < A_QK_DIM) if m == 0 else (lane >= A_QK_DIM)
            qm = jnp.where(sel, q, jnp.zeros_like(q))
            s = lax.dot_general(qm, k, NT_DIMS, preferred_element_type=F32) * scale + bias
            m_old = m_scr[m]
            m_new = jnp.maximum(m_old, jnp.max(s, axis=-1, keepdims=True))
            alpha = jnp.exp(m_old - m_new)
            p = jnp.exp(s - m_new)
            l_scr[m] = alpha * l_scr[m] + jnp.sum(p, axis=-1, keepdims=True)
            acc_scr[m] = alpha * acc_scr[m] + jnp.dot(p.astype(BF16), v,
                                                      preferred_element_type=F32)
            m_scr[m] = m_new

    @pl.when(ki < qi - 1)
    def _far():
        step(tbl_ref[(N_BUCKETS - 1) * A_HEADS + h])

    @pl.when(ki == qi - 1)
    def _near():
        step(bias_scr[1])

    @pl.when(ki == qi)
    def _diag():
        step(bias_scr[0])
        lv = lam_ref[...]
        s1 = jnp.sum(lv[0:1] * lv[1:2], axis=-1, keepdims=True)
        s2 = jnp.sum(lv[2:3] * lv[3:4], axis=-1, keepdims=True)
        lam = jnp.exp(s1) - jnp.exp(s2) + lam_init
        o = acc_scr[0] / l_scr[0] - lam * (acc_scr[1] / l_scr[1])
        o = _rms(o, sg_ref[...]) * (1.0 - lam_init)
        o_ref[...] = o.astype(o_ref.dtype)


def diff_attention(proj, rel_bias, lam_vec, subln_g, lam_init, bsz, s_len, *, t=512):
    t = min(t, s_len)
    assert t >= MAX_DISTANCE and s_len % t == 0
    nq = s_len // t
    kv_map = lambda off: (lambda b, h, qi, ki: (b * nq + jnp.minimum(ki, qi), off + h))
    return pl.pallas_call(
        functools.partial(_diff_attn_kernel, t=t, lam_init=lam_init),
        grid=(bsz, A_HEADS, nq, nq),
        in_specs=[pl.BlockSpec(memory_space=pltpu.SMEM),
                  pl.BlockSpec((t, A_V_DIM), lambda b, h, qi, ki: (b * nq + qi, h)),
                  pl.BlockSpec((t, A_V_DIM), kv_map(A_HEADS)),
                  pl.BlockSpec((t, A_V_DIM), kv_map(2 * A_HEADS)),
                  pl.BlockSpec((4, A_QK_DIM), lambda b, h, qi, ki: (0, 0)),
                  pl.BlockSpec((1, A_V_DIM), lambda b, h, qi, ki: (0, 0))],
        out_specs=pl.BlockSpec((t, A_V_DIM), lambda b, h, qi, ki: (b * nq + qi, h)),
        out_shape=jax.ShapeDtypeStruct((bsz * s_len, A_WIDTH), BF16),
        scratch_shapes=[pltpu.VMEM((2, t, 1), F32), pltpu.VMEM((2, t, 1), F32),
                        pltpu.VMEM((2, t, A_V_DIM), F32), pltpu.VMEM((2, t, t), F32)],
        compiler_params=_params("parallel", "arbitrary", "arbitrary", "arbitrary"),
        name="diff_attention",
    )(rel_bias.reshape(-1), proj, proj, proj, lam_vec, subln_g.reshape(1, A_V_DIM))


def _gla_kernel(q_ref, k_ref, v_ref, r_ref, gb_ref, wgk_ref, bgk_ref, gn_ref, o_ref, st_scr, *, rows):
    c_len = GLA_CHUNK
    scale = B_KEY_DIM ** -0.5

    @pl.when(pl.program_id(1) == 0)
    def _():
        st_scr[...] = jnp.zeros(st_scr.shape, F32)

    z = jnp.dot(gb_ref[...], wgk_ref[...], preferred_element_type=F32, precision=HIGHEST)
    z = z + bgk_ref[...]
    g_log = (jnp.minimum(z, 0.0) - jnp.log(1.0 + jnp.exp(-jnp.abs(z)))) * (1.0 / GATE_TEMP)
    row = lax.broadcasted_iota(I32, (rows, rows), 0)
    col = lax.broadcasted_iota(I32, (rows, rows), 1)
    ltri = ((row >= col) & ((row // c_len) == (col // c_len))).astype(F32)
    bcum = jnp.dot(ltri, g_log, preferred_element_type=F32, precision=HIGHEST)
    causal = (lax.broadcasted_iota(I32, (c_len, c_len), 0)
              >= lax.broadcasted_iota(I32, (c_len, c_len), 1))
    gn = gn_ref[...]

    for c in range(rows // c_len):
        rs = slice(c * c_len, (c + 1) * c_len)
        for hh in range(B_HEADS):
            ks = slice(hh * B_KEY_DIM, (hh + 1) * B_KEY_DIM)
            vs = slice(hh * B_VAL_DIM, (hh + 1) * B_VAL_DIM)
            b = bcum[rs, ks]
            b_last = b[c_len - 1:c_len, :]
            b_mid = b[c_len // 2:c_len // 2 + 1, :]
            q = q_ref[rs, ks].astype(F32) * scale
            k = k_ref[rs, ks].astype(F32)
            v = v_ref[rs, vs]
            st = st_scr[hh]
            q_in = (q * jnp.exp(b)).astype(BF16)
            q_ia = (q * jnp.exp(b - b_mid)).astype(BF16)
            k_ia = (k * jnp.exp(b_mid - b)).astype(BF16)
            att = lax.dot_general(q_ia, k_ia, NT_DIMS, preferred_element_type=F32)
            att = jnp.where(causal, att, 0.0)
            o = jnp.dot(att.astype(BF16), v, preferred_element_type=F32)
            o = o + lax.dot_general(q_in, st.astype(BF16), NT_DIMS, preferred_element_type=F32)
            k_st = (k * jnp.exp(b_last - b)).astype(BF16)
            st_scr[hh] = st * jnp.exp(b_last) + lax.dot_general(
                v, k_st, TN_DIMS, preferred_element_type=F32)
            on = _rms(o, gn)
            r = r_ref[rs, vs].astype(F32)
            o_ref[rs, vs] = (on * (r * _sigmoid(r))).astype(o_ref.dtype)


def gla_mixer(proj, gb, w_gk, b_gk, gnorm_g, bsz, s_len, *, rows=256):
    rows = min(rows, s_len)
    nr = s_len // rows
    kw = B_HEADS * B_KEY_DIM
    rmap = lambda blk: (lambda b, i: (b * nr + i, blk))
    return pl.pallas_call(
        functools.partial(_gla_kernel, rows=rows),
        grid=(bsz, nr),
        in_specs=[pl.BlockSpec((rows, kw), rmap(3 * A_WIDTH // kw)),
                  pl.BlockSpec((rows, kw), rmap(3 * A_WIDTH // kw + 1)),
                  pl.BlockSpec((rows, B_WIDTH), rmap((3 * A_WIDTH + 2 * kw) // B_WIDTH)),
                  pl.BlockSpec((rows, B_WIDTH), rmap((3 * A_WIDTH + 2 * kw) // B_WIDTH + 1)),
                  pl.BlockSpec((rows, GATE_RANK), rmap(0)),
                  pl.BlockSpec((GATE_RANK, kw), lambda b, i: (0, 0)),
                  pl.BlockSpec((1, kw), lambda b, i: (0, 0)),
                  pl.BlockSpec((1, B_VAL_DIM), lambda b, i: (0, 0))],
        out_specs=pl.BlockSpec((rows, B_WIDTH), rmap(0)),
        out_shape=jax.ShapeDtypeStruct((bsz * s_len, B_WIDTH), BF16),
        scratch_shapes=[pltpu.VMEM((B_HEADS, B_VAL_DIM, B_KEY_DIM), F32)],
        compiler_params=_params("parallel", "arbitrary"),
        name="gla",
    )(proj, proj, proj, proj, gb, w_gk, b_gk.reshape(1, kw), gnorm_g.reshape(1, B_VAL_DIM))


def _spatial_kernel(u_ref, v_ref, lg_ref, lb_ref, ws_ref, bs_ref, o_ref, *, rows):
    v = v_ref[...].astype(F32)
    mu = jnp.mean(v, axis=-1, keepdims=True)
    vc = v - mu
    var = jnp.mean(vc * vc, axis=-1, keepdims=True)
    vn = (vc * lax.rsqrt(var + EPS) * lg_ref[...] + lb_ref[...]).astype(BF16)
    gd = v.shape[1] // C_GROUPS
    mask = (lax.broadcasted_iota(I32, (C_CHUNK, C_CHUNK), 0)
            >= lax.broadcasted_iota(I32, (C_CHUNK, C_CHUNK), 1))
    for g in range(C_GROUPS):
        w = jnp.where(mask, ws_ref[g], 0.0).astype(BF16)
        bias = bs_ref[:, g:g + 1]
        cs = slice(g * gd, (g + 1) * gd)
        for c in range(rows // C_CHUNK):
            rs = slice(c * C_CHUNK, (c + 1) * C_CHUNK)
            z = jnp.dot(w, vn[rs, cs], preferred_element_type=F32) + bias
            o_ref[rs, cs] = (u_ref[rs, cs].astype(F32) * z).astype(o_ref.dtype)


def spatial_gate(proj, ln_g, ln_b, w_s, b_s, *, rows=512):
    t = proj.shape[0]
    cw = proj.shape[1] // 2
    rows = min(rows, t)
    return pl.pallas_call(
        functools.partial(_spatial_kernel, rows=rows),
        grid=(t // rows,),
        in_specs=[pl.BlockSpec((rows, cw), lambda i: (i, 0)),
                  pl.BlockSpec((rows, cw), lambda i: (i, 1)),
                  pl.BlockSpec((1, cw), lambda i: (0, 0)),
                  pl.BlockSpec((1, cw), lambda i: (0, 0)),
                  pl.BlockSpec((C_GROUPS, C_CHUNK, C_CHUNK), lambda i: (0, 0, 0)),
                  pl.BlockSpec((C_CHUNK, C_GROUPS), lambda i: (0, 0))],
        out_specs=pl.BlockSpec((rows, cw), lambda i: (i, 0)),
        out_shape=jax.ShapeDtypeStruct((t, cw), BF16),
        compiler_params=_params("parallel"),
        name="spatial_gate",
    )(proj, proj, ln_g.reshape(1, cw), ln_b.reshape(1, cw), w_s, b_s.T)


def _router_kernel(x_ref, g_ref, wr_ref, h_ref, ri_ref, rg_ref, cnt_ref, carry_scr, *, tr):
    @pl.when(pl.program_id(0) == 0)
    def _():
        carry_scr[...] = jnp.zeros(carry_scr.shape, F32)

    h = _rms(x_ref[...], g_ref[...])
    h_ref[...] = h
    logits = jnp.dot(h, wr_ref[...], preferred_element_type=F32, precision=HIGHEST)
    lane = lax.broadcasted_iota(I32, (tr, LANES), 1)
    lanef = lane.astype(F32)
    big = float(LANES)

    is_g = lane < N_GROUPS
    lg = jnp.where(is_g, logits, NEG)
    gmax = jnp.max(lg, axis=-1, keepdims=True)
    p_grp = 1.0 / jnp.sum(jnp.exp(lg - gmax), axis=-1, keepdims=True)
    gidx = jnp.min(jnp.where(is_g & (lg == gmax), lanef, big), axis=-1, keepdims=True)

    lo = ROUTER_LANE0 + EXPERTS_PER_GROUP * gidx
    in_e = (lanef >= lo) & (lanef < lo + EXPERTS_PER_GROUP)
    le = jnp.where(in_e, logits, NEG)
    emax = jnp.max(le, axis=-1, keepdims=True)
    i1 = jnp.min(jnp.where(in_e & (le == emax), lanef, big), axis=-1, keepdims=True)
    oh1 = lanef == i1
    le2 = jnp.where(oh1, NEG, le)
    emax2 = jnp.max(le2, axis=-1, keepdims=True)
    i2 = jnp.min(jnp.where(in_e & (le2 == emax2) & (~oh1), lanef, big), axis=-1, keepdims=True)
    oh2 = lanef == i2
    ratio = jnp.exp(emax2 - emax)
    g1 = p_grp / (1.0 + ratio)
    g2 = p_grp * ratio / (1.0 + ratio)

    onehot = (oh1 | oh2).astype(BF16)
    tri = (lax.broadcasted_iota(I32, (tr, tr), 0) > lax.broadcasted_iota(I32, (tr, tr), 1))
    before = jnp.dot(tri.astype(BF16), onehot, preferred_element_type=F32) + carry_scr[...]
    rank1 = jnp.sum(jnp.where(oh1, before, 0.0), axis=-1, keepdims=True)
    rank2 = jnp.sum(jnp.where(oh2, before, 0.0), axis=-1, keepdims=True)
    carry = carry_scr[...] + jnp.sum(onehot.astype(F32), axis=0, keepdims=True)
    carry_scr[...] = carry

    e1 = i1 - ROUTER_LANE0
    e2 = i2 - ROUTER_LANE0
    ri = jnp.where(lane == 0, e1, jnp.where(lane == 1, e2,
                                             jnp.where(lane == 2, rank1,
                                                       jnp.where(lane == 3, rank2, 0.0))))
    ri_ref[...] = ri.astype(I32)
    rg_ref[...] = jnp.where(lane == 0, g1, jnp.where(lane == 1, g2, 0.0))
    cnt_ref[...] = jnp.broadcast_to(carry, cnt_ref.shape).astype(I32)


def moe_router(x, g, w_rg, w_re, *, tr=512):
    t, d = x.shape
    tr = min(tr, t)
    w_r = jnp.zeros((d, LANES), F32)
    w_r = w_r.at[:, :N_GROUPS].set(w_rg).at[:, ROUTER_LANE0:ROUTER_LANE0 + N_EXPERTS].set(w_re)
    return pl.pallas_call(
        functools.partial(_router_kernel, tr=tr),
        grid=(t // tr,),
        in_specs=[pl.BlockSpec((tr, d), lambda i: (i, 0)),
                  pl.BlockSpec((1, d), lambda i: (0, 0)),
                  pl.BlockSpec((d, LANES), lambda i: (0, 0))],
        out_specs=[pl.BlockSpec((tr, d), lambda i: (i, 0)),
                   pl.BlockSpec((tr, LANES), lambda i: (i, 0)),
                   pl.BlockSpec((tr, LANES), lambda i: (i, 0)),
                   pl.BlockSpec((8, LANES), lambda i: (0, 0))],
        out_shape=[jax.ShapeDtypeStruct((t, d), F32),
                   jax.ShapeDtypeStruct((t, LANES), I32),
                   jax.ShapeDtypeStruct((t, LANES), F32),
                   jax.ShapeDtypeStruct((8, LANES), I32)],
        scratch_shapes=[pltpu.VMEM((1, LANES), F32)],
        compiler_params=_params("arbitrary"),
        name="moe_router",
    )(x, g.reshape(1, d), w_r)


def _row_pos(e_ref, rank_ref, pstart_ref, a):
    return pstart_ref[e_ref[a]] + rank_ref[a]


def _dispatch_kernel(e_ref, rank_ref, pstart_ref, h_ref, xs_in_ref, xs_ref, sem, *, td):
    del xs_in_ref
    base = pl.program_id(0) * td

    def row_copy(r, pos):
        return pltpu.make_async_copy(h_ref.at[pl.ds(r, 1)], xs_ref.at[pl.ds(pos, 1)], sem)

    def issue(r, carry):
        for kk in range(TOP_K):
            row_copy(r, _row_pos(e_ref, rank_ref, pstart_ref, (base + r) * TOP_K + kk)).start()
        return carry

    def drain(r, carry):
        for kk in range(TOP_K):
            row_copy(r, _row_pos(e_ref, rank_ref, pstart_ref, (base + r) * TOP_K + kk)).wait()
        return carry

    lax.fori_loop(0, td, issue, 0)
    lax.fori_loop(0, td, drain, 0)


def moe_dispatch(h, e_flat, rank_flat, pad_start, n_rows, *, td=256):
    t, d = h.shape
    td = min(td, t)
    grid_spec = pltpu.PrefetchScalarGridSpec(
        num_scalar_prefetch=3, grid=(t // td,),
        in_specs=[pl.BlockSpec((td, d), lambda i, *_: (i, 0)),
                  pl.BlockSpec(memory_space=pl.ANY)],
        out_specs=pl.BlockSpec(memory_space=pl.ANY),
        scratch_shapes=[pltpu.SemaphoreType.DMA(())])
    return pl.pallas_call(
        functools.partial(_dispatch_kernel, td=td),
        grid_spec=grid_spec,
        out_shape=jax.ShapeDtypeStruct((n_rows, d), F32),
        input_output_aliases={4: 0},
        compiler_params=_params("arbitrary"),
        name="moe_dispatch",
    )(e_flat, rank_flat, pad_start, h, jnp.zeros((n_rows, d), F32))


def _expert_kernel(blk_e_ref, nused_ref, xs_ref, wg_ref, wu_ref, wd_ref, y_ref, wg_s, wu_s, wd_s):
    j = pl.program_id(0)
    used = j < nused_ref[0]
    new_expert = (j == 0) | (blk_e_ref[j] != blk_e_ref[jnp.maximum(j - 1, 0)])

    @pl.when(used & new_expert)
    def _():
        wg_s[...] = wg_ref[0].astype(BF16)
        wu_s[...] = wu_ref[0].astype(BF16)
        wd_s[...] = wd_ref[0].astype(BF16)

    @pl.when(used)
    def _():
        x = xs_ref[...].astype(BF16)
        gate = jnp.dot(x, wg_s[...], preferred_element_type=F32)
        up = jnp.dot(x, wu_s[...], preferred_element_type=F32)
        act = (gate * _sigmoid(gate) * up).astype(BF16)
        y_ref[...] = jnp.dot(act, wd_s[...], preferred_element_type=F32)

    @pl.when(jnp.logical_not(used))
    def _():
        y_ref[...] = jnp.zeros(y_ref.shape, F32)


def moe_experts(xs, blk_e, n_used, w_gate, w_up, w_down):
    n_rows, d = xs.shape
    ff = w_gate.shape[2]
    nb = n_rows // MOE_BLOCK
    row_map = lambda j, be, nu: (jnp.minimum(j, nu[0] - 1), 0)
    grid_spec = pltpu.PrefetchScalarGridSpec(
        num_scalar_prefetch=2, grid=(nb,),
        in_specs=[pl.BlockSpec((MOE_BLOCK, d), row_map),
                  pl.BlockSpec((1, d, ff), lambda j, be, nu: (be[j], 0, 0)),
                  pl.BlockSpec((1, d, ff), lambda j, be, nu: (be[j], 0, 0)),
                  pl.BlockSpec((1, ff, d), lambda j, be, nu: (be[j], 0, 0))],
        out_specs=pl.BlockSpec((MOE_BLOCK, d), lambda j, be, nu: (j, 0)),
        scratch_shapes=[pltpu.VMEM((d, ff), BF16), pltpu.VMEM((d, ff), BF16),
                        pltpu.VMEM((ff, d), BF16)])
    return pl.pallas_call(
        _expert_kernel,
        grid_spec=grid_spec,
        out_shape=jax.ShapeDtypeStruct((n_rows, d), F32),
        compiler_params=_params("arbitrary"),
        name="moe_experts",
    )(blk_e, n_used, xs, w_gate, w_up, w_down)


def _combine_kernel(e_ref, rank_ref, pstart_ref, x_ref, rg_ref, gf_ref, y_ref, o_ref, ybuf, sem,
                    *, tc, final_norm):
    base = pl.program_id(0) * tc

    def row_copy(r, kk, pos):
        return pltpu.make_async_copy(y_ref.at[pl.ds(pos, 1)], ybuf.at[kk, pl.ds(r, 1)], sem)

    def issue(r, carry):
        for kk in range(TOP_K):
            row_copy(r, kk, _row_pos(e_ref, rank_ref, pstart_ref, (base + r) * TOP_K + kk)).start()
        return carry

    def drain(r, carry):
        for kk in range(TOP_K):
            row_copy(r, kk, _row_pos(e_ref, rank_ref, pstart_ref, (base + r) * TOP_K + kk)).wait()
        return carry

    lax.fori_loop(0, tc, issue, 0)
    lax.fori_loop(0, tc, drain, 0)
    gates = rg_ref[...]
    out = x_ref[...] + (gates[:, 0:1] * ybuf[0] + gates[:, 1:2] * ybuf[1])
    if final_norm:
        out = _rms(out, gf_ref[...])
    o_ref[...] = out


def moe_combine(x, y, gates, e_flat, rank_flat, pad_start, g_final, final_norm, *, tc=256):
    t, d = x.shape
    tc = min(tc, t)
    grid_spec = pltpu.PrefetchScalarGridSpec(
        num_scalar_prefetch=3, grid=(t // tc,),
        in_specs=[pl.BlockSpec((tc, d), lambda i, *_: (i, 0)),
                  pl.BlockSpec((tc, LANES), lambda i, *_: (i, 0)),
                  pl.BlockSpec((1, d), lambda i, *_: (0, 0)),
                  pl.BlockSpec(memory_space=pl.ANY)],
        out_specs=pl.BlockSpec((tc, d), lambda i, *_: (i, 0)),
        scratch_shapes=[pltpu.VMEM((TOP_K, tc, d), F32), pltpu.SemaphoreType.DMA(())])
    return pl.pallas_call(
        functools.partial(_combine_kernel, tc=tc, final_norm=final_norm),
        grid_spec=grid_spec,
        out_shape=jax.ShapeDtypeStruct((t, d), F32),
        compiler_params=_params("arbitrary"),
        name="moe_combine",
    )(e_flat, rank_flat, pad_start, x, gates, g_final.reshape(1, d), y)


def moe_layer(x, g_norm, w_rg, w_re, w_gate, w_up, w_down, g_final, final_norm):
    t, d = x.shape
    h, ri, gates, cnt = moe_router(x, g_norm, w_rg, w_re)
    e_flat = ri[:, 0:TOP_K].reshape(-1)
    rank_flat = ri[:, TOP_K:2 * TOP_K].reshape(-1)
    counts = cnt[0, ROUTER_LANE0:ROUTER_LANE0 + N_EXPERTS]
    nblk = (counts + MOE_BLOCK - 1) // MOE_BLOCK
    blk_end = jnp.cumsum(nblk)
    pad_start = ((blk_end - nblk) * MOE_BLOCK).astype(I32)
    n_used = blk_end[-1:].astype(I32)
    nb = (t * TOP_K) // MOE_BLOCK + N_EXPERTS
    blk = jnp.minimum(jnp.arange(nb, dtype=I32), n_used[0] - 1)
    blk_e = jnp.searchsorted(blk_end, blk, side="right").astype(I32)
    xs = moe_dispatch(h, e_flat, rank_flat, pad_start, nb * MOE_BLOCK)
    y = moe_experts(xs, blk_e, n_used, w_gate, w_up, w_down)
    return moe_combine(x, y, gates, e_flat, rank_flat, pad_start, g_final, final_norm)


def kernel(x, rel_bias, norm_mix, norm_ffn, norm_final, w_in_ab, lam_ab, subln_ab, w_gk_ab, b_gk_ab,
           gnorm_ab, w_out_ab, w_in_c, ln_v_g, ln_v_b, w_spatial, b_spatial, w_out_c,
           w_router_group, w_router_expert, w_exp_gate, w_exp_up, w_exp_down):
    bsz, s_len, d = x.shape
    depth = norm_mix.shape[0]
    xf = x.reshape(bsz * s_len, d)
    for layer in range(depth):
        i = layer // 2
        if layer % 2 == 0:
            lam_init = 0.8 - 0.6 * math.exp(-0.3 * layer)
            n_main = 3 * A_WIDTH + 2 * B_HEADS * B_KEY_DIM + 2 * B_WIDTH
            proj, gb = norm_proj(xf, norm_mix[layer], w_in_ab[i], n_main,
                                 w_extra=w_in_ab[i][:, n_main:])
            o_a = diff_attention(proj, rel_bias, lam_ab[i], subln_ab[i], lam_init, bsz, s_len)
            o_b = gla_mixer(proj, gb, w_gk_ab[i], b_gk_ab[i], gnorm_ab[i], bsz, s_len)
            xf = out_proj(o_a, 0, o_b, 0, w_out_ab[i], xf)
        else:
            proj = norm_proj(xf, norm_mix[layer], w_in_c[i], w_in_c.shape[2], act="gelu")
            uz = spatial_gate(proj, ln_v_g[i], ln_v_b[i], w_spatial[i], b_spatial[i])
            xf = out_proj(uz, 0, uz, 1, w_out_c[i], xf)
        xf = moe_layer(xf, norm_ffn[layer], w_router_group[layer], w_router_expert[layer],
                       w_exp_gate[layer], w_exp_up[layer], w_exp_down[layer],
                       norm_final, layer == depth - 1)
    return xf.reshape(bsz, s_len, d)
```

```python
import functools
import math

import jax
import jax.numpy as jnp
import numpy as np
from jax import lax
from jax.experimental import pallas as pl
from jax.experimental.pallas import tpu as pltpu

F32 = jnp.float32
BF16 = jnp.bfloat16
I32 = jnp.int32
HIGHEST = lax.Precision.HIGHEST

EPS = 1e-6
NEG = -1e30
A_HEADS = 8
A_QK_DIM = 64
A_V_DIM = 128
A_WIDTH = A_HEADS * A_V_DIM
N_BUCKETS = 32
MAX_EXACT = 16
MAX_DISTANCE = 128
B_HEADS = 4
B_KEY_DIM = 128
B_VAL_DIM = 256
B_WIDTH = B_HEADS * B_VAL_DIM
GATE_RANK = 16
GATE_TEMP = 16.0
GLA_CHUNK = 64
C_GROUPS = 8
C_CHUNK = 128
N_GROUPS = 4
EXPERTS_PER_GROUP = 8
N_EXPERTS = N_GROUPS * EXPERTS_PER_GROUP
TOP_K = 2
LANES = 128
ROUTER_LANE0 = N_GROUPS
MOE_BLOCK = 256
VMEM_LIMIT = 56 * 1024 * 1024

NT_DIMS = (((1,), (1,)), ((), ()))
TN_DIMS = (((0,), (0,)), ((), ()))


def _t5_thresholds():
    n = np.arange(0, 2 * MAX_DISTANCE)
    nf = np.maximum(n, 1).astype(np.float64)
    large = MAX_EXACT + (np.log(nf / MAX_EXACT) / math.log(MAX_DISTANCE / MAX_EXACT)
                         * (N_BUCKETS - MAX_EXACT)).astype(np.int64)
    bucket = np.where(n < MAX_EXACT, n, np.minimum(large, N_BUCKETS - 1))
    return [int(np.min(n[bucket >= b])) for b in range(N_BUCKETS)]


T5_THRESHOLDS = _t5_thresholds()


def _params(*sem):
    return pltpu.CompilerParams(dimension_semantics=sem, vmem_limit_bytes=VMEM_LIMIT)


def _rms(x, g):
    ms = jnp.mean(x * x, axis=-1, keepdims=True)
    return x * lax.rsqrt(ms + EPS) * g


def _gelu_tanh(x):
    c = math.sqrt(2.0 / math.pi)
    return 0.5 * x * (1.0 + jnp.tanh(c * (x + 0.044715 * (x * x * x))))


def _sigmoid(x):
    return 1.0 / (1.0 + jnp.exp(-x))


def _norm_proj_kernel(*refs, act, has_extra):
    if has_extra:
        x_ref, g_ref, w_ref, we_ref, o_ref, oe_ref, h_scr = refs
    else:
        x_ref, g_ref, w_ref, o_ref, h_scr = refs

    @pl.when(pl.program_id(1) == 0)
    def _():
        hb = _rms(x_ref[...], g_ref[...]).astype(BF16)
        h_scr[...] = hb
        if has_extra:
            oe_ref[...] = jnp.dot(hb, we_ref[...].astype(BF16), preferred_element_type=F32)

    y = jnp.dot(h_scr[...], w_ref[...].astype(BF16), preferred_element_type=F32)
    if act == "gelu":
        y = _gelu_tanh(y)
    o_ref[...] = y.astype(o_ref.dtype)


def norm_proj(x, g, w, li, n_cols, *, act=None, w_extra=None, tm=1024, tn=512):
    t, d = x.shape
    tm = min(tm, t)
    grid = (t // tm, n_cols // tn)
    in_specs = [pl.BlockSpec((tm, d), lambda i, j: (i, 0)),
                pl.BlockSpec((1, d), lambda i, j: (0, 0)),
                pl.BlockSpec((None, d, tn), lambda i, j: (li, 0, j))]
    out_specs = [pl.BlockSpec((tm, tn), lambda i, j: (i, j))]
    out_shape = [jax.ShapeDtypeStruct((t, n_cols), BF16)]
    args = [x, g.reshape(1, d), w]
    if w_extra is not None:
        ne = w_extra.shape[1]
        in_specs.append(pl.BlockSpec((d, ne), lambda i, j: (0, 0)))
        out_specs.append(pl.BlockSpec((tm, ne), lambda i, j: (i, 0)))
        out_shape.append(jax.ShapeDtypeStruct((t, ne), F32))
        args.append(w_extra)
    outs = pl.pallas_call(
        functools.partial(_norm_proj_kernel, act=act, has_extra=w_extra is not None),
        grid=grid, in_specs=in_specs, out_specs=out_specs, out_shape=out_shape,
        scratch_shapes=[pltpu.VMEM((tm, d), BF16)],
        compiler_params=_params("parallel", "arbitrary"),
        name="norm_proj_" + (act or "lin"),
    )(*args)
    return outs if w_extra is not None else outs[0]


def _out_proj_kernel(a0_ref, a1_ref, w_ref, r_ref, o_ref):
    kh = a0_ref.shape[1]
    w = w_ref[...].astype(BF16)
    y = jnp.dot(a0_ref[...], w[:kh], preferred_element_type=F32)
    y = y + jnp.dot(a1_ref[...], w[kh:], preferred_element_type=F32)
    o_ref[...] = r_ref[...] + y


def out_proj(a0, a0_blk, a1, a1_blk, w, li, res, *, tm=1024, tn=512):
    t, n = res.shape
    k = w.shape[1]
    kh = k // 2
    tm = min(tm, t)
    return pl.pallas_call(
        _out_proj_kernel,
        grid=(t // tm, n // tn),
        in_specs=[pl.BlockSpec((tm, kh), lambda i, j: (i, a0_blk)),
                  pl.BlockSpec((tm, kh), lambda i, j: (i, a1_blk)),
                  pl.BlockSpec((None, k, tn), lambda i, j: (li, 0, j)),
                  pl.BlockSpec((tm, tn), lambda i, j: (i, j))],
        out_specs=pl.BlockSpec((tm, tn), lambda i, j: (i, j)),
        out_shape=jax.ShapeDtypeStruct((t, n), F32),
        compiler_params=_params("parallel", "arbitrary"),
        name="out_proj",
    )(a0, a1, w, res)


def _diff_attn_kernel(qt_ref, kt_ref, tbl_ref, q_ref, k_ref, v_ref, lam_ref, sg_ref, o_ref,
                      m_scr, acc_scr, bias_scr, q_scr, v_scr, *, t, lam_init):
    h = pl.program_id(1)
    step_id = pl.program_id(2)
    qi = qt_ref[step_id]
    ki = kt_ref[step_id]
    scale = A_QK_DIM ** -0.5

    @pl.when(step_id == 0)
    def _build_bias():
        row = lax.broadcasted_iota(I32, (t, t), 0)
        col = lax.broadcasted_iota(I32, (t, t), 1)
        for d in range(2):
            rel = row - col + d * t
            acc = jnp.full((t, t), tbl_ref[h], F32)
            for b in range(1, N_BUCKETS):
                acc = jnp.where(rel >= T5_THRESHOLDS[b], tbl_ref[b * A_HEADS + h], acc)
            if d == 0:
                acc = jnp.where(rel >= 0, acc, NEG)
            bias_scr[d] = acc

    @pl.when(ki == 0)
    def _init():
        m_scr[...] = jnp.full(m_scr.shape, NEG, F32)
        acc_scr[...] = jnp.zeros(acc_scr.shape, F32)
        q = q_ref[...] * scale
        lane = lax.broadcasted_iota(I32, q.shape, 1)
        q_scr[0] = jnp.where(lane < A_QK_DIM, q, jnp.zeros_like(q))
        q_scr[1] = jnp.where(lane >= A_QK_DIM, q, jnp.zeros_like(q))
        v_scr[:, A_V_DIM:] = jnp.ones((t, A_V_DIM), BF16)

    def step(bias_tile, bias_const):
        k = k_ref[...]
        v_scr[:, :A_V_DIM] = v_ref[...]
        v_aug = v_scr[...]
        for m in range(2):
            s = lax.dot_general(q_scr[m], k, NT_DIMS, preferred_element_type=F32)
            if bias_tile is not None:
                s = s + bias_tile
            m_old = m_scr[m]
            m_new = jnp.maximum(m_old, jnp.max(s, axis=-1, keepdims=True) + bias_const)
            alpha = jnp.exp(m_old - m_new)
            p = jnp.exp((s - (m_new - bias_const)).astype(BF16))
            acc_scr[m] = alpha * acc_scr[m] + jnp.dot(p, v_aug, preferred_element_type=F32)
            m_scr[m] = m_new

    @pl.when(ki < qi - 1)
    def _far():
        step(None, tbl_ref[(N_BUCKETS - 1) * A_HEADS + h])

    @pl.when(ki == qi - 1)
    def _near():
        step(bias_scr[1], 0.0)

    @pl.when(ki == qi)
    def _diag():
        step(bias_scr[0], 0.0)
        lv = lam_ref[...]
        s1 = jnp.sum(lv[0:1] * lv[1:2], axis=-1, keepdims=True)
        s2 = jnp.sum(lv[2:3] * lv[3:4], axis=-1, keepdims=True)
        lam = jnp.exp(s1) - jnp.exp(s2) + lam_init
        a0 = acc_scr[0]
        a1 = acc_scr[1]
        o = (a0[:, :A_V_DIM] / a0[:, A_V_DIM:A_V_DIM + 1]
             - lam * (a1[:, :A_V_DIM] / a1[:, A_V_DIM:A_V_DIM + 1]))
        o = _rms(o, sg_ref[...]) * (1.0 - lam_init)
        o_ref[...] = o.astype(o_ref.dtype)


def diff_attention(proj, rel_bias, lam_vec, subln_g, lam_init, bsz, s_len, *, t=512):
    t = min(t, s_len)
    assert t >= MAX_DISTANCE and s_len % t == 0
    nq = s_len // t
    pairs = [(qi, ki) for qi in range(nq) for ki in range(qi + 1)]
    qt = jnp.asarray([p[0] for p in pairs], I32)
    kt = jnp.asarray([p[1] for p in pairs], I32)
    q_map = lambda b, h, s, qt, kt: (b * nq + qt[s], h)
    kv_map = lambda off: (lambda b, h, s, qt, kt: (b * nq + kt[s], off + h))
    const = lambda b, h, s, qt, kt: (0, 0)
    grid_spec = pltpu.PrefetchScalarGridSpec(
        num_scalar_prefetch=2, grid=(bsz, A_HEADS, len(pairs)),
        in_specs=[pl.BlockSpec(memory_space=pltpu.SMEM),
                  pl.BlockSpec((t, A_V_DIM), q_map),
                  pl.BlockSpec((t, A_V_DIM), kv_map(A_HEADS)),
                  pl.BlockSpec((t, A_V_DIM), kv_map(2 * A_HEADS)),
                  pl.BlockSpec((4, A_QK_DIM), const),
                  pl.BlockSpec((1, A_V_DIM), const)],
        out_specs=pl.BlockSpec((t, A_V_DIM), q_map),
        scratch_shapes=[pltpu.VMEM((2, t, 1), F32),
                        pltpu.VMEM((2, t, 2 * A_V_DIM), F32),
                        pltpu.VMEM((2, t, t), F32),
                        pltpu.VMEM((2, t, A_V_DIM), BF16),
                        pltpu.VMEM((t, 2 * A_V_DIM), BF16)])
    return pl.pallas_call(
        functools.partial(_diff_attn_kernel, t=t, lam_init=lam_init),
        grid_spec=grid_spec,
        out_shape=jax.ShapeDtypeStruct((bsz * s_len, A_WIDTH), BF16),
        compiler_params=_params("parallel", "arbitrary", "arbitrary"),
        name="diff_attention",
    )(qt, kt, rel_bias.reshape(-1), proj, proj, proj, lam_vec, subln_g.reshape(1, A_V_DIM))


def _gla_kernel(q_ref, k_ref, v_ref, r_ref, gb_ref, wgk_ref, bgk_ref, gn_ref, o_ref, st_scr, *, rows):
    c_len = GLA_CHUNK
    scale = B_KEY_DIM ** -0.5

    @pl.when(pl.program_id(1) == 0)
    def _():
        st_scr[...] = jnp.zeros(st_scr.shape, F32)

    z = jnp.dot(gb_ref[...], wgk_ref[...], preferred_element_type=F32, precision=HIGHEST)
    z = z + bgk_ref[...]
    g_log = (jnp.minimum(z, 0.0) - jnp.log(1.0 + jnp.exp(-jnp.abs(z)))) * (1.0 / GATE_TEMP)
    row = lax.broadcasted_iota(I32, (rows, rows), 0)
    col = lax.broadcasted_iota(I32, (rows, rows), 1)
    ltri = ((row >= col) & ((row // c_len) == (col // c_len))).astype(F32)
    bcum = jnp.dot(ltri, g_log, preferred_element_type=F32, precision=HIGHEST)
    causal = (lax.broadcasted_iota(I32, (c_len, c_len), 0)
              >= lax.broadcasted_iota(I32, (c_len, c_len), 1))
    gn = gn_ref[...]

    for c in range(rows // c_len):
        rs = slice(c * c_len, (c + 1) * c_len)
        for hh in range(B_HEADS):
            ks = slice(hh * B_KEY_DIM, (hh + 1) * B_KEY_DIM)
            vs = slice(hh * B_VAL_DIM, (hh + 1) * B_VAL_DIM)
            b = bcum[rs, ks]
            b_last = b[c_len - 1:c_len, :]
            b_mid = b[c_len // 2:c_len // 2 + 1, :]
            q = q_ref[rs, ks].astype(F32) * scale
            k = k_ref[rs, ks].astype(F32)
            v = v_ref[rs, vs]
            st = st_scr[hh]
            q_in = (q * jnp.exp(b)).astype(BF16)
            q_ia = (q * jnp.exp(b - b_mid)).astype(BF16)
            k_ia = (k * jnp.exp(b_mid - b)).astype(BF16)
            att = lax.dot_general(q_ia, k_ia, NT_DIMS, preferred_element_type=F32)
            att = jnp.where(causal, att, 0.0)
            o = jnp.dot(att.astype(BF16), v, preferred_element_type=F32)
            o = o + lax.dot_general(q_in, st.astype(BF16), NT_DIMS, preferred_element_type=F32)
            k_st = (k * jnp.exp(b_last - b)).astype(BF16)
            st_scr[hh] = st * jnp.exp(b_last) + lax.dot_general(
                v, k_st, TN_DIMS, preferred_element_type=F32)
            on = _rms(o, gn)
            r = r_ref[rs, vs].astype(F32)
            o_ref[rs, vs] = (on * (r * _sigmoid(r))).astype(o_ref.dtype)


def gla_mixer(proj, gb, w_gk, b_gk, gnorm_g, bsz, s_len, *, rows=256):
    rows = min(rows, s_len)
    nr = s_len // rows
    kw = B_HEADS * B_KEY_DIM
    rmap = lambda blk: (lambda b, i: (b * nr + i, blk))
    return pl.pallas_call(
        functools.partial(_gla_kernel, rows=rows),
        grid=(bsz, nr),
        in_specs=[pl.BlockSpec((rows, kw), rmap(3 * A_WIDTH // kw)),
                  pl.BlockSpec((rows, kw), rmap(3 * A_WIDTH // kw + 1)),
                  pl.BlockSpec((rows, B_WIDTH), rmap((3 * A_WIDTH + 2 * kw) // B_WIDTH)),
                  pl.BlockSpec((rows, B_WIDTH), rmap((3 * A_WIDTH + 2 * kw) // B_WIDTH + 1)),
                  pl.BlockSpec((rows, GATE_RANK), rmap(0)),
                  pl.BlockSpec((GATE_RANK, kw), lambda b, i: (0, 0)),
                  pl.BlockSpec((1, kw), lambda b, i: (0, 0)),
                  pl.BlockSpec((1, B_VAL_DIM), lambda b, i: (0, 0))],
        out_specs=pl.BlockSpec((rows, B_WIDTH), rmap(0)),
        out_shape=jax.ShapeDtypeStruct((bsz * s_len, B_WIDTH), BF16),
        scratch_shapes=[pltpu.VMEM((B_HEADS, B_VAL_DIM, B_KEY_DIM), F32)],
        compiler_params=_params("parallel", "arbitrary"),
        name="gla",
    )(proj, proj, proj, proj, gb, w_gk, b_gk.reshape(1, kw), gnorm_g.reshape(1, B_VAL_DIM))


def _spatial_kernel(u_ref, v_ref, lg_ref, lb_ref, ws_ref, bs_ref, o_ref, *, rows):
    v = v_ref[...].astype(F32)
    mu = jnp.mean(v, axis=-1, keepdims=True)
    vc = v - mu
    var = jnp.mean(vc * vc, axis=-1, keepdims=True)
    vn = (vc * lax.rsqrt(var + EPS) * lg_ref[...] + lb_ref[...]).astype(BF16)
    gd = v.shape[1] // C_GROUPS
    mask = (lax.broadcasted_iota(I32, (C_CHUNK, C_CHUNK), 0)
            >= lax.broadcasted_iota(I32, (C_CHUNK, C_CHUNK), 1))
    for g in range(C_GROUPS):
        w = jnp.where(mask, ws_ref[g], 0.0).astype(BF16)
        bias = bs_ref[:, g:g + 1]
        cs = slice(g * gd, (g + 1) * gd)
        for c in range(rows // C_CHUNK):
            rs = slice(c * C_CHUNK, (c + 1) * C_CHUNK)
            z = jnp.dot(w, vn[rs, cs], preferred_element_type=F32) + bias
            o_ref[rs, cs] = (u_ref[rs, cs].astype(F32) * z).astype(o_ref.dtype)


def spatial_gate(proj, ln_g, ln_b, w_s, b_s, *, rows=512):
    t = proj.shape[0]
    cw = proj.shape[1] // 2
    rows = min(rows, t)
    return pl.pallas_call(
        functools.partial(_spatial_kernel, rows=rows),
        grid=(t // rows,),
        in_specs=[pl.BlockSpec((rows, cw), lambda i: (i, 0)),
                  pl.BlockSpec((rows, cw), lambda i: (i, 1)),
                  pl.BlockSpec((1, cw), lambda i: (0, 0)),
                  pl.BlockSpec((1, cw), lambda i: (0, 0)),
                  pl.BlockSpec((C_GROUPS, C_CHUNK, C_CHUNK), lambda i: (0, 0, 0)),
                  pl.BlockSpec((C_CHUNK, C_GROUPS), lambda i: (0, 0))],
        out_specs=pl.BlockSpec((rows, cw), lambda i: (i, 0)),
        out_shape=jax.ShapeDtypeStruct((t, cw), BF16),
        compiler_params=_params("parallel"),
        name="spatial_gate",
    )(proj, proj, ln_g.reshape(1, cw), ln_b.reshape(1, cw), w_s, b_s.T)


def _router_kernel(x_ref, g_ref, wr_ref, h_ref, ri_ref, rg_ref, cnt_ref, carry_scr, *, tr):
    @pl.when(pl.program_id(0) == 0)
    def _():
        carry_scr[...] = jnp.zeros(carry_scr.shape, F32)

    h = _rms(x_ref[...], g_ref[...])
    h_ref[...] = h
    logits = jnp.dot(h, wr_ref[...], preferred_element_type=F32, precision=HIGHEST)
    lane = lax.broadcasted_iota(I32, (tr, LANES), 1)
    lanef = lane.astype(F32)
    big = float(LANES)

    is_g = lane < N_GROUPS
    lg = jnp.where(is_g, logits, NEG)
    gmax = jnp.max(lg, axis=-1, keepdims=True)
    p_grp = 1.0 / jnp.sum(jnp.exp(lg - gmax), axis=-1, keepdims=True)
    gidx = jnp.min(jnp.where(is_g & (lg == gmax), lanef, big), axis=-1, keepdims=True)

    lo = ROUTER_LANE0 + EXPERTS_PER_GROUP * gidx
    in_e = (lanef >= lo) & (lanef < lo + EXPERTS_PER_GROUP)
    le = jnp.where(in_e, logits, NEG)
    emax = jnp.max(le, axis=-1, keepdims=True)
    i1 = jnp.min(jnp.where(in_e & (le == emax), lanef, big), axis=-1, keepdims=True)
    oh1 = lanef == i1
    le2 = jnp.where(oh1, NEG, le)
    emax2 = jnp.max(le2, axis=-1, keepdims=True)
    i2 = jnp.min(jnp.where(in_e & (le2 == emax2) & (~oh1), lanef, big), axis=-1, keepdims=True)
    oh2 = lanef == i2
    ratio = jnp.exp(emax2 - emax)
    g1 = p_grp / (1.0 + ratio)
    g2 = p_grp * ratio / (1.0 + ratio)

    onehot = (oh1 | oh2).astype(BF16)
    tri = (lax.broadcasted_iota(I32, (tr, tr), 0) > lax.broadcasted_iota(I32, (tr, tr), 1))
    before = jnp.dot(tri.astype(BF16), onehot, preferred_element_type=F32) + carry_scr[...]
    rank1 = jnp.sum(jnp.where(oh1, before, 0.0), axis=-1, keepdims=True)
    rank2 = jnp.sum(jnp.where(oh2, before, 0.0), axis=-1, keepdims=True)
    carry = carry_scr[...] + jnp.sum(onehot.astype(F32), axis=0, keepdims=True)
    carry_scr[...] = carry

    e1 = i1 - ROUTER_LANE0
    e2 = i2 - ROUTER_LANE0
    ri = jnp.where(lane == 0, e1, jnp.where(lane == 1, e2,
                                             jnp.where(lane == 2, rank1,
                                                       jnp.where(lane == 3, rank2, 0.0))))
    ri_ref[...] = ri.astype(I32)
    rg_ref[...] = jnp.where(lane == 0, g1, jnp.where(lane == 1, g2, 0.0))
    cnt_ref[...] = jnp.broadcast_to(carry, cnt_ref.shape).astype(I32)


def moe_router(x, g, w_rg, w_re, *, tr=512):
    t, d = x.shape
    tr = min(tr, t)
    w_r = jnp.zeros((d, LANES), F32)
    w_r = w_r.at[:, :N_GROUPS].set(w_rg).at[:, ROUTER_LANE0:ROUTER_LANE0 + N_EXPERTS].set(w_re)
    return pl.pallas_call(
        functools.partial(_router_kernel, tr=tr),
        grid=(t // tr,),
        in_specs=[pl.BlockSpec((tr, d), lambda i: (i, 0)),
                  pl.BlockSpec((1, d), lambda i: (0, 0)),
                  pl.BlockSpec((d, LANES), lambda i: (0, 0))],
        out_specs=[pl.BlockSpec((tr, d), lambda i: (i, 0)),
                   pl.BlockSpec((tr, LANES), lambda i: (i, 0)),
                   pl.BlockSpec((tr, LANES), lambda i: (i, 0)),
                   pl.BlockSpec((8, LANES), lambda i: (0, 0))],
        out_shape=[jax.ShapeDtypeStruct((t, d), F32),
                   jax.ShapeDtypeStruct((t, LANES), I32),
                   jax.ShapeDtypeStruct((t, LANES), F32),
                   jax.ShapeDtypeStruct((8, LANES), I32)],
        scratch_shapes=[pltpu.VMEM((1, LANES), F32)],
        compiler_params=_params("arbitrary"),
        name="moe_router",
    )(x, g.reshape(1, d), w_r)


def _row_pos(e_ref, rank_ref, pstart_ref, a):
    return pstart_ref[e_ref[a]] + rank_ref[a]


def _dispatch_kernel(e_ref, rank_ref, pstart_ref, h_ref, xs_in_ref, xs_ref, sem, *, td):
    del xs_in_ref
    base = pl.program_id(0) * td

    def row_copy(r, pos):
        return pltpu.make_async_copy(h_ref.at[pl.ds(r, 1)], xs_ref.at[pl.ds(pos, 1)], sem)

    def issue(r, carry):
        for kk in range(TOP_K):
            row_copy(r, _row_pos(e_ref, rank_ref, pstart_ref, (base + r) * TOP_K + kk)).start()
        return carry

    def drain(r, carry):
        for kk in range(TOP_K):
            row_copy(r, _row_pos(e_ref, rank_ref, pstart_ref, (base + r) * TOP_K + kk)).wait()
        return carry

    lax.fori_loop(0, td, issue, 0)
    lax.fori_loop(0, td, drain, 0)


def moe_dispatch(h, e_flat, rank_flat, pad_start, n_rows, *, td=256):
    t, d = h.shape
    td = min(td, t)
    grid_spec = pltpu.PrefetchScalarGridSpec(
        num_scalar_prefetch=3, grid=(t // td,),
        in_specs=[pl.BlockSpec((td, d), lambda i, *_: (i, 0)),
                  pl.BlockSpec(memory_space=pl.ANY)],
        out_specs=pl.BlockSpec(memory_space=pl.ANY),
        scratch_shapes=[pltpu.SemaphoreType.DMA(())])
    return pl.pallas_call(
        functools.partial(_dispatch_kernel, td=td),
        grid_spec=grid_spec,
        out_shape=jax.ShapeDtypeStruct((n_rows, d), F32),
        input_output_aliases={4: 0},
        compiler_params=_params("arbitrary"),
        name="moe_dispatch",
    )(e_flat, rank_flat, pad_start, h, jnp.zeros((n_rows, d), F32))


def _expert_kernel(blk_e_ref, nused_ref, xs_ref, wg_ref, wu_ref, wd_ref, y_ref, wg_s, wu_s, wd_s):
    j = pl.program_id(0)
    used = j < nused_ref[0]
    new_expert = (j == 0) | (blk_e_ref[j] != blk_e_ref[jnp.maximum(j - 1, 0)])

    @pl.when(used & new_expert)
    def _():
        wg_s[...] = wg_ref[0].astype(BF16)
        wu_s[...] = wu_ref[0].astype(BF16)
        wd_s[...] = wd_ref[0].astype(BF16)

    @pl.when(used)
    def _():
        x = xs_ref[...].astype(BF16)
        gate = jnp.dot(x, wg_s[...], preferred_element_type=F32)
        up = jnp.dot(x, wu_s[...], preferred_element_type=F32)
        act = (gate * _sigmoid(gate) * up).astype(BF16)
        y_ref[...] = jnp.dot(act, wd_s[...], preferred_element_type=F32)

    @pl.when(jnp.logical_not(used))
    def _():
        y_ref[...] = jnp.zeros(y_ref.shape, F32)


def moe_experts(xs, blk_e, n_used, w_gate, w_up, w_down, li):
    n_rows, d = xs.shape
    ff = w_gate.shape[3]
    nb = n_rows // MOE_BLOCK
    row_map = lambda j, be, nu: (jnp.minimum(j, nu[0] - 1), 0)
    grid_spec = pltpu.PrefetchScalarGridSpec(
        num_scalar_prefetch=2, grid=(nb,),
        in_specs=[pl.BlockSpec((MOE_BLOCK, d), row_map),
                  pl.BlockSpec((None, 1, d, ff), lambda j, be, nu: (li, be[j], 0, 0)),
                  pl.BlockSpec((None, 1, d, ff), lambda j, be, nu: (li, be[j], 0, 0)),
                  pl.BlockSpec((None, 1, ff, d), lambda j, be, nu: (li, be[j], 0, 0))],
        out_specs=pl.BlockSpec((MOE_BLOCK, d), lambda j, be, nu: (j, 0)),
        scratch_shapes=[pltpu.VMEM((d, ff), BF16), pltpu.VMEM((d, ff), BF16),
                        pltpu.VMEM((ff, d), BF16)])
    return pl.pallas_call(
        _expert_kernel,
        grid_spec=grid_spec,
        out_shape=jax.ShapeDtypeStruct((n_rows, d), F32),
        compiler_params=_params("arbitrary"),
        name="moe_experts",
    )(blk_e, n_used, xs, w_gate, w_up, w_down)


def _combine_kernel(e_ref, rank_ref, pstart_ref, x_ref, rg_ref, gf_ref, y_ref, o_ref, ybuf, sem,
                    *, tc, final_norm):
    base = pl.program_id(0) * tc

    def row_copy(r, kk, pos):
        return pltpu.make_async_copy(y_ref.at[pl.ds(pos, 1)], ybuf.at[kk, pl.ds(r, 1)], sem)

    def issue(r, carry):
        for kk in range(TOP_K):
            row_copy(r, kk, _row_pos(e_ref, rank_ref, pstart_ref, (base + r) * TOP_K + kk)).start()
        return carry

    def drain(r, carry):
        for kk in range(TOP_K):
            row_copy(r, kk, _row_pos(e_ref, rank_ref, pstart_ref, (base + r) * TOP_K + kk)).wait()
        return carry

    lax.fori_loop(0, tc, issue, 0)
    lax.fori_loop(0, tc, drain, 0)
    gates = rg_ref[...]
    out = x_ref[...] + (gates[:, 0:1] * ybuf[0] + gates[:, 1:2] * ybuf[1])
    if final_norm:
        out = _rms(out, gf_ref[...])
    o_ref[...] = out


def moe_combine(x, y, gates, e_flat, rank_flat, pad_start, g_final, final_norm, *, tc=256):
    t, d = x.shape
    tc = min(tc, t)
    grid_spec = pltpu.PrefetchScalarGridSpec(
        num_scalar_prefetch=3, grid=(t // tc,),
        in_specs=[pl.BlockSpec((tc, d), lambda i, *_: (i, 0)),
                  pl.BlockSpec((tc, LANES), lambda i, *_: (i, 0)),
                  pl.BlockSpec((1, d), lambda i, *_: (0, 0)),
                  pl.BlockSpec(memory_space=pl.ANY)],
        out_specs=pl.BlockSpec((tc, d), lambda i, *_: (i, 0)),
        scratch_shapes=[pltpu.VMEM((TOP_K, tc, d), F32), pltpu.SemaphoreType.DMA(())])
    return pl.pallas_call(
        functools.partial(_combine_kernel, tc=tc, final_norm=final_norm),
        grid_spec=grid_spec,
        out_shape=jax.ShapeDtypeStruct((t, d), F32),
        compiler_params=_params("arbitrary"),
        name="moe_combine",
    )(e_flat, rank_flat, pad_start, x, gates, g_final.reshape(1, d), y)


def moe_layer(x, g_norm, w_rg, w_re, w_gate, w_up, w_down, li, g_final, final_norm):
    t, d = x.shape
    h, ri, gates, cnt = moe_router(x, g_norm, w_rg, w_re)
    e_flat = ri[:, 0:TOP_K].reshape(-1)
    rank_flat = ri[:, TOP_K:2 * TOP_K].reshape(-1)
    counts = cnt[0, ROUTER_LANE0:ROUTER_LANE0 + N_EXPERTS]
    nblk = (counts + MOE_BLOCK - 1) // MOE_BLOCK
    blk_end = jnp.cumsum(nblk)
    pad_start = ((blk_end - nblk) * MOE_BLOCK).astype(I32)
    n_used = blk_end[-1:].astype(I32)
    nb = (t * TOP_K) // MOE_BLOCK + N_EXPERTS
    blk = jnp.minimum(jnp.arange(nb, dtype=I32), n_used[0] - 1)
    blk_e = jnp.searchsorted(blk_end, blk, side="right").astype(I32)
    xs = moe_dispatch(h, e_flat, rank_flat, pad_start, nb * MOE_BLOCK)
    y = moe_experts(xs, blk_e, n_used, w_gate, w_up, w_down, li)
    return moe_combine(x, y, gates, e_flat, rank_flat, pad_start, g_final, final_norm)


def kernel(x, rel_bias, norm_mix, norm_ffn, norm_final, w_in_ab, lam_ab, subln_ab, w_gk_ab, b_gk_ab,
           gnorm_ab, w_out_ab, w_in_c, ln_v_g, ln_v_b, w_spatial, b_spatial, w_out_c,
           w_router_group, w_router_expert, w_exp_gate, w_exp_up, w_exp_down):
    bsz, s_len, d = x.shape
    depth = norm_mix.shape[0]
    xf = x.reshape(bsz * s_len, d)
    for layer in range(depth):
        i = layer // 2
        if layer % 2 == 0:
            lam_init = 0.8 - 0.6 * math.exp(-0.3 * layer)
            n_main = 3 * A_WIDTH + 2 * B_HEADS * B_KEY_DIM + 2 * B_WIDTH
            proj, gb = norm_proj(xf, norm_mix[layer], w_in_ab, i, n_main,
                                 w_extra=w_in_ab[i, :, n_main:])
            o_a = diff_attention(proj, rel_bias, lam_ab[i], subln_ab[i], lam_init, bsz, s_len)
            o_b = gla_mixer(proj, gb, w_gk_ab[i], b_gk_ab[i], gnorm_ab[i], bsz, s_len)
            xf = out_proj(o_a, 0, o_b, 0, w_out_ab, i, xf)
        else:
            proj = norm_proj(xf, norm_mix[layer], w_in_c, i, w_in_c.shape[2], act="gelu")
            uz = spatial_gate(proj, ln_v_g[i], ln_v_b[i], w_spatial[i], b_spatial[i])
            xf = out_proj(uz, 0, uz, 1, w_out_c, i, xf)
        xf = moe_layer(xf, norm_ffn[layer], w_router_group[layer], w_router_expert[layer],
                       w_exp_gate, w_exp_up, w_exp_down, layer,
                       norm_final, layer == depth - 1)
    return xf.reshape(bsz, s_len, d)
```

```python
import functools
import math

import jax
import jax.numpy as jnp
import numpy as np
from jax import lax
from jax.experimental import pallas as pl
from jax.experimental.pallas import tpu as pltpu

F32 = jnp.float32
BF16 = jnp.bfloat16
I32 = jnp.int32
HIGHEST = lax.Precision.HIGHEST

EPS = 1e-6
NEG = -1e30
A_HEADS = 8
A_QK_DIM = 64
A_V_DIM = 128
A_WIDTH = A_HEADS * A_V_DIM
N_BUCKETS = 32
MAX_EXACT = 16
MAX_DISTANCE = 128
B_HEADS = 4
B_KEY_DIM = 128
B_VAL_DIM = 256
B_WIDTH = B_HEADS * B_VAL_DIM
GATE_RANK = 16
GATE_TEMP = 16.0
GLA_CHUNK = 64
C_GROUPS = 8
C_CHUNK = 128
N_GROUPS = 4
EXPERTS_PER_GROUP = 8
N_EXPERTS = N_GROUPS * EXPERTS_PER_GROUP
TOP_K = 2
LANES = 128
ROUTER_LANE0 = N_GROUPS
MOE_BLOCK = 256
VMEM_LIMIT = 56 * 1024 * 1024

NT_DIMS = (((1,), (1,)), ((), ()))
TN_DIMS = (((0,), (0,)), ((), ()))


def _t5_thresholds():
    n = np.arange(0, 2 * MAX_DISTANCE)
    nf = np.maximum(n, 1).astype(np.float64)
    large = MAX_EXACT + (np.log(nf / MAX_EXACT) / math.log(MAX_DISTANCE / MAX_EXACT)
                         * (N_BUCKETS - MAX_EXACT)).astype(np.int64)
    bucket = np.where(n < MAX_EXACT, n, np.minimum(large, N_BUCKETS - 1))
    return [int(np.min(n[bucket >= b])) for b in range(N_BUCKETS)]


T5_THRESHOLDS = _t5_thresholds()


def _params(*sem):
    return pltpu.CompilerParams(dimension_semantics=sem, vmem_limit_bytes=VMEM_LIMIT)


def _rms(x, g):
    ms = jnp.mean(x * x, axis=-1, keepdims=True)
    return x * lax.rsqrt(ms + EPS) * g


def _gelu_tanh(x):
    c = math.sqrt(2.0 / math.pi)
    return 0.5 * x * (1.0 + jnp.tanh(c * (x + 0.044715 * (x * x * x))))


def _sigmoid(x):
    return 1.0 / (1.0 + jnp.exp(-x))


def _norm_proj_kernel(*refs, act, has_extra):
    if has_extra:
        x_ref, g_ref, w_ref, we_ref, o_ref, oe_ref, h_scr = refs
    else:
        x_ref, g_ref, w_ref, o_ref, h_scr = refs

    @pl.when(pl.program_id(1) == 0)
    def _():
        hb = _rms(x_ref[...], g_ref[...]).astype(BF16)
        h_scr[...] = hb
        if has_extra:
            oe_ref[...] = jnp.dot(hb, we_ref[...].astype(BF16), preferred_element_type=F32)

    y = jnp.dot(h_scr[...], w_ref[...].astype(BF16), preferred_element_type=F32)
    if act == "gelu":
        y = _gelu_tanh(y)
    o_ref[...] = y.astype(o_ref.dtype)


def norm_proj(x, g, w, li, n_cols, *, act=None, w_extra=None, tm=1024, tn=512):
    t, d = x.shape
    tm = min(tm, t)
    grid = (t // tm, n_cols // tn)
    in_specs = [pl.BlockSpec((tm, d), lambda i, j: (i, 0)),
                pl.BlockSpec((1, d), lambda i, j: (0, 0)),
                pl.BlockSpec((None, d, tn), lambda i, j: (li, 0, j))]
    out_specs = [pl.BlockSpec((tm, tn), lambda i, j: (i, j))]
    out_shape = [jax.ShapeDtypeStruct((t, n_cols), BF16)]
    args = [x, g.reshape(1, d), w]
    if w_extra is not None:
        ne = w_extra.shape[1]
        in_specs.append(pl.BlockSpec((d, ne), lambda i, j: (0, 0)))
        out_specs.append(pl.BlockSpec((tm, ne), lambda i, j: (i, 0)))
        out_shape.append(jax.ShapeDtypeStruct((t, ne), F32))
        args.append(w_extra)
    outs = pl.pallas_call(
        functools.partial(_norm_proj_kernel, act=act, has_extra=w_extra is not None),
        grid=grid, in_specs=in_specs, out_specs=out_specs, out_shape=out_shape,
        scratch_shapes=[pltpu.VMEM((tm, d), BF16)],
        compiler_params=_params("parallel", "arbitrary"),
        name="norm_proj_" + (act or "lin"),
    )(*args)
    return outs if w_extra is not None else outs[0]


def _out_proj_kernel(a0_ref, a1_ref, w_ref, r_ref, o_ref):
    kh = a0_ref.shape[1]
    w = w_ref[...].astype(BF16)
    y = jnp.dot(a0_ref[...], w[:kh], preferred_element_type=F32)
    y = y + jnp.dot(a1_ref[...], w[kh:], preferred_element_type=F32)
    o_ref[...] = r_ref[...] + y


def out_proj(a0, a0_blk, a1, a1_blk, w, li, res, *, tm=1024, tn=512):
    t, n = res.shape
    k = w.shape[1]
    kh = k // 2
    tm = min(tm, t)
    return pl.pallas_call(
        _out_proj_kernel,
        grid=(t // tm, n // tn),
        in_specs=[pl.BlockSpec((tm, kh), lambda i, j: (i, a0_blk)),
                  pl.BlockSpec((tm, kh), lambda i, j: (i, a1_blk)),
                  pl.BlockSpec((None, k, tn), lambda i, j: (li, 0, j)),
                  pl.BlockSpec((tm, tn), lambda i, j: (i, j))],
        out_specs=pl.BlockSpec((tm, tn), lambda i, j: (i, j)),
        out_shape=jax.ShapeDtypeStruct((t, n), F32),
        compiler_params=_params("parallel", "arbitrary"),
        name="out_proj",
    )(a0, a1, w, res)


def _diff_attn_kernel(qt_ref, kt_ref, tbl_ref, q_ref, k_ref, v_ref, lam_ref, sg_ref, o_ref,
                      m_scr, acc_scr, bias_scr, q_scr, v_scr, *, t, lam_init):
    h = pl.program_id(1)
    step_id = pl.program_id(2)
    qi = qt_ref[step_id]
    ki = kt_ref[step_id]
    scale = A_QK_DIM ** -0.5

    @pl.when(step_id == 0)
    def _build_bias():
        row = lax.broadcasted_iota(I32, (t, t), 0)
        col = lax.broadcasted_iota(I32, (t, t), 1)
        for d in range(2):
            rel = row - col + d * t
            acc = jnp.full((t, t), tbl_ref[h], F32)
            for b in range(1, N_BUCKETS):
                acc = jnp.where(rel >= T5_THRESHOLDS[b], tbl_ref[b * A_HEADS + h], acc)
            if d == 0:
                acc = jnp.where(rel >= 0, acc, NEG)
            bias_scr[d] = acc

    @pl.when(ki == 0)
    def _init():
        m_scr[...] = jnp.full(m_scr.shape, NEG, F32)
        acc_scr[...] = jnp.zeros(acc_scr.shape, F32)
        q = q_ref[...] * scale
        lane = lax.broadcasted_iota(I32, q.shape, 1)
        q_scr[0] = jnp.where(lane < A_QK_DIM, q, jnp.zeros_like(q))
        q_scr[1] = jnp.where(lane >= A_QK_DIM, q, jnp.zeros_like(q))
        v_scr[:, A_V_DIM:] = jnp.ones((t, A_V_DIM), BF16)

    def step(bias_tile, bias_const):
        k = k_ref[...]
        v_scr[:, :A_V_DIM] = v_ref[...]
        v_aug = v_scr[...]
        for m in range(2):
            s = lax.dot_general(q_scr[m], k, NT_DIMS, preferred_element_type=F32)
            if bias_tile is not None:
                s = s + bias_tile
            m_old = m_scr[m]
            m_new = jnp.maximum(m_old, jnp.max(s, axis=-1, keepdims=True) + bias_const)
            alpha = jnp.exp(m_old - m_new)
            p = jnp.exp((s - (m_new - bias_const)).astype(BF16))
            acc_scr[m] = alpha * acc_scr[m] + jnp.dot(p, v_aug, preferred_element_type=F32)
            m_scr[m] = m_new

    @pl.when(ki < qi - 1)
    def _far():
        step(None, tbl_ref[(N_BUCKETS - 1) * A_HEADS + h])

    @pl.when(ki == qi - 1)
    def _near():
        step(bias_scr[1], 0.0)

    @pl.when(ki == qi)
    def _diag():
        step(bias_scr[0], 0.0)
        lv = lam_ref[...]
        s1 = jnp.sum(lv[0:1] * lv[1:2], axis=-1, keepdims=True)
        s2 = jnp.sum(lv[2:3] * lv[3:4], axis=-1, keepdims=True)
        lam = jnp.exp(s1) - jnp.exp(s2) + lam_init
        a0 = acc_scr[0]
        a1 = acc_scr[1]
        o = (a0[:, :A_V_DIM] / a0[:, A_V_DIM:A_V_DIM + 1]
             - lam * (a1[:, :A_V_DIM] / a1[:, A_V_DIM:A_V_DIM + 1]))
        o = _rms(o, sg_ref[...]) * (1.0 - lam_init)
        o_ref[...] = o.astype(o_ref.dtype)


def diff_attention(proj, rel_bias, lam_vec, subln_g, lam_init, bsz, s_len, *, t=512):
    t = min(t, s_len)
    assert t >= MAX_DISTANCE and s_len % t == 0
    nq = s_len // t
    pairs = [(qi, ki) for qi in range(nq) for ki in range(qi + 1)]
    qt = jnp.asarray([p[0] for p in pairs], I32)
    kt = jnp.asarray([p[1] for p in pairs], I32)
    q_map = lambda b, h, s, qt, kt: (b * nq + qt[s], h)
    kv_map = lambda off: (lambda b, h, s, qt, kt: (b * nq + kt[s], off + h))
    const = lambda b, h, s, qt, kt: (0, 0)
    grid_spec = pltpu.PrefetchScalarGridSpec(
        num_scalar_prefetch=2, grid=(bsz, A_HEADS, len(pairs)),
        in_specs=[pl.BlockSpec(memory_space=pltpu.SMEM),
                  pl.BlockSpec((t, A_V_DIM), q_map),
                  pl.BlockSpec((t, A_V_DIM), kv_map(A_HEADS)),
                  pl.BlockSpec((t, A_V_DIM), kv_map(2 * A_HEADS)),
                  pl.BlockSpec((4, A_QK_DIM), const),
                  pl.BlockSpec((1, A_V_DIM), const)],
        out_specs=pl.BlockSpec((t, A_V_DIM), q_map),
        scratch_shapes=[pltpu.VMEM((2, t, 1), F32),
                        pltpu.VMEM((2, t, 2 * A_V_DIM), F32),
                        pltpu.VMEM((2, t, t), F32),
                        pltpu.VMEM((2, t, A_V_DIM), BF16),
                        pltpu.VMEM((t, 2 * A_V_DIM), BF16)])
    return pl.pallas_call(
        functools.partial(_diff_attn_kernel, t=t, lam_init=lam_init),
        grid_spec=grid_spec,
        out_shape=jax.ShapeDtypeStruct((bsz * s_len, A_WIDTH), BF16),
        compiler_params=_params("parallel", "arbitrary", "arbitrary"),
        name="diff_attention",
    )(qt, kt, rel_bias.reshape(-1), proj, proj, proj, lam_vec, subln_g.reshape(1, A_V_DIM))


def _gla_kernel(q_ref, k_ref, v_ref, r_ref, gb_ref, wgk_ref, bgk_ref, gn_ref, o_ref, st_scr, *, rows):
    c_len = GLA_CHUNK
    scale = B_KEY_DIM ** -0.5

    @pl.when(pl.program_id(1) == 0)
    def _():
        st_scr[...] = jnp.zeros(st_scr.shape, F32)

    z = jnp.dot(gb_ref[...], wgk_ref[...], preferred_element_type=F32, precision=HIGHEST)
    z = z + bgk_ref[...]
    g_log = (jnp.minimum(z, 0.0) - jnp.log(1.0 + jnp.exp(-jnp.abs(z)))) * (1.0 / GATE_TEMP)
    row = lax.broadcasted_iota(I32, (rows, rows), 0)
    col = lax.broadcasted_iota(I32, (rows, rows), 1)
    ltri = ((row >= col) & ((row // c_len) == (col // c_len))).astype(F32)
    bcum = jnp.dot(ltri, g_log, preferred_element_type=F32, precision=HIGHEST)
    causal = (lax.broadcasted_iota(I32, (c_len, c_len), 0)
              >= lax.broadcasted_iota(I32, (c_len, c_len), 1))
    gn = gn_ref[...]

    for c in range(rows // c_len):
        rs = slice(c * c_len, (c + 1) * c_len)
        for hh in range(B_HEADS):
            ks = slice(hh * B_KEY_DIM, (hh + 1) * B_KEY_DIM)
            vs = slice(hh * B_VAL_DIM, (hh + 1) * B_VAL_DIM)
            b = bcum[rs, ks]
            b_last = b[c_len - 1:c_len, :]
            b_mid = b[c_len // 2:c_len // 2 + 1, :]
            q = q_ref[rs, ks].astype(F32) * scale
            k = k_ref[rs, ks].astype(F32)
            v = v_ref[rs, vs]
            st = st_scr[hh]
            q_in = (q * jnp.exp(b)).astype(BF16)
            q_ia = (q * jnp.exp(b - b_mid)).astype(BF16)
            k_ia = (k * jnp.exp(b_mid - b)).astype(BF16)
            att = lax.dot_general(q_ia, k_ia, NT_DIMS, preferred_element_type=F32)
            att = jnp.where(causal, att, 0.0)
            o = jnp.dot(att.astype(BF16), v, preferred_element_type=F32)
            o = o + lax.dot_general(q_in, st.astype(BF16), NT_DIMS, preferred_element_type=F32)
            k_st = (k * jnp.exp(b_last - b)).astype(BF16)
            st_scr[hh] = st * jnp.exp(b_last) + lax.dot_general(
                v, k_st, TN_DIMS, preferred_element_type=F32)
            on = _rms(o, gn)
            r = r_ref[rs, vs].astype(F32)
            o_ref[rs, vs] = (on * (r * _sigmoid(r))).astype(o_ref.dtype)


def gla_mixer(proj, gb, w_gk, b_gk, gnorm_g, bsz, s_len, *, rows=256):
    rows = min(rows, s_len)
    nr = s_len // rows
    kw = B_HEADS * B_KEY_DIM
    rmap = lambda blk: (lambda b, i: (b * nr + i, blk))
    return pl.pallas_call(
        functools.partial(_gla_kernel, rows=rows),
        grid=(bsz, nr),
        in_specs=[pl.BlockSpec((rows, kw), rmap(3 * A_WIDTH // kw)),
                  pl.BlockSpec((rows, kw), rmap(3 * A_WIDTH // kw + 1)),
                  pl.BlockSpec((rows, B_WIDTH), rmap((3 * A_WIDTH + 2 * kw) // B_WIDTH)),
                  pl.BlockSpec((rows, B_WIDTH), rmap((3 * A_WIDTH + 2 * kw) // B_WIDTH + 1)),
                  pl.BlockSpec((rows, GATE_RANK), rmap(0)),
                  pl.BlockSpec((GATE_RANK, kw), lambda b, i: (0, 0)),
                  pl.BlockSpec((1, kw), lambda b, i: (0, 0)),
                  pl.BlockSpec((1, B_VAL_DIM), lambda b, i: (0, 0))],
        out_specs=pl.BlockSpec((rows, B_WIDTH), rmap(0)),
        out_shape=jax.ShapeDtypeStruct((bsz * s_len, B_WIDTH), BF16),
        scratch_shapes=[pltpu.VMEM((B_HEADS, B_VAL_DIM, B_KEY_DIM), F32)],
        compiler_params=_params("parallel", "arbitrary"),
        name="gla",
    )(proj, proj, proj, proj, gb, w_gk, b_gk.reshape(1, kw), gnorm_g.reshape(1, B_VAL_DIM))


def _spatial_kernel(u_ref, v_ref, lg_ref, lb_ref, ws_ref, bs_ref, o_ref, *, rows):
    v = v_ref[...].astype(F32)
    mu = jnp.mean(v, axis=-1, keepdims=True)
    vc = v - mu
    var = jnp.mean(vc * vc, axis=-1, keepdims=True)
    vn = (vc * lax.rsqrt(var + EPS) * lg_ref[...] + lb_ref[...]).astype(BF16)
    gd = v.shape[1] // C_GROUPS
    mask = (lax.broadcasted_iota(I32, (C_CHUNK, C_CHUNK), 0)
            >= lax.broadcasted_iota(I32, (C_CHUNK, C_CHUNK), 1))
    for g in range(C_GROUPS):
        w = jnp.where(mask, ws_ref[g], 0.0).astype(BF16)
        bias = bs_ref[:, g:g + 1]
        cs = slice(g * gd, (g + 1) * gd)
        for c in range(rows // C_CHUNK):
            rs = slice(c * C_CHUNK, (c + 1) * C_CHUNK)
            z = jnp.dot(w, vn[rs, cs], preferred_element_type=F32) + bias
            o_ref[rs, cs] = (u_ref[rs, cs].astype(F32) * z).astype(o_ref.dtype)


def spatial_gate(proj, ln_g, ln_b, w_s, b_s, *, rows=512):
    t = proj.shape[0]
    cw = proj.shape[1] // 2
    rows = min(rows, t)
    return pl.pallas_call(
        functools.partial(_spatial_kernel, rows=rows),
        grid=(t // rows,),
        in_specs=[pl.BlockSpec((rows, cw), lambda i: (i, 0)),
                  pl.BlockSpec((rows, cw), lambda i: (i, 1)),
                  pl.BlockSpec((1, cw), lambda i: (0, 0)),
                  pl.BlockSpec((1, cw), lambda i: (0, 0)),
                  pl.BlockSpec((C_GROUPS, C_CHUNK, C_CHUNK), lambda i: (0, 0, 0)),
                  pl.BlockSpec((C_CHUNK, C_GROUPS), lambda i: (0, 0))],
        out_specs=pl.BlockSpec((rows, cw), lambda i: (i, 0)),
        out_shape=jax.ShapeDtypeStruct((t, cw), BF16),
        compiler_params=_params("parallel"),
        name="spatial_gate",
    )(proj, proj, ln_g.reshape(1, cw), ln_b.reshape(1, cw), w_s, b_s.T)


def _router_kernel(x_ref, g_ref, wr_ref, h_ref, ri_ref, rg_ref, cnt_ref, carry_scr, *, tr):
    @pl.when(pl.program_id(0) == 0)
    def _():
        carry_scr[...] = jnp.zeros(carry_scr.shape, F32)

    h = _rms(x_ref[...], g_ref[...])
    h_ref[...] = h
    logits = jnp.dot(h, wr_ref[...], preferred_element_type=F32, precision=HIGHEST)
    lane = lax.broadcasted_iota(I32, (tr, LANES), 1)
    lanef = lane.astype(F32)
    big = float(LANES)

    is_g = lane < N_GROUPS
    lg = jnp.where(is_g, logits, NEG)
    gmax = jnp.max(lg, axis=-1, keepdims=True)
    p_grp = 1.0 / jnp.sum(jnp.exp(lg - gmax), axis=-1, keepdims=True)
    gidx = jnp.min(jnp.where(is_g & (lg == gmax), lanef, big), axis=-1, keepdims=True)

    lo = ROUTER_LANE0 + EXPERTS_PER_GROUP * gidx
    in_e = (lanef >= lo) & (lanef < lo + EXPERTS_PER_GROUP)
    le = jnp.where(in_e, logits, NEG)
    emax = jnp.max(le, axis=-1, keepdims=True)
    i1 = jnp.min(jnp.where(in_e & (le == emax), lanef, big), axis=-1, keepdims=True)
    oh1 = lanef == i1
    le2 = jnp.where(oh1, NEG, le)
    emax2 = jnp.max(le2, axis=-1, keepdims=True)
    i2 = jnp.min(jnp.where(in_e & (le2 == emax2) & (~oh1), lanef, big), axis=-1, keepdims=True)
    oh2 = lanef == i2
    ratio = jnp.exp(emax2 - emax)
    g1 = p_grp / (1.0 + ratio)
    g2 = p_grp * ratio / (1.0 + ratio)

    onehot = (oh1 | oh2).astype(BF16)
    tri = (lax.broadcasted_iota(I32, (tr, tr), 0) > lax.broadcasted_iota(I32, (tr, tr), 1))
    before = jnp.dot(tri.astype(BF16), onehot, preferred_element_type=F32) + carry_scr[...]
    rank1 = jnp.sum(jnp.where(oh1, before, 0.0), axis=-1, keepdims=True)
    rank2 = jnp.sum(jnp.where(oh2, before, 0.0), axis=-1, keepdims=True)
    carry = carry_scr[...] + jnp.sum(onehot.astype(F32), axis=0, keepdims=True)
    carry_scr[...] = carry

    e1 = i1 - ROUTER_LANE0
    e2 = i2 - ROUTER_LANE0
    ri = jnp.where(lane == 0, e1, jnp.where(lane == 1, e2,
                                             jnp.where(lane == 2, rank1,
                                                       jnp.where(lane == 3, rank2, 0.0))))
    ri_ref[...] = ri.astype(I32)
    rg_ref[...] = jnp.where(lane == 0, g1, jnp.where(lane == 1, g2, 0.0))
    cnt_ref[...] = jnp.broadcast_to(carry, cnt_ref.shape).astype(I32)


def moe_router(x, g, w_rg, w_re, *, tr=512):
    t, d = x.shape
    tr = min(tr, t)
    w_r = jnp.zeros((d, LANES), F32)
    w_r = w_r.at[:, :N_GROUPS].set(w_rg).at[:, ROUTER_LANE0:ROUTER_LANE0 + N_EXPERTS].set(w_re)
    return pl.pallas_call(
        functools.partial(_router_kernel, tr=tr),
        grid=(t // tr,),
        in_specs=[pl.BlockSpec((tr, d), lambda i: (i, 0)),
                  pl.BlockSpec((1, d), lambda i: (0, 0)),
                  pl.BlockSpec((d, LANES), lambda i: (0, 0))],
        out_specs=[pl.BlockSpec((tr, d), lambda i: (i, 0)),
                   pl.BlockSpec((tr, LANES), lambda i: (i, 0)),
                   pl.BlockSpec((tr, LANES), lambda i: (i, 0)),
                   pl.BlockSpec((8, LANES), lambda i: (0, 0))],
        out_shape=[jax.ShapeDtypeStruct((t, d), F32),
                   jax.ShapeDtypeStruct((t, LANES), I32),
                   jax.ShapeDtypeStruct((t, LANES), F32),
                   jax.ShapeDtypeStruct((8, LANES), I32)],
        scratch_shapes=[pltpu.VMEM((1, LANES), F32)],
        compiler_params=_params("arbitrary"),
        name="moe_router",
    )(x, g.reshape(1, d), w_r)


DMA_UNROLL = 8


def _dispatch_kernel(pos_ref, zf_ref, h_ref, xs_ref, zero_scr, sem, zsem, *, td, nb):
    i = pl.program_id(0)
    base = i * td

    @pl.when(i == 0)
    def _():
        zero_scr[...] = jnp.zeros(zero_scr.shape, zero_scr.dtype)

        def zero_copy(j):
            return pltpu.make_async_copy(zero_scr, xs_ref.at[pl.ds(j * MOE_BLOCK, MOE_BLOCK)], zsem)

        def zstart(j, carry):
            @pl.when(zf_ref[j] == 1)
            def _():
                zero_copy(j).start()
            return carry

        def zwait(j, carry):
            @pl.when(zf_ref[j] == 1)
            def _():
                zero_copy(j).wait()
            return carry

        lax.fori_loop(0, nb, zstart, 0)
        lax.fori_loop(0, nb, zwait, 0)

    def issue(r, carry):
        for kk in range(TOP_K):
            pos = pos_ref[(base + r) * TOP_K + kk]
            pltpu.make_async_copy(h_ref.at[pl.ds(r, 1)], xs_ref.at[pl.ds(pos, 1)], sem).start()
        return carry

    lax.fori_loop(0, td, issue, 0, unroll=DMA_UNROLL)
    for kk in range(TOP_K):
        pltpu.make_async_copy(h_ref, xs_ref.at[pl.ds(0, td)], sem).wait()


def moe_dispatch(h, pos, zero_flag, n_rows, *, td=256):
    t, d = h.shape
    td = min(td, t)
    nb = n_rows // MOE_BLOCK
    grid_spec = pltpu.PrefetchScalarGridSpec(
        num_scalar_prefetch=2, grid=(t // td,),
        in_specs=[pl.BlockSpec((td, d), lambda i, *_: (i, 0))],
        out_specs=pl.BlockSpec(memory_space=pl.ANY),
        scratch_shapes=[pltpu.VMEM((MOE_BLOCK, d), F32),
                        pltpu.SemaphoreType.DMA(()), pltpu.SemaphoreType.DMA(())])
    return pl.pallas_call(
        functools.partial(_dispatch_kernel, td=td, nb=nb),
        grid_spec=grid_spec,
        out_shape=jax.ShapeDtypeStruct((n_rows, d), F32),
        compiler_params=_params("arbitrary"),
        name="moe_dispatch",
    )(pos, zero_flag, h)


def _expert_kernel(blk_e_ref, nused_ref, xs_ref, wg_ref, wu_ref, wd_ref, y_ref, wg_s, wu_s, wd_s):
    j = pl.program_id(0)
    used = j < nused_ref[0]
    new_expert = (j == 0) | (blk_e_ref[j] != blk_e_ref[jnp.maximum(j - 1, 0)])

    @pl.when(used & new_expert)
    def _():
        wg_s[...] = wg_ref[0].astype(BF16)
        wu_s[...] = wu_ref[0].astype(BF16)
        wd_s[...] = wd_ref[0].astype(BF16)

    @pl.when(used)
    def _():
        x = xs_ref[...].astype(BF16)
        gate = jnp.dot(x, wg_s[...], preferred_element_type=F32)
        up = jnp.dot(x, wu_s[...], preferred_element_type=F32)
        act = (gate * _sigmoid(gate) * up).astype(BF16)
        y_ref[...] = jnp.dot(act, wd_s[...], preferred_element_type=F32)

    @pl.when(jnp.logical_not(used))
    def _():
        y_ref[...] = jnp.zeros(y_ref.shape, F32)


def moe_experts(xs, blk_e, n_used, w_gate, w_up, w_down, li):
    n_rows, d = xs.shape
    ff = w_gate.shape[3]
    nb = n_rows // MOE_BLOCK
    row_map = lambda j, be, nu: (jnp.minimum(j, nu[0] - 1), 0)
    grid_spec = pltpu.PrefetchScalarGridSpec(
        num_scalar_prefetch=2, grid=(nb,),
        in_specs=[pl.BlockSpec((MOE_BLOCK, d), row_map),
                  pl.BlockSpec((None, 1, d, ff), lambda j, be, nu: (li, be[j], 0, 0)),
                  pl.BlockSpec((None, 1, d, ff), lambda j, be, nu: (li, be[j], 0, 0)),
                  pl.BlockSpec((None, 1, ff, d), lambda j, be, nu: (li, be[j], 0, 0))],
        out_specs=pl.BlockSpec((MOE_BLOCK, d), lambda j, be, nu: (j, 0)),
        scratch_shapes=[pltpu.VMEM((d, ff), BF16), pltpu.VMEM((d, ff), BF16),
                        pltpu.VMEM((ff, d), BF16)])
    return pl.pallas_call(
        _expert_kernel,
        grid_spec=grid_spec,
        out_shape=jax.ShapeDtypeStruct((n_rows, d), F32),
        compiler_params=_params("arbitrary"),
        name="moe_experts",
    )(blk_e, n_used, xs, w_gate, w_up, w_down)


def _combine_kernel(pos_ref, x_ref, rg_ref, gf_ref, y_ref, o_ref, ybuf, sems, *, tc, final_norm):
    i = pl.program_id(0)
    slot = i % 2

    def issue_tile(tile, dst_slot):
        base = tile * tc

        def issue(r, carry):
            for kk in range(TOP_K):
                pos = pos_ref[(base + r) * TOP_K + kk]
                pltpu.make_async_copy(y_ref.at[pl.ds(pos, 1)], ybuf.at[dst_slot, kk, pl.ds(r, 1)],
                                      sems.at[dst_slot]).start()
            return carry

        lax.fori_loop(0, tc, issue, 0, unroll=DMA_UNROLL)

    @pl.when(i == 0)
    def _():
        issue_tile(0, 0)

    @pl.when(i + 1 < pl.num_programs(0))
    def _():
        issue_tile(i + 1, 1 - slot)

    for kk in range(TOP_K):
        pltpu.make_async_copy(y_ref.at[pl.ds(0, tc)], ybuf.at[slot, kk], sems.at[slot]).wait()
    gates = rg_ref[...]
    out = x_ref[...] + (gates[:, 0:1] * ybuf[slot, 0] + gates[:, 1:2] * ybuf[slot, 1])
    if final_norm:
        out = _rms(out, gf_ref[...])
    o_ref[...] = out


def moe_combine(x, y, gates, pos, g_final, final_norm, *, tc=256):
    t, d = x.shape
    tc = min(tc, t)
    grid_spec = pltpu.PrefetchScalarGridSpec(
        num_scalar_prefetch=1, grid=(t // tc,),
        in_specs=[pl.BlockSpec((tc, d), lambda i, *_: (i, 0)),
                  pl.BlockSpec((tc, LANES), lambda i, *_: (i, 0)),
                  pl.BlockSpec((1, d), lambda i, *_: (0, 0)),
                  pl.BlockSpec(memory_space=pl.ANY)],
        out_specs=pl.BlockSpec((tc, d), lambda i, *_: (i, 0)),
        scratch_shapes=[pltpu.VMEM((2, TOP_K, tc, d), F32), pltpu.SemaphoreType.DMA((2,))])
    return pl.pallas_call(
        functools.partial(_combine_kernel, tc=tc, final_norm=final_norm),
        grid_spec=grid_spec,
        out_shape=jax.ShapeDtypeStruct((t, d), F32),
        compiler_params=_params("arbitrary"),
        name="moe_combine",
    )(pos, x, gates, g_final.reshape(1, d), y)


def moe_layer(x, g_norm, w_rg, w_re, w_gate, w_up, w_down, li, g_final, final_norm):
    t, d = x.shape
    h, ri, gates, cnt = moe_router(x, g_norm, w_rg, w_re)
    e_flat = ri[:, 0:TOP_K].reshape(-1)
    rank_flat = ri[:, TOP_K:2 * TOP_K].reshape(-1)
    counts = cnt[0, ROUTER_LANE0:ROUTER_LANE0 + N_EXPERTS]
    nblk = (counts + MOE_BLOCK - 1) // MOE_BLOCK
    blk_end = jnp.cumsum(nblk)
    pad_start = ((blk_end - nblk) * MOE_BLOCK).astype(I32)
    n_used = blk_end[-1:].astype(I32)
    nb = (t * TOP_K) // MOE_BLOCK + N_EXPERTS
    blk_ids = jnp.arange(nb, dtype=I32)
    blk = jnp.minimum(blk_ids, n_used[0] - 1)
    blk_e = jnp.searchsorted(blk_end, blk, side="right").astype(I32)
    is_last = (blk_ids + 1 == blk_end[blk_e]) | (blk_ids >= n_used[0])
    zero_flag = is_last.astype(I32)
    pos = (pad_start[e_flat] + rank_flat).astype(I32)
    xs = moe_dispatch(h, pos, zero_flag, nb * MOE_BLOCK)
    y = moe_experts(xs, blk_e, n_used, w_gate, w_up, w_down, li)
    return moe_combine(x, y, gates, pos, g_final, final_norm)


def kernel(x, rel_bias, norm_mix, norm_ffn, norm_final, w_in_ab, lam_ab, subln_ab, w_gk_ab, b_gk_ab,
           gnorm_ab, w_out_ab, w_in_c, ln_v_g, ln_v_b, w_spatial, b_spatial, w_out_c,
           w_router_group, w_router_expert, w_exp_gate, w_exp_up, w_exp_down):
    bsz, s_len, d = x.shape
    depth = norm_mix.shape[0]
    xf = x.reshape(bsz * s_len, d)
    for layer in range(depth):
        i = layer // 2
        if layer % 2 == 0:
            lam_init = 0.8 - 0.6 * math.exp(-0.3 * layer)
            n_main = 3 * A_WIDTH + 2 * B_HEADS * B_KEY_DIM + 2 * B_WIDTH
            proj, gb = norm_proj(xf, norm_mix[layer], w_in_ab, i, n_main,
                                 w_extra=w_in_ab[i, :, n_main:])
            o_a = diff_attention(proj, rel_bias, lam_ab[i], subln_ab[i], lam_init, bsz, s_len)
            o_b = gla_mixer(proj, gb, w_gk_ab[i], b_gk_ab[i], gnorm_ab[i], bsz, s_len)
            xf = out_proj(o_a, 0, o_b, 0, w_out_ab, i, xf)
        else:
            proj = norm_proj(xf, norm_mix[layer], w_in_c, i, w_in_c.shape[2], act="gelu")
            uz = spatial_gate(proj, ln_v_g[i], ln_v_b[i], w_spatial[i], b_spatial[i])
            xf = out_proj(uz, 0, uz, 1, w_out_c, i, xf)
        xf = moe_layer(xf, norm_ffn[layer], w_router_group[layer], w_router_expert[layer],
                       w_exp_gate, w_exp_up, w_exp_down, layer,
                       norm_final, layer == depth - 1)
    return xf.reshape(bsz, s_len, d)
```

```python
import functools
import math

import jax
import jax.numpy as jnp
import numpy as np
from jax import lax
from jax.experimental import pallas as pl
from jax.experimental.pallas import tpu as pltpu

F32 = jnp.float32
BF16 = jnp.bfloat16
I32 = jnp.int32
HIGHEST = lax.Precision.HIGHEST

EPS = 1e-6
NEG = -1e30
A_HEADS = 8
A_QK_DIM = 64
A_V_DIM = 128
A_WIDTH = A_HEADS * A_V_DIM
N_BUCKETS = 32
MAX_EXACT = 16
MAX_DISTANCE = 128
B_HEADS = 4
B_KEY_DIM = 128
B_VAL_DIM = 256
B_WIDTH = B_HEADS * B_VAL_DIM
GATE_RANK = 16
GATE_TEMP = 16.0
GLA_CHUNK = 64
C_GROUPS = 8
C_CHUNK = 128
N_GROUPS = 4
EXPERTS_PER_GROUP = 8
N_EXPERTS = N_GROUPS * EXPERTS_PER_GROUP
TOP_K = 2
LANES = 128
ROUTER_LANE0 = N_GROUPS
MOE_BLOCK = 256
VMEM_LIMIT = 56 * 1024 * 1024

NT_DIMS = (((1,), (1,)), ((), ()))
TN_DIMS = (((0,), (0,)), ((), ()))


def _t5_thresholds():
    n = np.arange(0, 2 * MAX_DISTANCE)
    nf = np.maximum(n, 1).astype(np.float64)
    large = MAX_EXACT + (np.log(nf / MAX_EXACT) / math.log(MAX_DISTANCE / MAX_EXACT)
                         * (N_BUCKETS - MAX_EXACT)).astype(np.int64)
    bucket = np.where(n < MAX_EXACT, n, np.minimum(large, N_BUCKETS - 1))
    return [int(np.min(n[bucket >= b])) for b in range(N_BUCKETS)]


T5_THRESHOLDS = _t5_thresholds()


def _params(*sem):
    return pltpu.CompilerParams(dimension_semantics=sem, vmem_limit_bytes=VMEM_LIMIT)


def _rms(x, g):
    ms = jnp.mean(x * x, axis=-1, keepdims=True)
    return x * lax.rsqrt(ms + EPS) * g


def _gelu_tanh(x):
    c = math.sqrt(2.0 / math.pi)
    return 0.5 * x * (1.0 + jnp.tanh(c * (x + 0.044715 * (x * x * x))))


def _sigmoid(x):
    return 1.0 / (1.0 + jnp.exp(-x))


def _norm_proj_kernel(*refs, act, has_extra):
    if has_extra:
        x_ref, g_ref, w_ref, we_ref, o_ref, oe_ref, h_scr = refs
    else:
        x_ref, g_ref, w_ref, o_ref, h_scr = refs

    @pl.when(pl.program_id(1) == 0)
    def _():
        hb = _rms(x_ref[...], g_ref[...]).astype(BF16)
        h_scr[...] = hb
        if has_extra:
            oe_ref[...] = jnp.dot(hb, we_ref[...].astype(BF16), preferred_element_type=F32)

    y = jnp.dot(h_scr[...], w_ref[...].astype(BF16), preferred_element_type=F32)
    if act == "gelu":
        y = _gelu_tanh(y)
    o_ref[...] = y.astype(o_ref.dtype)


def norm_proj(x, g, w, li, n_cols, *, act=None, w_extra=None, tm=1024, tn=512):
    t, d = x.shape
    tm = min(tm, t)
    grid = (t // tm, n_cols // tn)
    in_specs = [pl.BlockSpec((tm, d), lambda i, j: (i, 0)),
                pl.BlockSpec((1, d), lambda i, j: (0, 0)),
                pl.BlockSpec((None, d, tn), lambda i, j: (li, 0, j))]
    out_specs = [pl.BlockSpec((tm, tn), lambda i, j: (i, j))]
    out_shape = [jax.ShapeDtypeStruct((t, n_cols), BF16)]
    args = [x, g.reshape(1, d), w]
    if w_extra is not None:
        ne = w_extra.shape[1]
        in_specs.append(pl.BlockSpec((d, ne), lambda i, j: (0, 0)))
        out_specs.append(pl.BlockSpec((tm, ne), lambda i, j: (i, 0)))
        out_shape.append(jax.ShapeDtypeStruct((t, ne), F32))
        args.append(w_extra)
    outs = pl.pallas_call(
        functools.partial(_norm_proj_kernel, act=act, has_extra=w_extra is not None),
        grid=grid, in_specs=in_specs, out_specs=out_specs, out_shape=out_shape,
        scratch_shapes=[pltpu.VMEM((tm, d), BF16)],
        compiler_params=_params("parallel", "arbitrary"),
        name="norm_proj_" + (act or "lin"),
    )(*args)
    return outs if w_extra is not None else outs[0]


def _out_proj_kernel(a0_ref, a1_ref, w_ref, r_ref, o_ref):
    kh = a0_ref.shape[1]
    w = w_ref[...].astype(BF16)
    y = jnp.dot(a0_ref[...], w[:kh], preferred_element_type=F32)
    y = y + jnp.dot(a1_ref[...], w[kh:], preferred_element_type=F32)
    o_ref[...] = r_ref[...] + y


def out_proj(a0, a0_blk, a1, a1_blk, w, li, res, *, tm=1024, tn=512):
    t, n = res.shape
    k = w.shape[1]
    kh = k // 2
    tm = min(tm, t)
    return pl.pallas_call(
        _out_proj_kernel,
        grid=(t // tm, n // tn),
        in_specs=[pl.BlockSpec((tm, kh), lambda i, j: (i, a0_blk)),
                  pl.BlockSpec((tm, kh), lambda i, j: (i, a1_blk)),
                  pl.BlockSpec((None, k, tn), lambda i, j: (li, 0, j)),
                  pl.BlockSpec((tm, tn), lambda i, j: (i, j))],
        out_specs=pl.BlockSpec((tm, tn), lambda i, j: (i, j)),
        out_shape=jax.ShapeDtypeStruct((t, n), F32),
        compiler_params=_params("parallel", "arbitrary"),
        name="out_proj",
    )(a0, a1, w, res)


def _diff_attn_kernel(tbl_ref, q_ref, k_ref, v_ref, lam_ref, sg_ref, o_ref,
                      m_scr, acc_scr, bias_scr, q_scr, v_scr, *, t, lam_init):
    h = pl.program_id(1)
    qi = pl.program_id(2)
    scale = A_QK_DIM ** -0.5

    @pl.when(qi == 0)
    def _build_bias():
        v_scr[:, :A_V_DIM] = v_ref[...]
        v_scr[:, A_V_DIM:] = jnp.ones((v_scr.shape[0], A_V_DIM), BF16)
        row = lax.broadcasted_iota(I32, (t, t), 0)
        col = lax.broadcasted_iota(I32, (t, t), 1)
        for d in range(2):
            rel = row - col + d * t
            acc = jnp.full((t, t), tbl_ref[h], F32)
            for b in range(1, N_BUCKETS):
                acc = jnp.where(rel >= T5_THRESHOLDS[b], tbl_ref[b * A_HEADS + h], acc)
            if d == 0:
                acc = jnp.where(rel >= 0, acc, NEG)
            bias_scr[d] = acc

    m_scr[...] = jnp.full(m_scr.shape, NEG, F32)
    acc_scr[...] = jnp.zeros(acc_scr.shape, F32)
    q = q_ref[...] * scale
    lane = lax.broadcasted_iota(I32, q.shape, 1)
    q_scr[0] = jnp.where(lane < A_QK_DIM, q, jnp.zeros_like(q))
    q_scr[1] = jnp.where(lane >= A_QK_DIM, q, jnp.zeros_like(q))

    def step(ki, bias_tile, bias_const):
        rows = pl.ds(pl.multiple_of(ki * t, t), t)
        k = k_ref[rows, :]
        v_aug = v_scr[rows, :]
        for m in range(2):
            s = lax.dot_general(q_scr[m], k, NT_DIMS, preferred_element_type=F32)
            if bias_tile is not None:
                s = s + bias_tile
            m_old = m_scr[m]
            m_new = jnp.maximum(m_old, jnp.max(s, axis=-1, keepdims=True) + bias_const)
            alpha = jnp.exp(m_old - m_new)
            p = jnp.exp((s - (m_new - bias_const)).astype(BF16))
            acc_scr[m] = alpha * acc_scr[m] + jnp.dot(p, v_aug, preferred_element_type=F32)
            m_scr[m] = m_new

    far_bias = tbl_ref[(N_BUCKETS - 1) * A_HEADS + h]

    def far(ki, carry):
        step(ki, None, far_bias)
        return carry

    lax.fori_loop(0, jnp.maximum(qi - 1, 0), far, 0)

    @pl.when(qi >= 1)
    def _near():
        step(qi - 1, bias_scr[1], 0.0)

    step(qi, bias_scr[0], 0.0)
    lv = lam_ref[...]
    s1 = jnp.sum(lv[0:1] * lv[1:2], axis=-1, keepdims=True)
    s2 = jnp.sum(lv[2:3] * lv[3:4], axis=-1, keepdims=True)
    lam = jnp.exp(s1) - jnp.exp(s2) + lam_init
    a0 = acc_scr[0]
    a1 = acc_scr[1]
    o = (a0[:, :A_V_DIM] / a0[:, A_V_DIM:A_V_DIM + 1]
         - lam * (a1[:, :A_V_DIM] / a1[:, A_V_DIM:A_V_DIM + 1]))
    o = _rms(o, sg_ref[...]) * (1.0 - lam_init)
    o_ref[...] = o.astype(o_ref.dtype)


def diff_attention(proj, rel_bias, lam_vec, subln_g, lam_init, bsz, s_len, *, t=512):
    t = min(t, s_len)
    assert t >= MAX_DISTANCE and s_len % t == 0
    nq = s_len // t
    q_map = lambda b, h, qi: (b * nq + qi, h)
    kv_map = lambda off: (lambda b, h, qi: (b, off + h))
    const = lambda b, h, qi: (0, 0)
    return pl.pallas_call(
        functools.partial(_diff_attn_kernel, t=t, lam_init=lam_init),
        grid=(bsz, A_HEADS, nq),
        in_specs=[pl.BlockSpec(memory_space=pltpu.SMEM),
                  pl.BlockSpec((t, A_V_DIM), q_map),
                  pl.BlockSpec((s_len, A_V_DIM), kv_map(A_HEADS)),
                  pl.BlockSpec((s_len, A_V_DIM), kv_map(2 * A_HEADS)),
                  pl.BlockSpec((4, A_QK_DIM), const),
                  pl.BlockSpec((1, A_V_DIM), const)],
        out_specs=pl.BlockSpec((t, A_V_DIM), q_map),
        out_shape=jax.ShapeDtypeStruct((bsz * s_len, A_WIDTH), BF16),
        scratch_shapes=[pltpu.VMEM((2, t, 1), F32),
                        pltpu.VMEM((2, t, 2 * A_V_DIM), F32),
                        pltpu.VMEM((2, t, t), F32),
                        pltpu.VMEM((2, t, A_V_DIM), BF16),
                        pltpu.VMEM((s_len, 2 * A_V_DIM), BF16)],
        compiler_params=_params("parallel", "arbitrary", "arbitrary"),
        name="diff_attention",
    )(rel_bias.reshape(-1), proj, proj, proj, lam_vec, subln_g.reshape(1, A_V_DIM))


def _gla_kernel(q_ref, k_ref, v_ref, r_ref, gb_ref, wgk_ref, bgk_ref, gn_ref, o_ref, st_scr, *, rows):
    c_len = GLA_CHUNK
    scale = B_KEY_DIM ** -0.5

    @pl.when(pl.program_id(1) == 0)
    def _():
        st_scr[...] = jnp.zeros(st_scr.shape, F32)

    z = jnp.dot(gb_ref[...], wgk_ref[...], preferred_element_type=F32, precision=HIGHEST)
    z = z + bgk_ref[...]
    g_log = (jnp.minimum(z, 0.0) - jnp.log(1.0 + jnp.exp(-jnp.abs(z)))) * (1.0 / GATE_TEMP)
    row = lax.broadcasted_iota(I32, (rows, rows), 0)
    col = lax.broadcasted_iota(I32, (rows, rows), 1)
    ltri = ((row >= col) & ((row // c_len) == (col // c_len))).astype(F32)
    bcum = jnp.dot(ltri, g_log, preferred_element_type=F32, precision=HIGHEST)
    causal = (lax.broadcasted_iota(I32, (c_len, c_len), 0)
              >= lax.broadcasted_iota(I32, (c_len, c_len), 1))
    gn = gn_ref[...]

    for c in range(rows // c_len):
        rs = slice(c * c_len, (c + 1) * c_len)
        for hh in range(B_HEADS):
            ks = slice(hh * B_KEY_DIM, (hh + 1) * B_KEY_DIM)
            vs = slice(hh * B_VAL_DIM, (hh + 1) * B_VAL_DIM)
            b = bcum[rs, ks]
            b_last = b[c_len - 1:c_len, :]
            b_mid = b[c_len // 2:c_len // 2 + 1, :]
            q = q_ref[rs, ks].astype(F32) * scale
            k = k_ref[rs, ks].astype(F32)
            v = v_ref[rs, vs]
            st = st_scr[hh]
            q_in = (q * jnp.exp(b)).astype(BF16)
            q_ia = (q * jnp.exp(b - b_mid)).astype(BF16)
            k_ia = (k * jnp.exp(b_mid - b)).astype(BF16)
            att = lax.dot_general(q_ia, k_ia, NT_DIMS, preferred_element_type=F32)
            att = jnp.where(causal, att, 0.0)
            o = jnp.dot(att.astype(BF16), v, preferred_element_type=F32)
            o = o + lax.dot_general(q_in, st.astype(BF16), NT_DIMS, preferred_element_type=F32)
            k_st = (k * jnp.exp(b_last - b)).astype(BF16)
            st_scr[hh] = st * jnp.exp(b_last) + lax.dot_general(
                v, k_st, TN_DIMS, preferred_element_type=F32)
            on = _rms(o, gn)
            r = r_ref[rs, vs].astype(F32)
            o_ref[rs, vs] = (on * (r * _sigmoid(r))).astype(o_ref.dtype)


def gla_mixer(proj, gb, w_gk, b_gk, gnorm_g, bsz, s_len, *, rows=256):
    rows = min(rows, s_len)
    nr = s_len // rows
    kw = B_HEADS * B_KEY_DIM
    rmap = lambda blk: (lambda b, i: (b * nr + i, blk))
    return pl.pallas_call(
        functools.partial(_gla_kernel, rows=rows),
        grid=(bsz, nr),
        in_specs=[pl.BlockSpec((rows, kw), rmap(3 * A_WIDTH // kw)),
                  pl.BlockSpec((rows, kw), rmap(3 * A_WIDTH // kw + 1)),
                  pl.BlockSpec((rows, B_WIDTH), rmap((3 * A_WIDTH + 2 * kw) // B_WIDTH)),
                  pl.BlockSpec((rows, B_WIDTH), rmap((3 * A_WIDTH + 2 * kw) // B_WIDTH + 1)),
                  pl.BlockSpec((rows, GATE_RANK), rmap(0)),
                  pl.BlockSpec((GATE_RANK, kw), lambda b, i: (0, 0)),
                  pl.BlockSpec((1, kw), lambda b, i: (0, 0)),
                  pl.BlockSpec((1, B_VAL_DIM), lambda b, i: (0, 0))],
        out_specs=pl.BlockSpec((rows, B_WIDTH), rmap(0)),
        out_shape=jax.ShapeDtypeStruct((bsz * s_len, B_WIDTH), BF16),
        scratch_shapes=[pltpu.VMEM((B_HEADS, B_VAL_DIM, B_KEY_DIM), F32)],
        compiler_params=_params("parallel", "arbitrary"),
        name="gla",
    )(proj, proj, proj, proj, gb, w_gk, b_gk.reshape(1, kw), gnorm_g.reshape(1, B_VAL_DIM))


def _spatial_kernel(u_ref, v_ref, lg_ref, lb_ref, ws_ref, bs_ref, o_ref, *, rows):
    v = v_ref[...].astype(F32)
    mu = jnp.mean(v, axis=-1, keepdims=True)
    vc = v - mu
    var = jnp.mean(vc * vc, axis=-1, keepdims=True)
    vn = (vc * lax.rsqrt(var + EPS) * lg_ref[...] + lb_ref[...]).astype(BF16)
    gd = v.shape[1] // C_GROUPS
    mask = (lax.broadcasted_iota(I32, (C_CHUNK, C_CHUNK), 0)
            >= lax.broadcasted_iota(I32, (C_CHUNK, C_CHUNK), 1))
    for g in range(C_GROUPS):
        w = jnp.where(mask, ws_ref[g], 0.0).astype(BF16)
        bias = bs_ref[:, g:g + 1]
        cs = slice(g * gd, (g + 1) * gd)
        for c in range(rows // C_CHUNK):
            rs = slice(c * C_CHUNK, (c + 1) * C_CHUNK)
            z = jnp.dot(w, vn[rs, cs], preferred_element_type=F32) + bias
            o_ref[rs, cs] = (u_ref[rs, cs].astype(F32) * z).astype(o_ref.dtype)


def spatial_gate(proj, ln_g, ln_b, w_s, b_s, *, rows=512):
    t = proj.shape[0]
    cw = proj.shape[1] // 2
    rows = min(rows, t)
    return pl.pallas_call(
        functools.partial(_spatial_kernel, rows=rows),
        grid=(t // rows,),
        in_specs=[pl.BlockSpec((rows, cw), lambda i: (i, 0)),
                  pl.BlockSpec((rows, cw), lambda i: (i, 1)),
                  pl.BlockSpec((1, cw), lambda i: (0, 0)),
                  pl.BlockSpec((1, cw), lambda i: (0, 0)),
                  pl.BlockSpec((C_GROUPS, C_CHUNK, C_CHUNK), lambda i: (0, 0, 0)),
                  pl.BlockSpec((C_CHUNK, C_GROUPS), lambda i: (0, 0))],
        out_specs=pl.BlockSpec((rows, cw), lambda i: (i, 0)),
        out_shape=jax.ShapeDtypeStruct((t, cw), BF16),
        compiler_params=_params("parallel"),
        name="spatial_gate",
    )(proj, proj, ln_g.reshape(1, cw), ln_b.reshape(1, cw), w_s, b_s.T)


def _router_kernel(x_ref, g_ref, wr_ref, h_ref, ri_ref, rg_ref, cnt_ref, carry_scr, *, tr):
    @pl.when(pl.program_id(0) == 0)
    def _():
        carry_scr[...] = jnp.zeros(carry_scr.shape, F32)

    h = _rms(x_ref[...], g_ref[...])
    h_ref[...] = h
    logits = jnp.dot(h, wr_ref[...], preferred_element_type=F32, precision=HIGHEST)
    lane = lax.broadcasted_iota(I32, (tr, LANES), 1)
    lanef = lane.astype(F32)
    big = float(LANES)

    is_g = lane < N_GROUPS
    lg = jnp.where(is_g, logits, NEG)
    gmax = jnp.max(lg, axis=-1, keepdims=True)
    p_grp = 1.0 / jnp.sum(jnp.exp(lg - gmax), axis=-1, keepdims=True)
    gidx = jnp.min(jnp.where(is_g & (lg == gmax), lanef, big), axis=-1, keepdims=True)

    lo = ROUTER_LANE0 + EXPERTS_PER_GROUP * gidx
    in_e = (lanef >= lo) & (lanef < lo + EXPERTS_PER_GROUP)
    le = jnp.where(in_e, logits, NEG)
    emax = jnp.max(le, axis=-1, keepdims=True)
    i1 = jnp.min(jnp.where(in_e & (le == emax), lanef, big), axis=-1, keepdims=True)
    oh1 = lanef == i1
    le2 = jnp.where(oh1, NEG, le)
    emax2 = jnp.max(le2, axis=-1, keepdims=True)
    i2 = jnp.min(jnp.where(in_e & (le2 == emax2) & (~oh1), lanef, big), axis=-1, keepdims=True)
    oh2 = lanef == i2
    ratio = jnp.exp(emax2 - emax)
    g1 = p_grp / (1.0 + ratio)
    g2 = p_grp * ratio / (1.0 + ratio)

    onehot = (oh1 | oh2).astype(BF16)
    tri = (lax.broadcasted_iota(I32, (tr, tr), 0) > lax.broadcasted_iota(I32, (tr, tr), 1))
    before = jnp.dot(tri.astype(BF16), onehot, preferred_element_type=F32) + carry_scr[...]
    rank1 = jnp.sum(jnp.where(oh1, before, 0.0), axis=-1, keepdims=True)
    rank2 = jnp.sum(jnp.where(oh2, before, 0.0), axis=-1, keepdims=True)
    carry = carry_scr[...] + jnp.sum(onehot.astype(F32), axis=0, keepdims=True)
    carry_scr[...] = carry

    e1 = i1 - ROUTER_LANE0
    e2 = i2 - ROUTER_LANE0
    ri = jnp.where(lane == 0, e1, jnp.where(lane == 1, e2,
                                             jnp.where(lane == 2, rank1,
                                                       jnp.where(lane == 3, rank2, 0.0))))
    ri_ref[...] = ri.astype(I32)
    rg_ref[...] = jnp.where(lane == 0, g1, jnp.where(lane == 1, g2, 0.0))
    cnt_ref[...] = jnp.broadcast_to(carry, cnt_ref.shape).astype(I32)


def moe_router(x, g, w_rg, w_re, *, tr=512):
    t, d = x.shape
    tr = min(tr, t)
    w_r = jnp.zeros((d, LANES), F32)
    w_r = w_r.at[:, :N_GROUPS].set(w_rg).at[:, ROUTER_LANE0:ROUTER_LANE0 + N_EXPERTS].set(w_re)
    return pl.pallas_call(
        functools.partial(_router_kernel, tr=tr),
        grid=(t // tr,),
        in_specs=[pl.BlockSpec((tr, d), lambda i: (i, 0)),
                  pl.BlockSpec((1, d), lambda i: (0, 0)),
                  pl.BlockSpec((d, LANES), lambda i: (0, 0))],
        out_specs=[pl.BlockSpec((tr, d), lambda i: (i, 0)),
                   pl.BlockSpec((tr, LANES), lambda i: (i, 0)),
                   pl.BlockSpec((tr, LANES), lambda i: (i, 0)),
                   pl.BlockSpec((8, LANES), lambda i: (0, 0))],
        out_shape=[jax.ShapeDtypeStruct((t, d), F32),
                   jax.ShapeDtypeStruct((t, LANES), I32),
                   jax.ShapeDtypeStruct((t, LANES), F32),
                   jax.ShapeDtypeStruct((8, LANES), I32)],
        scratch_shapes=[pltpu.VMEM((1, LANES), F32)],
        compiler_params=_params("arbitrary"),
        name="moe_router",
    )(x, g.reshape(1, d), w_r)


DMA_UNROLL = 8


def _dispatch_kernel(pos_ref, zf_ref, h_ref, xs_ref, zero_scr, sem, zsem, *, td, nb):
    i = pl.program_id(0)
    base = i * td

    @pl.when(i == 0)
    def _():
        zero_scr[...] = jnp.zeros(zero_scr.shape, zero_scr.dtype)

        def zero_copy(j):
            return pltpu.make_async_copy(zero_scr, xs_ref.at[pl.ds(j * MOE_BLOCK, MOE_BLOCK)], zsem)

        def zstart(j, carry):
            @pl.when(zf_ref[j] == 1)
            def _():
                zero_copy(j).start()
            return carry

        def zwait(j, carry):
            @pl.when(zf_ref[j] == 1)
            def _():
                zero_copy(j).wait()
            return carry

        lax.fori_loop(0, nb, zstart, 0)
        lax.fori_loop(0, nb, zwait, 0)

    def issue(r, carry):
        for kk in range(TOP_K):
            pos = pos_ref[(base + r) * TOP_K + kk]
            pltpu.make_async_copy(h_ref.at[pl.ds(r, 1)], xs_ref.at[pl.ds(pos, 1)], sem).start()
        return carry

    lax.fori_loop(0, td, issue, 0, unroll=DMA_UNROLL)
    for kk in range(TOP_K):
        pltpu.make_async_copy(h_ref, xs_ref.at[pl.ds(0, td)], sem).wait()


def moe_dispatch(h, pos, zero_flag, n_rows, *, td=256):
    t, d = h.shape
    td = min(td, t)
    nb = n_rows // MOE_BLOCK
    grid_spec = pltpu.PrefetchScalarGridSpec(
        num_scalar_prefetch=2, grid=(t // td,),
        in_specs=[pl.BlockSpec((td, d), lambda i, *_: (i, 0))],
        out_specs=pl.BlockSpec(memory_space=pl.ANY),
        scratch_shapes=[pltpu.VMEM((MOE_BLOCK, d), F32),
                        pltpu.SemaphoreType.DMA(()), pltpu.SemaphoreType.DMA(())])
    return pl.pallas_call(
        functools.partial(_dispatch_kernel, td=td, nb=nb),
        grid_spec=grid_spec,
        out_shape=jax.ShapeDtypeStruct((n_rows, d), F32),
        compiler_params=_params("arbitrary"),
        name="moe_dispatch",
    )(pos, zero_flag, h)


def _expert_kernel(blk_e_ref, nused_ref, xs_ref, wg_ref, wu_ref, wd_ref, y_ref, wg_s, wu_s, wd_s):
    j = pl.program_id(0)
    used = j < nused_ref[0]
    new_expert = (j == 0) | (blk_e_ref[j] != blk_e_ref[jnp.maximum(j - 1, 0)])

    @pl.when(used & new_expert)
    def _():
        wg_s[...] = wg_ref[0].astype(BF16)
        wu_s[...] = wu_ref[0].astype(BF16)
        wd_s[...] = wd_ref[0].astype(BF16)

    @pl.when(used)
    def _():
        x = xs_ref[...].astype(BF16)
        gate = jnp.dot(x, wg_s[...], preferred_element_type=F32)
        up = jnp.dot(x, wu_s[...], preferred_element_type=F32)
        act = (gate * _sigmoid(gate) * up).astype(BF16)
        y_ref[...] = jnp.dot(act, wd_s[...], preferred_element_type=F32)

    @pl.when(jnp.logical_not(used))
    def _():
        y_ref[...] = jnp.zeros(y_ref.shape, F32)


def moe_experts(xs, blk_e, n_used, w_gate, w_up, w_down, li):
    n_rows, d = xs.shape
    ff = w_gate.shape[3]
    nb = n_rows // MOE_BLOCK
    row_map = lambda j, be, nu: (jnp.minimum(j, nu[0] - 1), 0)
    grid_spec = pltpu.PrefetchScalarGridSpec(
        num_scalar_prefetch=2, grid=(nb,),
        in_specs=[pl.BlockSpec((MOE_BLOCK, d), row_map),
                  pl.BlockSpec((None, 1, d, ff), lambda j, be, nu: (li, be[j], 0, 0)),
                  pl.BlockSpec((None, 1, d, ff), lambda j, be, nu: (li, be[j], 0, 0)),
                  pl.BlockSpec((None, 1, ff, d), lambda j, be, nu: (li, be[j], 0, 0))],
        out_specs=pl.BlockSpec((MOE_BLOCK, d), lambda j, be, nu: (j, 0)),
        scratch_shapes=[pltpu.VMEM((d, ff), BF16), pltpu.VMEM((d, ff), BF16),
                        pltpu.VMEM((ff, d), BF16)])
    return pl.pallas_call(
        _expert_kernel,
        grid_spec=grid_spec,
        out_shape=jax.ShapeDtypeStruct((n_rows, d), F32),
        compiler_params=_params("arbitrary"),
        name="moe_experts",
    )(blk_e, n_used, xs, w_gate, w_up, w_down)


def _combine_kernel(pos_ref, x_ref, rg_ref, gf_ref, y_ref, o_ref, ybuf, sems, *, tc, final_norm):
    i = pl.program_id(0)
    slot = i % 2

    def issue_tile(tile, dst_slot):
        base = tile * tc

        def issue(r, carry):
            for kk in range(TOP_K):
                pos = pos_ref[(base + r) * TOP_K + kk]
                pltpu.make_async_copy(y_ref.at[pl.ds(pos, 1)], ybuf.at[dst_slot, kk, pl.ds(r, 1)],
                                      sems.at[dst_slot]).start()
            return carry

        lax.fori_loop(0, tc, issue, 0, unroll=DMA_UNROLL)

    @pl.when(i == 0)
    def _():
        issue_tile(0, 0)

    @pl.when(i + 1 < pl.num_programs(0))
    def _():
        issue_tile(i + 1, 1 - slot)

    for kk in range(TOP_K):
        pltpu.make_async_copy(y_ref.at[pl.ds(0, tc)], ybuf.at[slot, kk], sems.at[slot]).wait()
    gates = rg_ref[...]
    out = x_ref[...] + (gates[:, 0:1] * ybuf[slot, 0] + gates[:, 1:2] * ybuf[slot, 1])
    if final_norm:
        out = _rms(out, gf_ref[...])
    o_ref[...] = out


def moe_combine(x, y, gates, pos, g_final, final_norm, *, tc=256):
    t, d = x.shape
    tc = min(tc, t)
    grid_spec = pltpu.PrefetchScalarGridSpec(
        num_scalar_prefetch=1, grid=(t // tc,),
        in_specs=[pl.BlockSpec((tc, d), lambda i, *_: (i, 0)),
                  pl.BlockSpec((tc, LANES), lambda i, *_: (i, 0)),
                  pl.BlockSpec((1, d), lambda i, *_: (0, 0)),
                  pl.BlockSpec(memory_space=pl.ANY)],
        out_specs=pl.BlockSpec((tc, d), lambda i, *_: (i, 0)),
        scratch_shapes=[pltpu.VMEM((2, TOP_K, tc, d), F32), pltpu.SemaphoreType.DMA((2,))])
    return pl.pallas_call(
        functools.partial(_combine_kernel, tc=tc, final_norm=final_norm),
        grid_spec=grid_spec,
        out_shape=jax.ShapeDtypeStruct((t, d), F32),
        compiler_params=_params("arbitrary"),
        name="moe_combine",
    )(pos, x, gates, g_final.reshape(1, d), y)


def moe_layer(x, g_norm, w_rg, w_re, w_gate, w_up, w_down, li, g_final, final_norm):
    t, d = x.shape
    h, ri, gates, cnt = moe_router(x, g_norm, w_rg, w_re)
    e_flat = ri[:, 0:TOP_K].reshape(-1)
    rank_flat = ri[:, TOP_K:2 * TOP_K].reshape(-1)
    counts = cnt[0, ROUTER_LANE0:ROUTER_LANE0 + N_EXPERTS]
    nblk = (counts + MOE_BLOCK - 1) // MOE_BLOCK
    blk_end = jnp.cumsum(nblk)
    pad_start = ((blk_end - nblk) * MOE_BLOCK).astype(I32)
    n_used = blk_end[-1:].astype(I32)
    nb = (t * TOP_K) // MOE_BLOCK + N_EXPERTS
    blk_ids = jnp.arange(nb, dtype=I32)
    blk = jnp.minimum(blk_ids, n_used[0] - 1)
    blk_e = jnp.searchsorted(blk_end, blk, side="right").astype(I32)
    is_last = (blk_ids + 1 == blk_end[blk_e]) | (blk_ids >= n_used[0])
    zero_flag = is_last.astype(I32)
    pos = (pad_start[e_flat] + rank_flat).astype(I32)
    xs = moe_dispatch(h, pos, zero_flag, nb * MOE_BLOCK)
    y = moe_experts(xs, blk_e, n_used, w_gate, w_up, w_down, li)
    return moe_combine(x, y, gates, pos, g_final, final_norm)


def kernel(x, rel_bias, norm_mix, norm_ffn, norm_final, w_in_ab, lam_ab, subln_ab, w_gk_ab, b_gk_ab,
           gnorm_ab, w_out_ab, w_in_c, ln_v_g, ln_v_b, w_spatial, b_spatial, w_out_c,
           w_router_group, w_router_expert, w_exp_gate, w_exp_up, w_exp_down):
    bsz, s_len, d = x.shape
    depth = norm_mix.shape[0]
    xf = x.reshape(bsz * s_len, d)
    for layer in range(depth):
        i = layer // 2
        if layer % 2 == 0:
            lam_init = 0.8 - 0.6 * math.exp(-0.3 * layer)
            n_main = 3 * A_WIDTH + 2 * B_HEADS * B_KEY_DIM + 2 * B_WIDTH
            proj, gb = norm_proj(xf, norm_mix[layer], w_in_ab, i, n_main,
                                 w_extra=w_in_ab[i, :, n_main:])
            o_a = diff_attention(proj, rel_bias, lam_ab[i], subln_ab[i], lam_init, bsz, s_len)
            o_b = gla_mixer(proj, gb, w_gk_ab[i], b_gk_ab[i], gnorm_ab[i], bsz, s_len)
            xf = out_proj(o_a, 0, o_b, 0, w_out_ab, i, xf)
        else:
            proj = norm_proj(xf, norm_mix[layer], w_in_c, i, w_in_c.shape[2], act="gelu")
            uz = spatial_gate(proj, ln_v_g[i], ln_v_b[i], w_spatial[i], b_spatial[i])
            xf = out_proj(uz, 0, uz, 1, w_out_c, i, xf)
        xf = moe_layer(xf, norm_ffn[layer], w_router_group[layer], w_router_expert[layer],
                       w_exp_gate, w_exp_up, w_exp_down, layer,
                       norm_final, layer == depth - 1)
    return xf.reshape(bsz, s_len, d)
```

```python
import functools
import math

import jax
import jax.numpy as jnp
import numpy as np
from jax import lax
from jax.experimental import pallas as pl
from jax.experimental.pallas import tpu as pltpu

F32 = jnp.float32
BF16 = jnp.bfloat16
I32 = jnp.int32
HIGHEST = lax.Precision.HIGHEST

EPS = 1e-6
NEG = -1e30
A_HEADS = 8
A_QK_DIM = 64
A_V_DIM = 128
A_WIDTH = A_HEADS * A_V_DIM
N_BUCKETS = 32
MAX_EXACT = 16
MAX_DISTANCE = 128
B_HEADS = 4
B_KEY_DIM = 128
B_VAL_DIM = 256
B_WIDTH = B_HEADS * B_VAL_DIM
GATE_RANK = 16
GATE_TEMP = 16.0
GLA_CHUNK = 64
C_GROUPS = 8
C_CHUNK = 128
N_GROUPS = 4
EXPERTS_PER_GROUP = 8
N_EXPERTS = N_GROUPS * EXPERTS_PER_GROUP
TOP_K = 2
LANES = 128
ROUTER_LANE0 = N_GROUPS
MOE_BLOCK = 256
VMEM_LIMIT = 56 * 1024 * 1024

NT_DIMS = (((1,), (1,)), ((), ()))
TN_DIMS = (((0,), (0,)), ((), ()))


def _t5_thresholds():
    n = np.arange(0, 2 * MAX_DISTANCE)
    nf = np.maximum(n, 1).astype(np.float64)
    large = MAX_EXACT + (np.log(nf / MAX_EXACT) / math.log(MAX_DISTANCE / MAX_EXACT)
                         * (N_BUCKETS - MAX_EXACT)).astype(np.int64)
    bucket = np.where(n < MAX_EXACT, n, np.minimum(large, N_BUCKETS - 1))
    return [int(np.min(n[bucket >= b])) for b in range(N_BUCKETS)]


T5_THRESHOLDS = _t5_thresholds()


def _params(*sem):
    return pltpu.CompilerParams(dimension_semantics=sem, vmem_limit_bytes=VMEM_LIMIT)


def _rms(x, g):
    ms = jnp.mean(x * x, axis=-1, keepdims=True)
    return x * lax.rsqrt(ms + EPS) * g


def _gelu_tanh(x):
    c = math.sqrt(2.0 / math.pi)
    return 0.5 * x * (1.0 + jnp.tanh(c * (x + 0.044715 * (x * x * x))))


def _sigmoid(x):
    return 1.0 / (1.0 + jnp.exp(-x))


def _norm_proj_kernel(*refs, act, has_extra):
    if has_extra:
        x_ref, g_ref, w_ref, we_ref, o_ref, oe_ref, h_scr = refs
    else:
        x_ref, g_ref, w_ref, o_ref, h_scr = refs

    @pl.when(pl.program_id(1) == 0)
    def _():
        hb = _rms(x_ref[...], g_ref[...]).astype(BF16)
        h_scr[...] = hb
        if has_extra:
            oe_ref[...] = jnp.dot(hb, we_ref[...].astype(BF16), preferred_element_type=F32)

    y = jnp.dot(h_scr[...], w_ref[...].astype(BF16), preferred_element_type=F32)
    if act == "gelu":
        y = _gelu_tanh(y)
    o_ref[...] = y.astype(o_ref.dtype)


def norm_proj(x, g, w, li, n_cols, *, act=None, w_extra=None, tm=1024, tn=512):
    t, d = x.shape
    tm = min(tm, t)
    grid = (t // tm, n_cols // tn)
    in_specs = [pl.BlockSpec((tm, d), lambda i, j: (i, 0)),
                pl.BlockSpec((1, d), lambda i, j: (0, 0)),
                pl.BlockSpec((None, d, tn), lambda i, j: (li, 0, j))]
    out_specs = [pl.BlockSpec((tm, tn), lambda i, j: (i, j))]
    out_shape = [jax.ShapeDtypeStruct((t, n_cols), BF16)]
    args = [x, g.reshape(1, d), w]
    if w_extra is not None:
        ne = w_extra.shape[1]
        in_specs.append(pl.BlockSpec((d, ne), lambda i, j: (0, 0)))
        out_specs.append(pl.BlockSpec((tm, ne), lambda i, j: (i, 0)))
        out_shape.append(jax.ShapeDtypeStruct((t, ne), F32))
        args.append(w_extra)
    outs = pl.pallas_call(
        functools.partial(_norm_proj_kernel, act=act, has_extra=w_extra is not None),
        grid=grid, in_specs=in_specs, out_specs=out_specs, out_shape=out_shape,
        scratch_shapes=[pltpu.VMEM((tm, d), BF16)],
        compiler_params=_params("parallel", "arbitrary"),
        name="norm_proj_" + (act or "lin"),
    )(*args)
    return outs if w_extra is not None else outs[0]


def _out_proj_kernel(a0_ref, a1_ref, w_ref, r_ref, o_ref):
    kh = a0_ref.shape[1]
    w = w_ref[...].astype(BF16)
    y = jnp.dot(a0_ref[...], w[:kh], preferred_element_type=F32)
    y = y + jnp.dot(a1_ref[...], w[kh:], preferred_element_type=F32)
    o_ref[...] = r_ref[...] + y


def out_proj(a0, a0_blk, a1, a1_blk, w, li, res, *, tm=1024, tn=512):
    t, n = res.shape
    k = w.shape[1]
    kh = k // 2
    tm = min(tm, t)
    return pl.pallas_call(
        _out_proj_kernel,
        grid=(t // tm, n // tn),
        in_specs=[pl.BlockSpec((tm, kh), lambda i, j: (i, a0_blk)),
                  pl.BlockSpec((tm, kh), lambda i, j: (i, a1_blk)),
                  pl.BlockSpec((None, k, tn), lambda i, j: (li, 0, j)),
                  pl.BlockSpec((tm, tn), lambda i, j: (i, j))],
        out_specs=pl.BlockSpec((tm, tn), lambda i, j: (i, j)),
        out_shape=jax.ShapeDtypeStruct((t, n), F32),
        compiler_params=_params("parallel", "arbitrary"),
        name="out_proj",
    )(a0, a1, w, res)


def _diff_attn_kernel(tbl_ref, q_ref, k_ref, v_ref, lam_ref, sg_ref, o_ref,
                      m_scr, acc_scr, bias_scr, q_scr, v_scr, *, t, lam_init):
    h = pl.program_id(1)
    qi = pl.program_id(2)
    scale = A_QK_DIM ** -0.5

    @pl.when(qi == 0)
    def _build_bias():
        v_scr[:, :A_V_DIM] = v_ref[...]
        v_scr[:, A_V_DIM:] = jnp.ones((v_scr.shape[0], A_V_DIM), BF16)
        row = lax.broadcasted_iota(I32, (t, t), 0)
        col = lax.broadcasted_iota(I32, (t, t), 1)
        for d in range(2):
            rel = row - col + d * t
            acc = jnp.full((t, t), tbl_ref[h], F32)
            for b in range(1, N_BUCKETS):
                acc = jnp.where(rel >= T5_THRESHOLDS[b], tbl_ref[b * A_HEADS + h], acc)
            if d == 0:
                acc = jnp.where(rel >= 0, acc, NEG)
            bias_scr[d] = acc

    m_scr[...] = jnp.full(m_scr.shape, NEG, F32)
    acc_scr[...] = jnp.zeros(acc_scr.shape, F32)
    q = q_ref[...] * scale
    lane = lax.broadcasted_iota(I32, q.shape, 1)
    q_scr[0] = jnp.where(lane < A_QK_DIM, q, jnp.zeros_like(q))
    q_scr[1] = jnp.where(lane >= A_QK_DIM, q, jnp.zeros_like(q))

    def step(ki, bias_tile, bias_const):
        rows = pl.ds(pl.multiple_of(ki * t, t), t)
        k = k_ref[rows, :]
        v_aug = v_scr[rows, :]
        for m in range(2):
            s = lax.dot_general(q_scr[m], k, NT_DIMS, preferred_element_type=F32)
            if bias_tile is not None:
                s = s + bias_tile
            m_old = m_scr[m]
            m_new = jnp.maximum(m_old, jnp.max(s, axis=-1, keepdims=True) + bias_const)
            alpha = jnp.exp(m_old - m_new)
            p = jnp.exp((s - (m_new - bias_const)).astype(BF16))
            acc_scr[m] = alpha * acc_scr[m] + jnp.dot(p, v_aug, preferred_element_type=F32)
            m_scr[m] = m_new

    far_bias = tbl_ref[(N_BUCKETS - 1) * A_HEADS + h]

    def far(ki, carry):
        step(ki, None, far_bias)
        return carry

    lax.fori_loop(0, jnp.maximum(qi - 1, 0), far, 0)

    @pl.when(qi >= 1)
    def _near():
        step(qi - 1, bias_scr[1], 0.0)

    step(qi, bias_scr[0], 0.0)
    lv = lam_ref[...]
    s1 = jnp.sum(lv[0:1] * lv[1:2], axis=-1, keepdims=True)
    s2 = jnp.sum(lv[2:3] * lv[3:4], axis=-1, keepdims=True)
    lam = jnp.exp(s1) - jnp.exp(s2) + lam_init
    a0 = acc_scr[0]
    a1 = acc_scr[1]
    o = (a0[:, :A_V_DIM] / a0[:, A_V_DIM:A_V_DIM + 1]
         - lam * (a1[:, :A_V_DIM] / a1[:, A_V_DIM:A_V_DIM + 1]))
    o = _rms(o, sg_ref[...]) * (1.0 - lam_init)
    o_ref[...] = o.astype(o_ref.dtype)


def diff_attention(proj, rel_bias, lam_vec, subln_g, lam_init, bsz, s_len, *, t=512):
    t = min(t, s_len)
    assert t >= MAX_DISTANCE and s_len % t == 0
    nq = s_len // t
    q_map = lambda b, h, qi: (b * nq + qi, h)
    kv_map = lambda off: (lambda b, h, qi: (b, off + h))
    const = lambda b, h, qi: (0, 0)
    return pl.pallas_call(
        functools.partial(_diff_attn_kernel, t=t, lam_init=lam_init),
        grid=(bsz, A_HEADS, nq),
        in_specs=[pl.BlockSpec(memory_space=pltpu.SMEM),
                  pl.BlockSpec((t, A_V_DIM), q_map),
                  pl.BlockSpec((s_len, A_V_DIM), kv_map(A_HEADS)),
                  pl.BlockSpec((s_len, A_V_DIM), kv_map(2 * A_HEADS)),
                  pl.BlockSpec((4, A_QK_DIM), const),
                  pl.BlockSpec((1, A_V_DIM), const)],
        out_specs=pl.BlockSpec((t, A_V_DIM), q_map),
        out_shape=jax.ShapeDtypeStruct((bsz * s_len, A_WIDTH), BF16),
        scratch_shapes=[pltpu.VMEM((2, t, 1), F32),
                        pltpu.VMEM((2, t, 2 * A_V_DIM), F32),
                        pltpu.VMEM((2, t, t), F32),
                        pltpu.VMEM((2, t, A_V_DIM), BF16),
                        pltpu.VMEM((s_len, 2 * A_V_DIM), BF16)],
        compiler_params=_params("parallel", "arbitrary", "arbitrary"),
        name="diff_attention",
    )(rel_bias.reshape(-1), proj, proj, proj, lam_vec, subln_g.reshape(1, A_V_DIM))


def _gla_kernel(q_ref, k_ref, v_ref, r_ref, gb_ref, wgk_ref, bgk_ref, gn_ref, o_ref, st_scr, *, rows):
    c_len = GLA_CHUNK
    scale = B_KEY_DIM ** -0.5

    @pl.when(pl.program_id(1) == 0)
    def _():
        st_scr[...] = jnp.zeros(st_scr.shape, F32)

    z = jnp.dot(gb_ref[...], wgk_ref[...], preferred_element_type=F32, precision=HIGHEST)
    z = z + bgk_ref[...]
    g_log = (jnp.minimum(z, 0.0) - jnp.log(1.0 + jnp.exp(-jnp.abs(z)))) * (1.0 / GATE_TEMP)
    row = lax.broadcasted_iota(I32, (rows, rows), 0)
    col = lax.broadcasted_iota(I32, (rows, rows), 1)
    ltri = ((row >= col) & ((row // c_len) == (col // c_len))).astype(F32)
    bcum = jnp.dot(ltri, g_log, preferred_element_type=F32, precision=HIGHEST)
    causal = (lax.broadcasted_iota(I32, (c_len, c_len), 0)
              >= lax.broadcasted_iota(I32, (c_len, c_len), 1))
    gn = gn_ref[...]

    for c in range(rows // c_len):
        rs = slice(c * c_len, (c + 1) * c_len)
        for hh in range(B_HEADS):
            ks = slice(hh * B_KEY_DIM, (hh + 1) * B_KEY_DIM)
            vs = slice(hh * B_VAL_DIM, (hh + 1) * B_VAL_DIM)
            b = bcum[rs, ks]
            b_last = b[c_len - 1:c_len, :]
            b_mid = b[c_len // 2:c_len // 2 + 1, :]
            q = q_ref[rs, ks].astype(F32) * scale
            k = k_ref[rs, ks].astype(F32)
            v = v_ref[rs, vs]
            st = st_scr[hh]
            q_in = (q * jnp.exp(b)).astype(BF16)
            q_ia = (q * jnp.exp(b - b_mid)).astype(BF16)
            k_ia = (k * jnp.exp(b_mid - b)).astype(BF16)
            att = lax.dot_general(q_ia, k_ia, NT_DIMS, preferred_element_type=F32)
            att = jnp.where(causal, att, 0.0)
            o = jnp.dot(att.astype(BF16), v, preferred_element_type=F32)
            o = o + lax.dot_general(q_in, st.astype(BF16), NT_DIMS, preferred_element_type=F32)
            k_st = (k * jnp.exp(b_last - b)).astype(BF16)
            st_scr[hh] = st * jnp.exp(b_last) + lax.dot_general(
                v, k_st, TN_DIMS, preferred_element_type=F32)
            on = _rms(o, gn)
            r = r_ref[rs, vs].astype(F32)
            o_ref[rs, vs] = (on * (r * _sigmoid(r))).astype(o_ref.dtype)


def gla_mixer(proj, gb, w_gk, b_gk, gnorm_g, bsz, s_len, *, rows=256):
    rows = min(rows, s_len)
    nr = s_len // rows
    kw = B_HEADS * B_KEY_DIM
    rmap = lambda blk: (lambda b, i: (b * nr + i, blk))
    return pl.pallas_call(
        functools.partial(_gla_kernel, rows=rows),
        grid=(bsz, nr),
        in_specs=[pl.BlockSpec((rows, kw), rmap(3 * A_WIDTH // kw)),
                  pl.BlockSpec((rows, kw), rmap(3 * A_WIDTH // kw + 1)),
                  pl.BlockSpec((rows, B_WIDTH), rmap((3 * A_WIDTH + 2 * kw) // B_WIDTH)),
                  pl.BlockSpec((rows, B_WIDTH), rmap((3 * A_WIDTH + 2 * kw) // B_WIDTH + 1)),
                  pl.BlockSpec((rows, GATE_RANK), rmap(0)),
                  pl.BlockSpec((GATE_RANK, kw), lambda b, i: (0, 0)),
                  pl.BlockSpec((1, kw), lambda b, i: (0, 0)),
                  pl.BlockSpec((1, B_VAL_DIM), lambda b, i: (0, 0))],
        out_specs=pl.BlockSpec((rows, B_WIDTH), rmap(0)),
        out_shape=jax.ShapeDtypeStruct((bsz * s_len, B_WIDTH), BF16),
        scratch_shapes=[pltpu.VMEM((B_HEADS, B_VAL_DIM, B_KEY_DIM), F32)],
        compiler_params=_params("parallel", "arbitrary"),
        name="gla",
    )(proj, proj, proj, proj, gb, w_gk, b_gk.reshape(1, kw), gnorm_g.reshape(1, B_VAL_DIM))


def _spatial_kernel(u_ref, v_ref, lg_ref, lb_ref, ws_ref, bs_ref, o_ref, *, rows):
    v = v_ref[...].astype(F32)
    mu = jnp.mean(v, axis=-1, keepdims=True)
    vc = v - mu
    var = jnp.mean(vc * vc, axis=-1, keepdims=True)
    vn = (vc * lax.rsqrt(var + EPS) * lg_ref[...] + lb_ref[...]).astype(BF16)
    gd = v.shape[1] // C_GROUPS
    mask = (lax.broadcasted_iota(I32, (C_CHUNK, C_CHUNK), 0)
            >= lax.broadcasted_iota(I32, (C_CHUNK, C_CHUNK), 1))
    for g in range(C_GROUPS):
        w = jnp.where(mask, ws_ref[g], 0.0).astype(BF16)
        bias = bs_ref[:, g:g + 1]
        cs = slice(g * gd, (g + 1) * gd)
        for c in range(rows // C_CHUNK):
            rs = slice(c * C_CHUNK, (c + 1) * C_CHUNK)
            z = jnp.dot(w, vn[rs, cs], preferred_element_type=F32) + bias
            o_ref[rs, cs] = (u_ref[rs, cs].astype(F32) * z).astype(o_ref.dtype)


def spatial_gate(proj, ln_g, ln_b, w_s, b_s, *, rows=512):
    t = proj.shape[0]
    cw = proj.shape[1] // 2
    rows = min(rows, t)
    return pl.pallas_call(
        functools.partial(_spatial_kernel, rows=rows),
        grid=(t // rows,),
        in_specs=[pl.BlockSpec((rows, cw), lambda i: (i, 0)),
                  pl.BlockSpec((rows, cw), lambda i: (i, 1)),
                  pl.BlockSpec((1, cw), lambda i: (0, 0)),
                  pl.BlockSpec((1, cw), lambda i: (0, 0)),
                  pl.BlockSpec((C_GROUPS, C_CHUNK, C_CHUNK), lambda i: (0, 0, 0)),
                  pl.BlockSpec((C_CHUNK, C_GROUPS), lambda i: (0, 0))],
        out_specs=pl.BlockSpec((rows, cw), lambda i: (i, 0)),
        out_shape=jax.ShapeDtypeStruct((t, cw), BF16),
        compiler_params=_params("parallel"),
        name="spatial_gate",
    )(proj, proj, ln_g.reshape(1, cw), ln_b.reshape(1, cw), w_s, b_s.T)


def _router_kernel(x_ref, g_ref, wr_ref, h_ref, ri_ref, rg_ref, cnt_ref, carry_scr, *, tr):
    @pl.when(pl.program_id(0) == 0)
    def _():
        carry_scr[...] = jnp.zeros(carry_scr.shape, F32)

    h = _rms(x_ref[...], g_ref[...])
    h_ref[...] = h
    logits = jnp.dot(h, wr_ref[...], preferred_element_type=F32, precision=HIGHEST)
    lane = lax.broadcasted_iota(I32, (tr, LANES), 1)
    lanef = lane.astype(F32)
    big = float(LANES)

    is_g = lane < N_GROUPS
    lg = jnp.where(is_g, logits, NEG)
    gmax = jnp.max(lg, axis=-1, keepdims=True)
    p_grp = 1.0 / jnp.sum(jnp.exp(lg - gmax), axis=-1, keepdims=True)
    gidx = jnp.min(jnp.where(is_g & (lg == gmax), lanef, big), axis=-1, keepdims=True)

    lo = ROUTER_LANE0 + EXPERTS_PER_GROUP * gidx
    in_e = (lanef >= lo) & (lanef < lo + EXPERTS_PER_GROUP)
    le = jnp.where(in_e, logits, NEG)
    emax = jnp.max(le, axis=-1, keepdims=True)
    i1 = jnp.min(jnp.where(in_e & (le == emax), lanef, big), axis=-1, keepdims=True)
    oh1 = lanef == i1
    le2 = jnp.where(oh1, NEG, le)
    emax2 = jnp.max(le2, axis=-1, keepdims=True)
    i2 = jnp.min(jnp.where(in_e & (le2 == emax2) & (~oh1), lanef, big), axis=-1, keepdims=True)
    oh2 = lanef == i2
    ratio = jnp.exp(emax2 - emax)
    g1 = p_grp / (1.0 + ratio)
    g2 = p_grp * ratio / (1.0 + ratio)

    onehot = (oh1 | oh2).astype(BF16)
    tri = (lax.broadcasted_iota(I32, (tr, tr), 0) > lax.broadcasted_iota(I32, (tr, tr), 1))
    before = jnp.dot(tri.astype(BF16), onehot, preferred_element_type=F32) + carry_scr[...]
    rank1 = jnp.sum(jnp.where(oh1, before, 0.0), axis=-1, keepdims=True)
    rank2 = jnp.sum(jnp.where(oh2, before, 0.0), axis=-1, keepdims=True)
    carry = carry_scr[...] + jnp.sum(onehot.astype(F32), axis=0, keepdims=True)
    carry_scr[...] = carry

    e1 = i1 - ROUTER_LANE0
    e2 = i2 - ROUTER_LANE0
    ri = jnp.where(lane == 0, e1, jnp.where(lane == 1, e2,
                                             jnp.where(lane == 2, rank1,
                                                       jnp.where(lane == 3, rank2, 0.0))))
    ri_ref[...] = ri.astype(I32)
    rg_ref[...] = jnp.where(lane == 0, g1, jnp.where(lane == 1, g2, 0.0))
    cnt_ref[...] = jnp.broadcast_to(carry, cnt_ref.shape).astype(I32)


def moe_router(x, g, w_rg, w_re, *, tr=512):
    t, d = x.shape
    tr = min(tr, t)
    w_r = jnp.zeros((d, LANES), F32)
    w_r = w_r.at[:, :N_GROUPS].set(w_rg).at[:, ROUTER_LANE0:ROUTER_LANE0 + N_EXPERTS].set(w_re)
    return pl.pallas_call(
        functools.partial(_router_kernel, tr=tr),
        grid=(t // tr,),
        in_specs=[pl.BlockSpec((tr, d), lambda i: (i, 0)),
                  pl.BlockSpec((1, d), lambda i: (0, 0)),
                  pl.BlockSpec((d, LANES), lambda i: (0, 0))],
        out_specs=[pl.BlockSpec((tr, d), lambda i: (i, 0)),
                   pl.BlockSpec((tr, LANES), lambda i: (i, 0)),
                   pl.BlockSpec((tr, LANES), lambda i: (i, 0)),
                   pl.BlockSpec((8, LANES), lambda i: (0, 0))],
        out_shape=[jax.ShapeDtypeStruct((t, d), F32),
                   jax.ShapeDtypeStruct((t, LANES), I32),
                   jax.ShapeDtypeStruct((t, LANES), F32),
                   jax.ShapeDtypeStruct((8, LANES), I32)],
        scratch_shapes=[pltpu.VMEM((1, LANES), F32)],
        compiler_params=_params("arbitrary"),
        name="moe_router",
    )(x, g.reshape(1, d), w_r)


DMA_UNROLL = 8


def _dispatch_kernel(pos_ref, zf_ref, h_ref, xs_ref, zero_scr, sem, zsem, *, td, nb):
    i = pl.program_id(0)
    base = i * td

    @pl.when(i == 0)
    def _():
        zero_scr[...] = jnp.zeros(zero_scr.shape, zero_scr.dtype)

        def zero_copy(j):
            return pltpu.make_async_copy(zero_scr, xs_ref.at[pl.ds(j * MOE_BLOCK, MOE_BLOCK)], zsem)

        def zstart(j, carry):
            @pl.when(zf_ref[j] == 1)
            def _():
                zero_copy(j).start()
            return carry

        def zwait(j, carry):
            @pl.when(zf_ref[j] == 1)
            def _():
                zero_copy(j).wait()
            return carry

        lax.fori_loop(0, nb, zstart, 0)
        lax.fori_loop(0, nb, zwait, 0)

    def issue(r, carry):
        for kk in range(TOP_K):
            pos = pos_ref[(base + r) * TOP_K + kk]
            pltpu.make_async_copy(h_ref.at[pl.ds(r, 1)], xs_ref.at[pl.ds(pos, 1)], sem).start()
        return carry

    lax.fori_loop(0, td, issue, 0, unroll=DMA_UNROLL)
    for kk in range(TOP_K):
        pltpu.make_async_copy(h_ref, xs_ref.at[pl.ds(0, td)], sem).wait()


def moe_dispatch(h, pos, zero_flag, n_rows, *, td=256):
    t, d = h.shape
    td = min(td, t)
    nb = n_rows // MOE_BLOCK
    grid_spec = pltpu.PrefetchScalarGridSpec(
        num_scalar_prefetch=2, grid=(t // td,),
        in_specs=[pl.BlockSpec((td, d), lambda i, *_: (i, 0))],
        out_specs=pl.BlockSpec(memory_space=pl.ANY),
        scratch_shapes=[pltpu.VMEM((MOE_BLOCK, d), F32),
                        pltpu.SemaphoreType.DMA(()), pltpu.SemaphoreType.DMA(())])
    return pl.pallas_call(
        functools.partial(_dispatch_kernel, td=td, nb=nb),
        grid_spec=grid_spec,
        out_shape=jax.ShapeDtypeStruct((n_rows, d), F32),
        compiler_params=_params("arbitrary"),
        name="moe_dispatch",
    )(pos, zero_flag, h)


def _expert_kernel(blk_e_ref, first_ref, ord_ref, next_ref, nused_ref, xs_ref, wg_hbm, wu_hbm, wd_hbm,
                   y_ref, wg_f, wu_f, wd_f, wg_s, wu_s, wd_s, sems, *, li):
    j = pl.program_id(0)
    used = j < nused_ref[0]

    def weight_copies(e, slot):
        return [pltpu.make_async_copy(src.at[li, e], dst.at[slot], sems.at[slot])
                for src, dst in ((wg_hbm, wg_f), (wu_hbm, wu_f), (wd_hbm, wd_f))]

    @pl.when(j == 0)
    def _():
        for c in weight_copies(blk_e_ref[0], 0):
            c.start()

    @pl.when(used & (first_ref[j] == 1))
    def _():
        slot = ord_ref[j] % 2
        for c in weight_copies(blk_e_ref[j], slot):
            c.wait()
        nxt = next_ref[j]

        @pl.when(nxt < N_EXPERTS)
        def _():
            for c in weight_copies(nxt, 1 - slot):
                c.start()

        wg_s[...] = wg_f[slot].astype(BF16)
        wu_s[...] = wu_f[slot].astype(BF16)
        wd_s[...] = wd_f[slot].astype(BF16)

    @pl.when(used)
    def _():
        x = xs_ref[...].astype(BF16)
        gate = jnp.dot(x, wg_s[...], preferred_element_type=F32)
        up = jnp.dot(x, wu_s[...], preferred_element_type=F32)
        act = (gate * _sigmoid(gate) * up).astype(BF16)
        y_ref[...] = jnp.dot(act, wd_s[...], preferred_element_type=F32)

    @pl.when(jnp.logical_not(used))
    def _():
        y_ref[...] = jnp.zeros(y_ref.shape, F32)


def moe_experts(xs, blk_e, blk_first, blk_ord, blk_next, n_used, w_gate, w_up, w_down, li):
    n_rows, d = xs.shape
    ff = w_gate.shape[3]
    nb = n_rows // MOE_BLOCK
    grid_spec = pltpu.PrefetchScalarGridSpec(
        num_scalar_prefetch=5, grid=(nb,),
        in_specs=[pl.BlockSpec((MOE_BLOCK, d), lambda j, be, bf, bo, bn, nu: (jnp.minimum(j, nu[0] - 1), 0)),
                  pl.BlockSpec(memory_space=pl.ANY),
                  pl.BlockSpec(memory_space=pl.ANY),
                  pl.BlockSpec(memory_space=pl.ANY)],
        out_specs=pl.BlockSpec((MOE_BLOCK, d), lambda j, *_: (j, 0)),
        scratch_shapes=[pltpu.VMEM((2, d, ff), F32), pltpu.VMEM((2, d, ff), F32),
                        pltpu.VMEM((2, ff, d), F32),
                        pltpu.VMEM((d, ff), BF16), pltpu.VMEM((d, ff), BF16),
                        pltpu.VMEM((ff, d), BF16),
                        pltpu.SemaphoreType.DMA((2,))])
    return pl.pallas_call(
        functools.partial(_expert_kernel, li=li),
        grid_spec=grid_spec,
        out_shape=jax.ShapeDtypeStruct((n_rows, d), F32),
        compiler_params=_params("arbitrary"),
        name="moe_experts",
    )(blk_e, blk_first, blk_ord, blk_next, n_used, xs, w_gate, w_up, w_down)


def _combine_kernel(pos_ref, x_ref, rg_ref, gf_ref, y_ref, o_ref, ybuf, sems, *, tc, final_norm):
    i = pl.program_id(0)
    slot = i % 2

    def issue_tile(tile, dst_slot):
        base = tile * tc

        def issue(r, carry):
            for kk in range(TOP_K):
                pos = pos_ref[(base + r) * TOP_K + kk]
                pltpu.make_async_copy(y_ref.at[pl.ds(pos, 1)], ybuf.at[dst_slot, kk, pl.ds(r, 1)],
                                      sems.at[dst_slot]).start()
            return carry

        lax.fori_loop(0, tc, issue, 0, unroll=DMA_UNROLL)

    @pl.when(i == 0)
    def _():
        issue_tile(0, 0)

    @pl.when(i + 1 < pl.num_programs(0))
    def _():
        issue_tile(i + 1, 1 - slot)

    for kk in range(TOP_K):
        pltpu.make_async_copy(y_ref.at[pl.ds(0, tc)], ybuf.at[slot, kk], sems.at[slot]).wait()
    gates = rg_ref[...]
    out = x_ref[...] + (gates[:, 0:1] * ybuf[slot, 0] + gates[:, 1:2] * ybuf[slot, 1])
    if final_norm:
        out = _rms(out, gf_ref[...])
    o_ref[...] = out


def moe_combine(x, y, gates, pos, g_final, final_norm, *, tc=256):
    t, d = x.shape
    tc = min(tc, t)
    grid_spec = pltpu.PrefetchScalarGridSpec(
        num_scalar_prefetch=1, grid=(t // tc,),
        in_specs=[pl.BlockSpec((tc, d), lambda i, *_: (i, 0)),
                  pl.BlockSpec((tc, LANES), lambda i, *_: (i, 0)),
                  pl.BlockSpec((1, d), lambda i, *_: (0, 0)),
                  pl.BlockSpec(memory_space=pl.ANY)],
        out_specs=pl.BlockSpec((tc, d), lambda i, *_: (i, 0)),
        scratch_shapes=[pltpu.VMEM((2, TOP_K, tc, d), F32), pltpu.SemaphoreType.DMA((2,))])
    return pl.pallas_call(
        functools.partial(_combine_kernel, tc=tc, final_norm=final_norm),
        grid_spec=grid_spec,
        out_shape=jax.ShapeDtypeStruct((t, d), F32),
        compiler_params=_params("arbitrary"),
        name="moe_combine",
    )(pos, x, gates, g_final.reshape(1, d), y)


def moe_layer(x, g_norm, w_rg, w_re, w_gate, w_up, w_down, li, g_final, final_norm):
    t, d = x.shape
    h, ri, gates, cnt = moe_router(x, g_norm, w_rg, w_re)
    e_flat = ri[:, 0:TOP_K].reshape(-1)
    rank_flat = ri[:, TOP_K:2 * TOP_K].reshape(-1)
    counts = cnt[0, ROUTER_LANE0:ROUTER_LANE0 + N_EXPERTS]
    nblk = (counts + MOE_BLOCK - 1) // MOE_BLOCK
    blk_end = jnp.cumsum(nblk)
    pad_start = ((blk_end - nblk) * MOE_BLOCK).astype(I32)
    n_used = blk_end[-1:].astype(I32)
    nb = (t * TOP_K) // MOE_BLOCK + N_EXPERTS
    blk_ids = jnp.arange(nb, dtype=I32)
    blk = jnp.minimum(blk_ids, n_used[0] - 1)
    blk_e = jnp.searchsorted(blk_end, blk, side="right").astype(I32)
    is_last = (blk_ids + 1 == blk_end[blk_e]) | (blk_ids >= n_used[0])
    zero_flag = is_last.astype(I32)
    pos = (pad_start[e_flat] + rank_flat).astype(I32)
    prev_e = jnp.concatenate([jnp.full((1,), -1, I32), blk_e[:-1]])
    blk_first = ((blk_e != prev_e) & (blk_ids < n_used[0])).astype(I32)
    blk_ord = (jnp.cumsum(blk_first) - 1).astype(I32)
    e_ids = jnp.arange(N_EXPERTS, dtype=I32)
    nonempty_id = jnp.where(nblk > 0, e_ids, N_EXPERTS)
    next_ge = lax.cummin(nonempty_id, reverse=True)
    next_after = jnp.concatenate([next_ge[1:], jnp.full((1,), N_EXPERTS, I32)])
    blk_next = next_after[blk_e].astype(I32)
    xs = moe_dispatch(h, pos, zero_flag, nb * MOE_BLOCK)
    y = moe_experts(xs, blk_e, blk_first, blk_ord, blk_next, n_used, w_gate, w_up, w_down, li)
    return moe_combine(x, y, gates, pos, g_final, final_norm)


def kernel(x, rel_bias, norm_mix, norm_ffn, norm_final, w_in_ab, lam_ab, subln_ab, w_gk_ab, b_gk_ab,
           gnorm_ab, w_out_ab, w_in_c, ln_v_g, ln_v_b, w_spatial, b_spatial, w_out_c,
           w_router_group, w_router_expert, w_exp_gate, w_exp_up, w_exp_down):
    bsz, s_len, d = x.shape
    depth = norm_mix.shape[0]
    xf = x.reshape(bsz * s_len, d)
    for layer in range(depth):
        i = layer // 2
        if layer % 2 == 0:
            lam_init = 0.8 - 0.6 * math.exp(-0.3 * layer)
            n_main = 3 * A_WIDTH + 2 * B_HEADS * B_KEY_DIM + 2 * B_WIDTH
            proj, gb = norm_proj(xf, norm_mix[layer], w_in_ab, i, n_main,
                                 w_extra=w_in_ab[i, :, n_main:])
            o_a = diff_attention(proj, rel_bias, lam_ab[i], subln_ab[i], lam_init, bsz, s_len)
            o_b = gla_mixer(proj, gb, w_gk_ab[i], b_gk_ab[i], gnorm_ab[i], bsz, s_len)
            xf = out_proj(o_a, 0, o_b, 0, w_out_ab, i, xf)
        else:
            proj = norm_proj(xf, norm_mix[layer], w_in_c, i, w_in_c.shape[2], act="gelu")
            uz = spatial_gate(proj, ln_v_g[i], ln_v_b[i], w_spatial[i], b_spatial[i])
            xf = out_proj(uz, 0, uz, 1, w_out_c, i, xf)
        xf = moe_layer(xf, norm_ffn[layer], w_router_group[layer], w_router_expert[layer],
                       w_exp_gate, w_exp_up, w_exp_down, layer,
                       norm_final, layer == depth - 1)
    return xf.reshape(bsz, s_len, d)
```

```python
import functools
import math

import jax
import jax.numpy as jnp
import numpy as np
from jax import lax
from jax.experimental import pallas as pl
from jax.experimental.pallas import tpu as pltpu

F32 = jnp.float32
BF16 = jnp.bfloat16
I32 = jnp.int32
HIGHEST = lax.Precision.HIGHEST

EPS = 1e-6
NEG = -1e30
A_HEADS = 8
A_QK_DIM = 64
A_V_DIM = 128
A_WIDTH = A_HEADS * A_V_DIM
N_BUCKETS = 32
MAX_EXACT = 16
MAX_DISTANCE = 128
B_HEADS = 4
B_KEY_DIM = 128
B_VAL_DIM = 256
B_WIDTH = B_HEADS * B_VAL_DIM
GATE_RANK = 16
GATE_TEMP = 16.0
GLA_CHUNK = 64
C_GROUPS = 8
C_CHUNK = 128
N_GROUPS = 4
EXPERTS_PER_GROUP = 8
N_EXPERTS = N_GROUPS * EXPERTS_PER_GROUP
TOP_K = 2
LANES = 128
ROUTER_LANE0 = N_GROUPS
MOE_BLOCK = 256
Q_SPLIT = 2
VMEM_LIMIT = 56 * 1024 * 1024

NT_DIMS = (((1,), (1,)), ((), ()))
TN_DIMS = (((0,), (0,)), ((), ()))


def _t5_thresholds():
    n = np.arange(0, 2 * MAX_DISTANCE)
    nf = np.maximum(n, 1).astype(np.float64)
    large = MAX_EXACT + (np.log(nf / MAX_EXACT) / math.log(MAX_DISTANCE / MAX_EXACT)
                         * (N_BUCKETS - MAX_EXACT)).astype(np.int64)
    bucket = np.where(n < MAX_EXACT, n, np.minimum(large, N_BUCKETS - 1))
    return [int(np.min(n[bucket >= b])) for b in range(N_BUCKETS)]


T5_THRESHOLDS = _t5_thresholds()


def _params(*sem):
    return pltpu.CompilerParams(dimension_semantics=sem, vmem_limit_bytes=VMEM_LIMIT)


def _rms(x, g):
    ms = jnp.mean(x * x, axis=-1, keepdims=True)
    return x * lax.rsqrt(ms + EPS) * g


def _gelu_tanh(x):
    c = math.sqrt(2.0 / math.pi)
    return 0.5 * x * (1.0 + jnp.tanh(c * (x + 0.044715 * (x * x * x))))


def _sigmoid(x):
    return 1.0 / (1.0 + jnp.exp(-x))


def _norm_proj_kernel(*refs, act, has_extra):
    if has_extra:
        x_ref, g_ref, w_ref, we_ref, o_ref, oe_ref, h_scr = refs
    else:
        x_ref, g_ref, w_ref, o_ref, h_scr = refs

    @pl.when(pl.program_id(1) == 0)
    def _():
        hb = _rms(x_ref[...], g_ref[...]).astype(BF16)
        h_scr[...] = hb
        if has_extra:
            ye = lax.dot_general(hb, we_ref[...].astype(BF16), NT_DIMS, preferred_element_type=F32)
            oe_ref[...] = ye[:, :oe_ref.shape[1]]

    if has_extra:
        y = lax.dot_general(h_scr[...], w_ref[...].astype(BF16), NT_DIMS,
                            preferred_element_type=F32)
    else:
        y = jnp.dot(h_scr[...], w_ref[...].astype(BF16), preferred_element_type=F32)
    if act == "gelu":
        y = _gelu_tanh(y)
    o_ref[...] = y.astype(o_ref.dtype)


def norm_proj(x, g, w, li, n_cols, *, act=None, tm=1024, tn=512):
    t, d = x.shape
    tm = min(tm, t)
    n_extra = w.shape[2] - n_cols
    assert n_cols % tn == 0 and n_cols % LANES == 0 and 0 <= n_extra <= LANES
    grid = (t // tm, n_cols // tn)
    in_specs = [pl.BlockSpec((tm, d), lambda i, j: (i, 0)),
                pl.BlockSpec((1, d), lambda i, j: (0, 0))]
    out_specs = [pl.BlockSpec((tm, tn), lambda i, j: (i, j))]
    out_shape = [jax.ShapeDtypeStruct((t, n_cols), BF16)]
    args = [x, g.reshape(1, d)]
    if n_extra:
        wt = jnp.swapaxes(w, 1, 2)
        in_specs.append(pl.BlockSpec((None, tn, d), lambda i, j: (li, j, 0)))
        in_specs.append(pl.BlockSpec((None, LANES, d), lambda i, j: (li, n_cols // LANES, 0)))
        out_specs.append(pl.BlockSpec((tm, n_extra), lambda i, j: (i, 0)))
        out_shape.append(jax.ShapeDtypeStruct((t, n_extra), F32))
        args += [wt, wt]
    else:
        in_specs.append(pl.BlockSpec((None, d, tn), lambda i, j: (li, 0, j)))
        args.append(w)
    outs = pl.pallas_call(
        functools.partial(_norm_proj_kernel, act=act, has_extra=bool(n_extra)),
        grid=grid, in_specs=in_specs, out_specs=out_specs, out_shape=out_shape,
        scratch_shapes=[pltpu.VMEM((tm, d), BF16)],
        compiler_params=_params("parallel", "arbitrary"),
        name="norm_proj_" + (act or "lin"),
    )(*args)
    return outs if n_extra else outs[0]


def _out_proj_kernel(a0_ref, a1_ref, w_ref, r_ref, o_ref):
    kh = a0_ref.shape[1]
    w = w_ref[...].astype(BF16)
    y = jnp.dot(a0_ref[...], w[:kh], preferred_element_type=F32)
    y = y + jnp.dot(a1_ref[...], w[kh:], preferred_element_type=F32)
    o_ref[...] = r_ref[...] + y


def out_proj(a0, a0_blk, a1, a1_blk, w, li, res, *, tm=1024, tn=512):
    t, n = res.shape
    k = w.shape[1]
    kh = k // 2
    tm = min(tm, t)
    return pl.pallas_call(
        _out_proj_kernel,
        grid=(t // tm, n // tn),
        in_specs=[pl.BlockSpec((tm, kh), lambda i, j: (i, a0_blk)),
                  pl.BlockSpec((tm, kh), lambda i, j: (i, a1_blk)),
                  pl.BlockSpec((None, k, tn), lambda i, j: (li, 0, j)),
                  pl.BlockSpec((tm, tn), lambda i, j: (i, j))],
        out_specs=pl.BlockSpec((tm, tn), lambda i, j: (i, j)),
        out_shape=jax.ShapeDtypeStruct((t, n), F32),
        compiler_params=_params("parallel", "arbitrary"),
        name="out_proj",
    )(a0, a1, w, res)


def _diff_attn_kernel(tbl_ref, q_ref, k_ref, v_ref, lam_ref, sg_ref, o_ref,
                      m_scr, acc_scr, bias_scr, q_scr, v_scr, *, t, lam_init):
    h = pl.program_id(1)
    qi = pl.program_id(2)
    scale = A_QK_DIM ** -0.5

    @pl.when(qi == 0)
    def _build_bias():
        v_scr[:, :A_V_DIM] = v_ref[...]
        v_scr[:, A_V_DIM:] = jnp.ones((v_scr.shape[0], A_V_DIM), BF16)
        row = lax.broadcasted_iota(I32, (t, t), 0)
        col = lax.broadcasted_iota(I32, (t, t), 1)
        for d in range(2):
            rel = row - col + d * t
            acc = jnp.full((t, t), tbl_ref[h], F32)
            for b in range(1, N_BUCKETS):
                acc = jnp.where(rel >= T5_THRESHOLDS[b], tbl_ref[b * A_HEADS + h], acc)
            if d == 0:
                acc = jnp.where(rel >= 0, acc, NEG)
            bias_scr[d] = acc

    m_scr[...] = jnp.full(m_scr.shape, NEG, F32)
    acc_scr[...] = jnp.zeros(acc_scr.shape, F32)
    q = q_ref[...] * scale
    lane = lax.broadcasted_iota(I32, q.shape, 1)
    q_scr[0] = jnp.where(lane < A_QK_DIM, q, jnp.zeros_like(q))
    q_scr[1] = jnp.where(lane >= A_QK_DIM, q, jnp.zeros_like(q))

    def step(ki, bias_tile, bias_const):
        rows = pl.ds(pl.multiple_of(ki * t, t), t)
        k = k_ref[rows, :]
        v_aug = v_scr[rows, :]
        for m in range(2):
            for half in range(Q_SPLIT):
                hs = t // Q_SPLIT
                rs = slice(half * hs, (half + 1) * hs)
                s = lax.dot_general(q_scr[m, rs, :], k, NT_DIMS, preferred_element_type=F32)
                if bias_tile is not None:
                    s = s + bias_tile[rs, :]
                m_old = m_scr[m, rs, :]
                m_new = jnp.maximum(m_old, jnp.max(s, axis=-1, keepdims=True) + bias_const)
                alpha = jnp.exp(m_old - m_new)
                p = jnp.exp((s - (m_new - bias_const)).astype(BF16))
                acc_scr[m, rs, :] = alpha * acc_scr[m, rs, :] + jnp.dot(
                    p, v_aug, preferred_element_type=F32)
                m_scr[m, rs, :] = m_new

    far_bias = tbl_ref[(N_BUCKETS - 1) * A_HEADS + h]

    def far(ki, carry):
        step(ki, None, far_bias)
        return carry

    lax.fori_loop(0, jnp.maximum(qi - 1, 0), far, 0)

    @pl.when(qi >= 1)
    def _near():
        step(qi - 1, bias_scr.at[1], 0.0)

    step(qi, bias_scr.at[0], 0.0)
    lv = lam_ref[...]
    s1 = jnp.sum(lv[0:1] * lv[1:2], axis=-1, keepdims=True)
    s2 = jnp.sum(lv[2:3] * lv[3:4], axis=-1, keepdims=True)
    lam = jnp.exp(s1) - jnp.exp(s2) + lam_init
    a0 = acc_scr[0]
    a1 = acc_scr[1]
    o = (a0[:, :A_V_DIM] / a0[:, A_V_DIM:A_V_DIM + 1]
         - lam * (a1[:, :A_V_DIM] / a1[:, A_V_DIM:A_V_DIM + 1]))
    o = _rms(o, sg_ref[...]) * (1.0 - lam_init)
    o_ref[...] = o.astype(o_ref.dtype)


def diff_attention(proj, rel_bias, lam_vec, subln_g, lam_init, bsz, s_len, *, t=512):
    t = min(t, s_len)
    assert t >= MAX_DISTANCE and s_len % t == 0
    nq = s_len // t
    q_map = lambda b, h, qi: (b * nq + qi, h)
    kv_map = lambda off: (lambda b, h, qi: (b, off + h))
    const = lambda b, h, qi: (0, 0)
    return pl.pallas_call(
        functools.partial(_diff_attn_kernel, t=t, lam_init=lam_init),
        grid=(bsz, A_HEADS, nq),
        in_specs=[pl.BlockSpec(memory_space=pltpu.SMEM),
                  pl.BlockSpec((t, A_V_DIM), q_map),
                  pl.BlockSpec((s_len, A_V_DIM), kv_map(A_HEADS)),
                  pl.BlockSpec((s_len, A_V_DIM), kv_map(2 * A_HEADS)),
                  pl.BlockSpec((4, A_QK_DIM), const),
                  pl.BlockSpec((1, A_V_DIM), const)],
        out_specs=pl.BlockSpec((t, A_V_DIM), q_map),
        out_shape=jax.ShapeDtypeStruct((bsz * s_len, A_WIDTH), BF16),
        scratch_shapes=[pltpu.VMEM((2, t, 1), F32),
                        pltpu.VMEM((2, t, 2 * A_V_DIM), F32),
                        pltpu.VMEM((2, t, t), F32),
                        pltpu.VMEM((2, t, A_V_DIM), BF16),
                        pltpu.VMEM((s_len, 2 * A_V_DIM), BF16)],
        compiler_params=_params("parallel", "arbitrary", "arbitrary"),
        name="diff_attention",
    )(rel_bias.reshape(-1), proj, proj, proj, lam_vec, subln_g.reshape(1, A_V_DIM))


def _gla_kernel(q_ref, k_ref, v_ref, r_ref, gb_ref, wgk_ref, bgk_ref, gn_ref, o_ref, st_scr, *, rows):
    c_len = GLA_CHUNK
    scale = B_KEY_DIM ** -0.5

    @pl.when(pl.program_id(1) == 0)
    def _():
        st_scr[...] = jnp.zeros(st_scr.shape, F32)

    z = jnp.dot(gb_ref[...], wgk_ref[...], preferred_element_type=F32, precision=HIGHEST)
    z = z + bgk_ref[...]
    g_log = (jnp.minimum(z, 0.0) - jnp.log(1.0 + jnp.exp(-jnp.abs(z)))) * (1.0 / GATE_TEMP)
    row = lax.broadcasted_iota(I32, (rows, rows), 0)
    col = lax.broadcasted_iota(I32, (rows, rows), 1)
    ltri = ((row >= col) & ((row // c_len) == (col // c_len))).astype(F32)
    bcum = jnp.dot(ltri, g_log, preferred_element_type=F32, precision=HIGHEST)
    causal = (lax.broadcasted_iota(I32, (c_len, c_len), 0)
              >= lax.broadcasted_iota(I32, (c_len, c_len), 1))
    gn = gn_ref[...]

    for c in range(rows // c_len):
        rs = slice(c * c_len, (c + 1) * c_len)
        for hh in range(B_HEADS):
            ks = slice(hh * B_KEY_DIM, (hh + 1) * B_KEY_DIM)
            vs = slice(hh * B_VAL_DIM, (hh + 1) * B_VAL_DIM)
            b = bcum[rs, ks]
            b_last = b[c_len - 1:c_len, :]
            b_mid = b[c_len // 2:c_len // 2 + 1, :]
            q = q_ref[rs, ks].astype(F32) * scale
            k = k_ref[rs, ks].astype(F32)
            v = v_ref[rs, vs]
            st = st_scr[hh]
            q_in = (q * jnp.exp(b)).astype(BF16)
            q_ia = (q * jnp.exp(b - b_mid)).astype(BF16)
            k_ia = (k * jnp.exp(b_mid - b)).astype(BF16)
            att = lax.dot_general(q_ia, k_ia, NT_DIMS, preferred_element_type=F32)
            att = jnp.where(causal, att, 0.0)
            o = jnp.dot(att.astype(BF16), v, preferred_element_type=F32)
            o = o + lax.dot_general(q_in, st.astype(BF16), NT_DIMS, preferred_element_type=F32)
            k_st = (k * jnp.exp(b_last - b)).astype(BF16)
            st_scr[hh] = st * jnp.exp(b_last) + lax.dot_general(
                v, k_st, TN_DIMS, preferred_element_type=F32)
            on = _rms(o, gn)
            r = r_ref[rs, vs].astype(F32)
            o_ref[rs, vs] = (on * (r * _sigmoid(r))).astype(o_ref.dtype)


def gla_mixer(proj, gb, w_gk, b_gk, gnorm_g, bsz, s_len, *, rows=256):
    rows = min(rows, s_len)
    nr = s_len // rows
    kw = B_HEADS * B_KEY_DIM
    rmap = lambda blk: (lambda b, i: (b * nr + i, blk))
    return pl.pallas_call(
        functools.partial(_gla_kernel, rows=rows),
        grid=(bsz, nr),
        in_specs=[pl.BlockSpec((rows, kw), rmap(3 * A_WIDTH // kw)),
                  pl.BlockSpec((rows, kw), rmap(3 * A_WIDTH // kw + 1)),
                  pl.BlockSpec((rows, B_WIDTH), rmap((3 * A_WIDTH + 2 * kw) // B_WIDTH)),
                  pl.BlockSpec((rows, B_WIDTH), rmap((3 * A_WIDTH + 2 * kw) // B_WIDTH + 1)),
                  pl.BlockSpec((rows, GATE_RANK), rmap(0)),
                  pl.BlockSpec((GATE_RANK, kw), lambda b, i: (0, 0)),
                  pl.BlockSpec((1, kw), lambda b, i: (0, 0)),
                  pl.BlockSpec((1, B_VAL_DIM), lambda b, i: (0, 0))],
        out_specs=pl.BlockSpec((rows, B_WIDTH), rmap(0)),
        out_shape=jax.ShapeDtypeStruct((bsz * s_len, B_WIDTH), BF16),
        scratch_shapes=[pltpu.VMEM((B_HEADS, B_VAL_DIM, B_KEY_DIM), F32)],
        compiler_params=_params("parallel", "arbitrary"),
        name="gla",
    )(proj, proj, proj, proj, gb, w_gk, b_gk.reshape(1, kw), gnorm_g.reshape(1, B_VAL_DIM))


def _spatial_kernel(u_ref, v_ref, lg_ref, lb_ref, ws_ref, bs_ref, o_ref, *, rows):
    v = v_ref[...].astype(F32)
    mu = jnp.mean(v, axis=-1, keepdims=True)
    vc = v - mu
    var = jnp.mean(vc * vc, axis=-1, keepdims=True)
    vn = (vc * lax.rsqrt(var + EPS) * lg_ref[...] + lb_ref[...]).astype(BF16)
    gd = v.shape[1] // C_GROUPS
    mask = (lax.broadcasted_iota(I32, (C_CHUNK, C_CHUNK), 0)
            >= lax.broadcasted_iota(I32, (C_CHUNK, C_CHUNK), 1))
    for g in range(C_GROUPS):
        w = jnp.where(mask, ws_ref[g], 0.0).astype(BF16)
        bias = bs_ref[:, g:g + 1]
        cs = slice(g * gd, (g + 1) * gd)
        for c in range(rows // C_CHUNK):
            rs = slice(c * C_CHUNK, (c + 1) * C_CHUNK)
            z = jnp.dot(w, vn[rs, cs], preferred_element_type=F32) + bias
            o_ref[rs, cs] = (u_ref[rs, cs].astype(F32) * z).astype(o_ref.dtype)


def spatial_gate(proj, ln_g, ln_b, w_s, b_s, *, rows=512):
    t = proj.shape[0]
    cw = proj.shape[1] // 2
    rows = min(rows, t)
    return pl.pallas_call(
        functools.partial(_spatial_kernel, rows=rows),
        grid=(t // rows,),
        in_specs=[pl.BlockSpec((rows, cw), lambda i: (i, 0)),
                  pl.BlockSpec((rows, cw), lambda i: (i, 1)),
                  pl.BlockSpec((1, cw), lambda i: (0, 0)),
                  pl.BlockSpec((1, cw), lambda i: (0, 0)),
                  pl.BlockSpec((C_GROUPS, C_CHUNK, C_CHUNK), lambda i: (0, 0, 0)),
                  pl.BlockSpec((C_CHUNK, C_GROUPS), lambda i: (0, 0))],
        out_specs=pl.BlockSpec((rows, cw), lambda i: (i, 0)),
        out_shape=jax.ShapeDtypeStruct((t, cw), BF16),
        compiler_params=_params("parallel"),
        name="spatial_gate",
    )(proj, proj, ln_g.reshape(1, cw), ln_b.reshape(1, cw), w_s, b_s.T)


def _router_kernel(x_ref, g_ref, wr_ref, h_ref, ri_ref, rg_ref, cnt_ref, carry_scr, *, tr):
    @pl.when(pl.program_id(0) == 0)
    def _():
        carry_scr[...] = jnp.zeros(carry_scr.shape, F32)

    h = _rms(x_ref[...], g_ref[...])
    h_ref[...] = h
    logits = jnp.dot(h, wr_ref[...], preferred_element_type=F32, precision=HIGHEST)
    lane = lax.broadcasted_iota(I32, (tr, LANES), 1)
    lanef = lane.astype(F32)
    big = float(LANES)

    is_g = lane < N_GROUPS
    lg = jnp.where(is_g, logits, NEG)
    gmax = jnp.max(lg, axis=-1, keepdims=True)
    p_grp = 1.0 / jnp.sum(jnp.exp(lg - gmax), axis=-1, keepdims=True)
    gidx = jnp.min(jnp.where(is_g & (lg == gmax), lanef, big), axis=-1, keepdims=True)

    lo = ROUTER_LANE0 + EXPERTS_PER_GROUP * gidx
    in_e = (lanef >= lo) & (lanef < lo + EXPERTS_PER_GROUP)
    le = jnp.where(in_e, logits, NEG)
    emax = jnp.max(le, axis=-1, keepdims=True)
    i1 = jnp.min(jnp.where(in_e & (le == emax), lanef, big), axis=-1, keepdims=True)
    oh1 = lanef == i1
    le2 = jnp.where(oh1, NEG, le)
    emax2 = jnp.max(le2, axis=-1, keepdims=True)
    i2 = jnp.min(jnp.where(in_e & (le2 == emax2) & (~oh1), lanef, big), axis=-1, keepdims=True)
    oh2 = lanef == i2
    ratio = jnp.exp(emax2 - emax)
    g1 = p_grp / (1.0 + ratio)
    g2 = p_grp * ratio / (1.0 + ratio)

    onehot = (oh1 | oh2).astype(BF16)
    tri = (lax.broadcasted_iota(I32, (tr, tr), 0) > lax.broadcasted_iota(I32, (tr, tr), 1))
    before = jnp.dot(tri.astype(BF16), onehot, preferred_element_type=F32) + carry_scr[...]
    rank1 = jnp.sum(jnp.where(oh1, before, 0.0), axis=-1, keepdims=True)
    rank2 = jnp.sum(jnp.where(oh2, before, 0.0), axis=-1, keepdims=True)
    carry = carry_scr[...] + jnp.sum(onehot.astype(F32), axis=0, keepdims=True)
    carry_scr[...] = carry

    e1 = i1 - ROUTER_LANE0
    e2 = i2 - ROUTER_LANE0
    ri = jnp.where(lane == 0, e1, jnp.where(lane == 1, e2,
                                             jnp.where(lane == 2, rank1,
                                                       jnp.where(lane == 3, rank2, 0.0))))
    ri_ref[...] = ri.astype(I32)
    rg_ref[...] = jnp.where(lane == 0, g1, jnp.where(lane == 1, g2, 0.0))
    cnt_ref[...] = jnp.broadcast_to(carry, cnt_ref.shape).astype(I32)


def moe_router(x, g, w_rg, w_re, *, tr=512):
    t, d = x.shape
    tr = min(tr, t)
    w_r = jnp.zeros((d, LANES), F32)
    w_r = w_r.at[:, :N_GROUPS].set(w_rg).at[:, ROUTER_LANE0:ROUTER_LANE0 + N_EXPERTS].set(w_re)
    return pl.pallas_call(
        functools.partial(_router_kernel, tr=tr),
        grid=(t // tr,),
        in_specs=[pl.BlockSpec((tr, d), lambda i: (i, 0)),
                  pl.BlockSpec((1, d), lambda i: (0, 0)),
                  pl.BlockSpec((d, LANES), lambda i: (0, 0))],
        out_specs=[pl.BlockSpec((tr, d), lambda i: (i, 0)),
                   pl.BlockSpec((tr, LANES), lambda i: (i, 0)),
                   pl.BlockSpec((tr, LANES), lambda i: (i, 0)),
                   pl.BlockSpec((8, LANES), lambda i: (0, 0))],
        out_shape=[jax.ShapeDtypeStruct((t, d), F32),
                   jax.ShapeDtypeStruct((t, LANES), I32),
                   jax.ShapeDtypeStruct((t, LANES), F32),
                   jax.ShapeDtypeStruct((8, LANES), I32)],
        scratch_shapes=[pltpu.VMEM((1, LANES), F32)],
        compiler_params=_params("arbitrary"),
        name="moe_router",
    )(x, g.reshape(1, d), w_r)


DMA_UNROLL = 8


def _dispatch_kernel(pos_ref, zf_ref, h_ref, xs_ref, zero_scr, sem, zsem, *, td, nb):
    i = pl.program_id(0)
    base = i * td

    @pl.when(i == 0)
    def _():
        zero_scr[...] = jnp.zeros(zero_scr.shape, zero_scr.dtype)

        def zero_copy(j):
            return pltpu.make_async_copy(zero_scr, xs_ref.at[pl.ds(j * MOE_BLOCK, MOE_BLOCK)], zsem)

        def zstart(j, carry):
            @pl.when(zf_ref[j] == 1)
            def _():
                zero_copy(j).start()
            return carry

        def zwait(j, carry):
            @pl.when(zf_ref[j] == 1)
            def _():
                zero_copy(j).wait()
            return carry

        lax.fori_loop(0, nb, zstart, 0)
        lax.fori_loop(0, nb, zwait, 0)

    def issue(r, carry):
        for kk in range(TOP_K):
            pos = pos_ref[(base + r) * TOP_K + kk]
            pltpu.make_async_copy(h_ref.at[pl.ds(r, 1)], xs_ref.at[pl.ds(pos, 1)], sem).start()
        return carry

    lax.fori_loop(0, td, issue, 0, unroll=DMA_UNROLL)
    for kk in range(TOP_K):
        pltpu.make_async_copy(h_ref, xs_ref.at[pl.ds(0, td)], sem).wait()


def moe_dispatch(h, pos, zero_flag, n_rows, *, td=256):
    t, d = h.shape
    td = min(td, t)
    nb = n_rows // MOE_BLOCK
    grid_spec = pltpu.PrefetchScalarGridSpec(
        num_scalar_prefetch=2, grid=(t // td,),
        in_specs=[pl.BlockSpec((td, d), lambda i, *_: (i, 0))],
        out_specs=pl.BlockSpec(memory_space=pl.ANY),
        scratch_shapes=[pltpu.VMEM((MOE_BLOCK, d), F32),
                        pltpu.SemaphoreType.DMA(()), pltpu.SemaphoreType.DMA(())])
    return pl.pallas_call(
        functools.partial(_dispatch_kernel, td=td, nb=nb),
        grid_spec=grid_spec,
        out_shape=jax.ShapeDtypeStruct((n_rows, d), F32),
        compiler_params=_params("arbitrary"),
        name="moe_dispatch",
    )(pos, zero_flag, h)


def _expert_kernel(blk_e_ref, first_ref, ord_ref, next_ref, nused_ref, xs_ref, wg_hbm, wu_hbm, wd_hbm,
                   y_ref, wg_f, wu_f, wd_f, wg_s, wu_s, wd_s, sems, *, li):
    j = pl.program_id(0)
    used = j < nused_ref[0]

    def weight_copies(e, slot):
        return [pltpu.make_async_copy(src.at[li, e], dst.at[slot], sems.at[slot])
                for src, dst in ((wg_hbm, wg_f), (wu_hbm, wu_f), (wd_hbm, wd_f))]

    @pl.when(j == 0)
    def _():
        for c in weight_copies(blk_e_ref[0], 0):
            c.start()

    @pl.when(used & (first_ref[j] == 1))
    def _():
        slot = ord_ref[j] % 2
        for c in weight_copies(blk_e_ref[j], slot):
            c.wait()
        nxt = next_ref[j]

        @pl.when(nxt < N_EXPERTS)
        def _():
            for c in weight_copies(nxt, 1 - slot):
                c.start()

        wg_s[...] = wg_f[slot].astype(BF16)
        wu_s[...] = wu_f[slot].astype(BF16)
        wd_s[...] = wd_f[slot].astype(BF16)

    @pl.when(used)
    def _():
        x = xs_ref[...].astype(BF16)
        gate = jnp.dot(x, wg_s[...], preferred_element_type=F32)
        up = jnp.dot(x, wu_s[...], preferred_element_type=F32)
        act = (gate * _sigmoid(gate) * up).astype(BF16)
        y_ref[...] = jnp.dot(act, wd_s[...], preferred_element_type=F32)

    @pl.when(jnp.logical_not(used))
    def _():
        y_ref[...] = jnp.zeros(y_ref.shape, F32)


def moe_experts(xs, blk_e, blk_first, blk_ord, blk_next, n_used, w_gate, w_up, w_down, li):
    n_rows, d = xs.shape
    ff = w_gate.shape[3]
    nb = n_rows // MOE_BLOCK
    grid_spec = pltpu.PrefetchScalarGridSpec(
        num_scalar_prefetch=5, grid=(nb,),
        in_specs=[pl.BlockSpec((MOE_BLOCK, d), lambda j, be, bf, bo, bn, nu: (jnp.minimum(j, nu[0] - 1), 0)),
                  pl.BlockSpec(memory_space=pl.ANY),
                  pl.BlockSpec(memory_space=pl.ANY),
                  pl.BlockSpec(memory_space=pl.ANY)],
        out_specs=pl.BlockSpec((MOE_BLOCK, d), lambda j, *_: (j, 0)),
        scratch_shapes=[pltpu.VMEM((2, d, ff), F32), pltpu.VMEM((2, d, ff), F32),
                        pltpu.VMEM((2, ff, d), F32),
                        pltpu.VMEM((d, ff), BF16), pltpu.VMEM((d, ff), BF16),
                        pltpu.VMEM((ff, d), BF16),
                        pltpu.SemaphoreType.DMA((2,))])
    return pl.pallas_call(
        functools.partial(_expert_kernel, li=li),
        grid_spec=grid_spec,
        out_shape=jax.ShapeDtypeStruct((n_rows, d), F32),
        compiler_params=_params("arbitrary"),
        name="moe_experts",
    )(blk_e, blk_first, blk_ord, blk_next, n_used, xs, w_gate, w_up, w_down)


def _combine_kernel(pos_ref, x_ref, rg_ref, gf_ref, y_ref, o_ref, ybuf, sems, *, tc, final_norm):
    i = pl.program_id(0)
    slot = i % 2

    def issue_tile(tile, dst_slot):
        base = tile * tc

        def issue(r, carry):
            for kk in range(TOP_K):
                pos = pos_ref[(base + r) * TOP_K + kk]
                pltpu.make_async_copy(y_ref.at[pl.ds(pos, 1)], ybuf.at[dst_slot, kk, pl.ds(r, 1)],
                                      sems.at[dst_slot]).start()
            return carry

        lax.fori_loop(0, tc, issue, 0, unroll=DMA_UNROLL)

    @pl.when(i == 0)
    def _():
        issue_tile(0, 0)

    @pl.when(i + 1 < pl.num_programs(0))
    def _():
        issue_tile(i + 1, 1 - slot)

    for kk in range(TOP_K):
        pltpu.make_async_copy(y_ref.at[pl.ds(0, tc)], ybuf.at[slot, kk], sems.at[slot]).wait()
    gates = rg_ref[...]
    out = x_ref[...] + (gates[:, 0:1] * ybuf[slot, 0] + gates[:, 1:2] * ybuf[slot, 1])
    if final_norm:
        out = _rms(out, gf_ref[...])
    o_ref[...] = out


def moe_combine(x, y, gates, pos, g_final, final_norm, *, tc=256):
    t, d = x.shape
    tc = min(tc, t)
    grid_spec = pltpu.PrefetchScalarGridSpec(
        num_scalar_prefetch=1, grid=(t // tc,),
        in_specs=[pl.BlockSpec((tc, d), lambda i, *_: (i, 0)),
                  pl.BlockSpec((tc, LANES), lambda i, *_: (i, 0)),
                  pl.BlockSpec((1, d), lambda i, *_: (0, 0)),
                  pl.BlockSpec(memory_space=pl.ANY)],
        out_specs=pl.BlockSpec((tc, d), lambda i, *_: (i, 0)),
        scratch_shapes=[pltpu.VMEM((2, TOP_K, tc, d), F32), pltpu.SemaphoreType.DMA((2,))])
    return pl.pallas_call(
        functools.partial(_combine_kernel, tc=tc, final_norm=final_norm),
        grid_spec=grid_spec,
        out_shape=jax.ShapeDtypeStruct((t, d), F32),
        compiler_params=_params("arbitrary"),
        name="moe_combine",
    )(pos, x, gates, g_final.reshape(1, d), y)


def moe_layer(x, g_norm, w_rg, w_re, w_gate, w_up, w_down, li, g_final, final_norm):
    t, d = x.shape
    h, ri, gates, cnt = moe_router(x, g_norm, w_rg, w_re)
    e_flat = ri[:, 0:TOP_K].reshape(-1)
    rank_flat = ri[:, TOP_K:2 * TOP_K].reshape(-1)
    counts = cnt[0, ROUTER_LANE0:ROUTER_LANE0 + N_EXPERTS]
    nblk = (counts + MOE_BLOCK - 1) // MOE_BLOCK
    blk_end = jnp.cumsum(nblk)
    pad_start = ((blk_end - nblk) * MOE_BLOCK).astype(I32)
    n_used = blk_end[-1:].astype(I32)
    nb = (t * TOP_K) // MOE_BLOCK + N_EXPERTS
    blk_ids = jnp.arange(nb, dtype=I32)
    blk = jnp.minimum(blk_ids, n_used[0] - 1)
    blk_e = jnp.searchsorted(blk_end, blk, side="right").astype(I32)
    is_last = (blk_ids + 1 == blk_end[blk_e]) | (blk_ids >= n_used[0])
    zero_flag = is_last.astype(I32)
    pos = (pad_start[e_flat] + rank_flat).astype(I32)
    prev_e = jnp.concatenate([jnp.full((1,), -1, I32), blk_e[:-1]])
    blk_first = ((blk_e != prev_e) & (blk_ids < n_used[0])).astype(I32)
    blk_ord = (jnp.cumsum(blk_first) - 1).astype(I32)
    e_ids = jnp.arange(N_EXPERTS, dtype=I32)
    nonempty_id = jnp.where(nblk > 0, e_ids, N_EXPERTS)
    next_ge = lax.cummin(nonempty_id, reverse=True)
    next_after = jnp.concatenate([next_ge[1:], jnp.full((1,), N_EXPERTS, I32)])
    blk_next = next_after[blk_e].astype(I32)
    xs = moe_dispatch(h, pos, zero_flag, nb * MOE_BLOCK)
    y = moe_experts(xs, blk_e, blk_first, blk_ord, blk_next, n_used, w_gate, w_up, w_down, li)
    return moe_combine(x, y, gates, pos, g_final, final_norm)


def kernel(x, rel_bias, norm_mix, norm_ffn, norm_final, w_in_ab, lam_ab, subln_ab, w_gk_ab, b_gk_ab,
           gnorm_ab, w_out_ab, w_in_c, ln_v_g, ln_v_b, w_spatial, b_spatial, w_out_c,
           w_router_group, w_router_expert, w_exp_gate, w_exp_up, w_exp_down):
    bsz, s_len, d = x.shape
    depth = norm_mix.shape[0]
    xf = x.reshape(bsz * s_len, d)
    for layer in range(depth):
        i = layer // 2
        if layer % 2 == 0:
            lam_init = 0.8 - 0.6 * math.exp(-0.3 * layer)
            n_main = 3 * A_WIDTH + 2 * B_HEADS * B_KEY_DIM + 2 * B_WIDTH
            proj, gb = norm_proj(xf, norm_mix[layer], w_in_ab, i, n_main)
            o_a = diff_attention(proj, rel_bias, lam_ab[i], subln_ab[i], lam_init, bsz, s_len)
            o_b = gla_mixer(proj, gb, w_gk_ab[i], b_gk_ab[i], gnorm_ab[i], bsz, s_len)
            xf = out_proj(o_a, 0, o_b, 0, w_out_ab, i, xf)
        else:
            proj = norm_proj(xf, norm_mix[layer], w_in_c, i, w_in_c.shape[2], act="gelu")
            uz = spatial_gate(proj, ln_v_g[i], ln_v_b[i], w_spatial[i], b_spatial[i])
            xf = out_proj(uz, 0, uz, 1, w_out_c, i, xf)
        xf = moe_layer(xf, norm_ffn[layer], w_router_group[layer], w_router_expert[layer],
                       w_exp_gate, w_exp_up, w_exp_down, layer,
                       norm_final, layer == depth - 1)
    return xf.reshape(bsz, s_len, d)
```

```python
import functools
import math

import jax
import jax.numpy as jnp
import numpy as np
from jax import lax
from jax.experimental import pallas as pl
from jax.experimental.pallas import tpu as pltpu

F32 = jnp.float32
BF16 = jnp.bfloat16
I32 = jnp.int32
U32 = jnp.uint32
HIGHEST = lax.Precision.HIGHEST

EPS = 1e-6
NEG = -1e30
A_HEADS = 8
A_QK_DIM = 64
A_V_DIM = 128
A_WIDTH = A_HEADS * A_V_DIM
N_BUCKETS = 32
MAX_EXACT = 16
MAX_DISTANCE = 128
B_HEADS = 4
B_KEY_DIM = 128
B_VAL_DIM = 256
B_WIDTH = B_HEADS * B_VAL_DIM
GATE_RANK = 16
GATE_TEMP = 16.0
GLA_CHUNK = 64
C_GROUPS = 8
C_CHUNK = 128
N_GROUPS = 4
EXPERTS_PER_GROUP = 8
N_EXPERTS = N_GROUPS * EXPERTS_PER_GROUP
TOP_K = 2
LANES = 128
ROUTER_LANE0 = N_GROUPS
MOE_BLOCK = 256
Q_SPLIT = 2
VMEM_LIMIT = 56 * 1024 * 1024

NT_DIMS = (((1,), (1,)), ((), ()))
TN_DIMS = (((0,), (0,)), ((), ()))


def _t5_thresholds():
    n = np.arange(0, 2 * MAX_DISTANCE)
    nf = np.maximum(n, 1).astype(np.float64)
    large = MAX_EXACT + (np.log(nf / MAX_EXACT) / math.log(MAX_DISTANCE / MAX_EXACT)
                         * (N_BUCKETS - MAX_EXACT)).astype(np.int64)
    bucket = np.where(n < MAX_EXACT, n, np.minimum(large, N_BUCKETS - 1))
    return [int(np.min(n[bucket >= b])) for b in range(N_BUCKETS)]


T5_THRESHOLDS = _t5_thresholds()


def _params(*sem):
    return pltpu.CompilerParams(dimension_semantics=sem, vmem_limit_bytes=VMEM_LIMIT)


def _rms(x, g):
    ms = jnp.mean(x * x, axis=-1, keepdims=True)
    return x * lax.rsqrt(ms + EPS) * g


def _gelu_tanh(x):
    c = math.sqrt(2.0 / math.pi)
    return 0.5 * x * (1.0 + jnp.tanh(c * (x + 0.044715 * (x * x * x))))


def _sigmoid(x):
    return 1.0 / (1.0 + jnp.exp(-x))


def _pack_bf16_pairs(x):
    c = x.shape[1] // 2
    bits = lax.bitcast_convert_type(x.astype(BF16).astype(F32), U32)
    return (bits[:, :c] >> 16) | (bits[:, c:] & jnp.uint32(0xFFFF0000))


def _unpack_bf16_pairs(w):
    lo = lax.bitcast_convert_type(w << 16, F32)
    hi = lax.bitcast_convert_type(w & jnp.uint32(0xFFFF0000), F32)
    return lo, hi


def _norm_proj_kernel(*refs, act, has_extra):
    if has_extra:
        x_ref, g_ref, w_ref, we_ref, o_ref, oe_ref, h_scr = refs
    else:
        x_ref, g_ref, w_ref, o_ref, h_scr = refs

    @pl.when(pl.program_id(1) == 0)
    def _():
        hb = _rms(x_ref[...], g_ref[...]).astype(BF16)
        h_scr[...] = hb
        if has_extra:
            ye = lax.dot_general(hb, we_ref[...].astype(BF16), NT_DIMS, preferred_element_type=F32)
            oe_ref[...] = ye[:, :oe_ref.shape[1]]

    if has_extra:
        y = lax.dot_general(h_scr[...], w_ref[...].astype(BF16), NT_DIMS,
                            preferred_element_type=F32)
    else:
        y = jnp.dot(h_scr[...], w_ref[...].astype(BF16), preferred_element_type=F32)
    if act == "gelu":
        y = _gelu_tanh(y)
    o_ref[...] = y.astype(o_ref.dtype)


def norm_proj(x, g, w, li, n_cols, *, act=None, tm=1024, tn=512):
    t, d = x.shape
    tm = min(tm, t)
    n_extra = w.shape[2] - n_cols
    assert n_cols % tn == 0 and n_cols % LANES == 0 and 0 <= n_extra <= LANES
    grid = (t // tm, n_cols // tn)
    in_specs = [pl.BlockSpec((tm, d), lambda i, j: (i, 0)),
                pl.BlockSpec((1, d), lambda i, j: (0, 0))]
    out_specs = [pl.BlockSpec((tm, tn), lambda i, j: (i, j))]
    out_shape = [jax.ShapeDtypeStruct((t, n_cols), BF16)]
    args = [x, g.reshape(1, d)]
    if n_extra:
        wt = jnp.swapaxes(w, 1, 2)
        in_specs.append(pl.BlockSpec((None, tn, d), lambda i, j: (li, j, 0)))
        in_specs.append(pl.BlockSpec((None, LANES, d), lambda i, j: (li, n_cols // LANES, 0)))
        out_specs.append(pl.BlockSpec((tm, n_extra), lambda i, j: (i, 0)))
        out_shape.append(jax.ShapeDtypeStruct((t, n_extra), F32))
        args += [wt, wt]
    else:
        in_specs.append(pl.BlockSpec((None, d, tn), lambda i, j: (li, 0, j)))
        args.append(w)
    outs = pl.pallas_call(
        functools.partial(_norm_proj_kernel, act=act, has_extra=bool(n_extra)),
        grid=grid, in_specs=in_specs, out_specs=out_specs, out_shape=out_shape,
        scratch_shapes=[pltpu.VMEM((tm, d), BF16)],
        compiler_params=_params("parallel", "arbitrary"),
        name="norm_proj_" + (act or "lin"),
    )(*args)
    return outs if n_extra else outs[0]


def _out_proj_kernel(a0_ref, a1_ref, w_ref, r_ref, o_ref):
    kh = a0_ref.shape[1]
    w = w_ref[...].astype(BF16)
    y = jnp.dot(a0_ref[...], w[:kh], preferred_element_type=F32)
    y = y + jnp.dot(a1_ref[...], w[kh:], preferred_element_type=F32)
    o_ref[...] = r_ref[...] + y


def out_proj(a0, a0_blk, a1, a1_blk, w, li, res, *, tm=1024, tn=512):
    t, n = res.shape
    k = w.shape[1]
    kh = k // 2
    tm = min(tm, t)
    return pl.pallas_call(
        _out_proj_kernel,
        grid=(t // tm, n // tn),
        in_specs=[pl.BlockSpec((tm, kh), lambda i, j: (i, a0_blk)),
                  pl.BlockSpec((tm, kh), lambda i, j: (i, a1_blk)),
                  pl.BlockSpec((None, k, tn), lambda i, j: (li, 0, j)),
                  pl.BlockSpec((tm, tn), lambda i, j: (i, j))],
        out_specs=pl.BlockSpec((tm, tn), lambda i, j: (i, j)),
        out_shape=jax.ShapeDtypeStruct((t, n), F32),
        compiler_params=_params("parallel", "arbitrary"),
        name="out_proj",
    )(a0, a1, w, res)


def _diff_attn_kernel(tbl_ref, q_ref, k_ref, v_ref, lam_ref, sg_ref, o_ref,
                      m_scr, acc_scr, bias_scr, q_scr, v_scr, *, t, lam_init):
    h = pl.program_id(1)
    qi = pl.program_id(2)
    scale = A_QK_DIM ** -0.5

    @pl.when(qi == 0)
    def _build_bias():
        v_scr[:, :A_V_DIM] = v_ref[...]
        v_scr[:, A_V_DIM:] = jnp.ones((v_scr.shape[0], A_V_DIM), BF16)
        row = lax.broadcasted_iota(I32, (t, t), 0)
        col = lax.broadcasted_iota(I32, (t, t), 1)
        for d in range(2):
            rel = row - col + d * t
            acc = jnp.full((t, t), tbl_ref[h], F32)
            for b in range(1, N_BUCKETS):
                acc = jnp.where(rel >= T5_THRESHOLDS[b], tbl_ref[b * A_HEADS + h], acc)
            if d == 0:
                acc = jnp.where(rel >= 0, acc, NEG)
            bias_scr[d] = acc

    m_scr[...] = jnp.full(m_scr.shape, NEG, F32)
    acc_scr[...] = jnp.zeros(acc_scr.shape, F32)
    q = q_ref[...] * scale
    lane = lax.broadcasted_iota(I32, q.shape, 1)
    q_scr[0] = jnp.where(lane < A_QK_DIM, q, jnp.zeros_like(q))
    q_scr[1] = jnp.where(lane >= A_QK_DIM, q, jnp.zeros_like(q))

    def step(ki, bias_tile, bias_const):
        rows = pl.ds(pl.multiple_of(ki * t, t), t)
        k = k_ref[rows, :]
        v_aug = v_scr[rows, :]
        for m in range(2):
            for half in range(Q_SPLIT):
                hs = t // Q_SPLIT
                rs = slice(half * hs, (half + 1) * hs)
                s = lax.dot_general(q_scr[m, rs, :], k, NT_DIMS, preferred_element_type=F32)
                if bias_tile is not None:
                    s = s + bias_tile[rs, :]
                m_old = m_scr[m, rs, :]
                m_new = jnp.maximum(m_old, jnp.max(s, axis=-1, keepdims=True) + bias_const)
                alpha = jnp.exp(m_old - m_new)
                p = jnp.exp((s - (m_new - bias_const)).astype(BF16))
                acc_scr[m, rs, :] = alpha * acc_scr[m, rs, :] + jnp.dot(
                    p, v_aug, preferred_element_type=F32)
                m_scr[m, rs, :] = m_new

    far_bias = tbl_ref[(N_BUCKETS - 1) * A_HEADS + h]

    def far(ki, carry):
        step(ki, None, far_bias)
        return carry

    lax.fori_loop(0, jnp.maximum(qi - 1, 0), far, 0)

    @pl.when(qi >= 1)
    def _near():
        step(qi - 1, bias_scr.at[1], 0.0)

    step(qi, bias_scr.at[0], 0.0)
    lv = lam_ref[...]
    s1 = jnp.sum(lv[0:1] * lv[1:2], axis=-1, keepdims=True)
    s2 = jnp.sum(lv[2:3] * lv[3:4], axis=-1, keepdims=True)
    lam = jnp.exp(s1) - jnp.exp(s2) + lam_init
    a0 = acc_scr[0]
    a1 = acc_scr[1]
    o = (a0[:, :A_V_DIM] / a0[:, A_V_DIM:A_V_DIM + 1]
         - lam * (a1[:, :A_V_DIM] / a1[:, A_V_DIM:A_V_DIM + 1]))
    o = _rms(o, sg_ref[...]) * (1.0 - lam_init)
    o_ref[...] = o.astype(o_ref.dtype)


def diff_attention(proj, rel_bias, lam_vec, subln_g, lam_init, bsz, s_len, *, t=512):
    t = min(t, s_len)
    assert t >= MAX_DISTANCE and s_len % t == 0
    nq = s_len // t
    q_map = lambda b, h, qi: (b * nq + qi, h)
    kv_map = lambda off: (lambda b, h, qi: (b, off + h))
    const = lambda b, h, qi: (0, 0)
    return pl.pallas_call(
        functools.partial(_diff_attn_kernel, t=t, lam_init=lam_init),
        grid=(bsz, A_HEADS, nq),
        in_specs=[pl.BlockSpec(memory_space=pltpu.SMEM),
                  pl.BlockSpec((t, A_V_DIM), q_map),
                  pl.BlockSpec((s_len, A_V_DIM), kv_map(A_HEADS)),
                  pl.BlockSpec((s_len, A_V_DIM), kv_map(2 * A_HEADS)),
                  pl.BlockSpec((4, A_QK_DIM), const),
                  pl.BlockSpec((1, A_V_DIM), const)],
        out_specs=pl.BlockSpec((t, A_V_DIM), q_map),
        out_shape=jax.ShapeDtypeStruct((bsz * s_len, A_WIDTH), BF16),
        scratch_shapes=[pltpu.VMEM((2, t, 1), F32),
                        pltpu.VMEM((2, t, 2 * A_V_DIM), F32),
                        pltpu.VMEM((2, t, t), F32),
                        pltpu.VMEM((2, t, A_V_DIM), BF16),
                        pltpu.VMEM((s_len, 2 * A_V_DIM), BF16)],
        compiler_params=_params("parallel", "arbitrary", "arbitrary"),
        name="diff_attention",
    )(rel_bias.reshape(-1), proj, proj, proj, lam_vec, subln_g.reshape(1, A_V_DIM))


def _gla_kernel(q_ref, k_ref, v_ref, r_ref, gb_ref, wgk_ref, bgk_ref, gn_ref, o_ref, st_scr, *, rows):
    c_len = GLA_CHUNK
    scale = B_KEY_DIM ** -0.5

    @pl.when(pl.program_id(1) == 0)
    def _():
        st_scr[...] = jnp.zeros(st_scr.shape, F32)

    z = jnp.dot(gb_ref[...], wgk_ref[...], preferred_element_type=F32, precision=HIGHEST)
    z = z + bgk_ref[...]
    g_log = (jnp.minimum(z, 0.0) - jnp.log(1.0 + jnp.exp(-jnp.abs(z)))) * (1.0 / GATE_TEMP)
    row = lax.broadcasted_iota(I32, (rows, rows), 0)
    col = lax.broadcasted_iota(I32, (rows, rows), 1)
    ltri = ((row >= col) & ((row // c_len) == (col // c_len))).astype(F32)
    bcum = jnp.dot(ltri, g_log, preferred_element_type=F32, precision=HIGHEST)
    causal = (lax.broadcasted_iota(I32, (c_len, c_len), 0)
              >= lax.broadcasted_iota(I32, (c_len, c_len), 1))
    gn = gn_ref[...]

    for c in range(rows // c_len):
        rs = slice(c * c_len, (c + 1) * c_len)
        for hh in range(B_HEADS):
            ks = slice(hh * B_KEY_DIM, (hh + 1) * B_KEY_DIM)
            vs = slice(hh * B_VAL_DIM, (hh + 1) * B_VAL_DIM)
            b = bcum[rs, ks]
            b_last = b[c_len - 1:c_len, :]
            b_mid = b[c_len // 2:c_len // 2 + 1, :]
            q = q_ref[rs, ks].astype(F32) * scale
            k = k_ref[rs, ks].astype(F32)
            v = v_ref[rs, vs]
            st = st_scr[hh]
            q_in = (q * jnp.exp(b)).astype(BF16)
            q_ia = (q * jnp.exp(b - b_mid)).astype(BF16)
            k_ia = (k * jnp.exp(b_mid - b)).astype(BF16)
            att = lax.dot_general(q_ia, k_ia, NT_DIMS, preferred_element_type=F32)
            att = jnp.where(causal, att, 0.0)
            o = jnp.dot(att.astype(BF16), v, preferred_element_type=F32)
            o = o + lax.dot_general(q_in, st.astype(BF16), NT_DIMS, preferred_element_type=F32)
            k_st = (k * jnp.exp(b_last - b)).astype(BF16)
            st_scr[hh] = st * jnp.exp(b_last) + lax.dot_general(
                v, k_st, TN_DIMS, preferred_element_type=F32)
            on = _rms(o, gn)
            r = r_ref[rs, vs].astype(F32)
            o_ref[rs, vs] = (on * (r * _sigmoid(r))).astype(o_ref.dtype)


def gla_mixer(proj, gb, w_gk, b_gk, gnorm_g, bsz, s_len, *, rows=256):
    rows = min(rows, s_len)
    nr = s_len // rows
    kw = B_HEADS * B_KEY_DIM
    rmap = lambda blk: (lambda b, i: (b * nr + i, blk))
    return pl.pallas_call(
        functools.partial(_gla_kernel, rows=rows),
        grid=(bsz, nr),
        in_specs=[pl.BlockSpec((rows, kw), rmap(3 * A_WIDTH // kw)),
                  pl.BlockSpec((rows, kw), rmap(3 * A_WIDTH // kw + 1)),
                  pl.BlockSpec((rows, B_WIDTH), rmap((3 * A_WIDTH + 2 * kw) // B_WIDTH)),
                  pl.BlockSpec((rows, B_WIDTH), rmap((3 * A_WIDTH + 2 * kw) // B_WIDTH + 1)),
                  pl.BlockSpec((rows, GATE_RANK), rmap(0)),
                  pl.BlockSpec((GATE_RANK, kw), lambda b, i: (0, 0)),
                  pl.BlockSpec((1, kw), lambda b, i: (0, 0)),
                  pl.BlockSpec((1, B_VAL_DIM), lambda b, i: (0, 0))],
        out_specs=pl.BlockSpec((rows, B_WIDTH), rmap(0)),
        out_shape=jax.ShapeDtypeStruct((bsz * s_len, B_WIDTH), BF16),
        scratch_shapes=[pltpu.VMEM((B_HEADS, B_VAL_DIM, B_KEY_DIM), F32)],
        compiler_params=_params("parallel", "arbitrary"),
        name="gla",
    )(proj, proj, proj, proj, gb, w_gk, b_gk.reshape(1, kw), gnorm_g.reshape(1, B_VAL_DIM))


def _spatial_kernel(u_ref, v_ref, lg_ref, lb_ref, ws_ref, bs_ref, o_ref, *, rows):
    v = v_ref[...].astype(F32)
    mu = jnp.mean(v, axis=-1, keepdims=True)
    vc = v - mu
    var = jnp.mean(vc * vc, axis=-1, keepdims=True)
    vn = (vc * lax.rsqrt(var + EPS) * lg_ref[...] + lb_ref[...]).astype(BF16)
    gd = v.shape[1] // C_GROUPS
    mask = (lax.broadcasted_iota(I32, (C_CHUNK, C_CHUNK), 0)
            >= lax.broadcasted_iota(I32, (C_CHUNK, C_CHUNK), 1))
    for g in range(C_GROUPS):
        w = jnp.where(mask, ws_ref[g], 0.0).astype(BF16)
        bias = bs_ref[:, g:g + 1]
        cs = slice(g * gd, (g + 1) * gd)
        for c in range(rows // C_CHUNK):
            rs = slice(c * C_CHUNK, (c + 1) * C_CHUNK)
            z = jnp.dot(w, vn[rs, cs], preferred_element_type=F32) + bias
            o_ref[rs, cs] = (u_ref[rs, cs].astype(F32) * z).astype(o_ref.dtype)


def spatial_gate(proj, ln_g, ln_b, w_s, b_s, *, rows=512):
    t = proj.shape[0]
    cw = proj.shape[1] // 2
    rows = min(rows, t)
    return pl.pallas_call(
        functools.partial(_spatial_kernel, rows=rows),
        grid=(t // rows,),
        in_specs=[pl.BlockSpec((rows, cw), lambda i: (i, 0)),
                  pl.BlockSpec((rows, cw), lambda i: (i, 1)),
                  pl.BlockSpec((1, cw), lambda i: (0, 0)),
                  pl.BlockSpec((1, cw), lambda i: (0, 0)),
                  pl.BlockSpec((C_GROUPS, C_CHUNK, C_CHUNK), lambda i: (0, 0, 0)),
                  pl.BlockSpec((C_CHUNK, C_GROUPS), lambda i: (0, 0))],
        out_specs=pl.BlockSpec((rows, cw), lambda i: (i, 0)),
        out_shape=jax.ShapeDtypeStruct((t, cw), BF16),
        compiler_params=_params("parallel"),
        name="spatial_gate",
    )(proj, proj, ln_g.reshape(1, cw), ln_b.reshape(1, cw), w_s, b_s.T)


def _router_kernel(x_ref, g_ref, wr_ref, h_ref, ri_ref, rg_ref, cnt_ref, carry_scr, *, tr):
    @pl.when(pl.program_id(0) == 0)
    def _():
        carry_scr[...] = jnp.zeros(carry_scr.shape, F32)

    h = _rms(x_ref[...], g_ref[...])
    h_ref[...] = _pack_bf16_pairs(h)
    logits = jnp.dot(h, wr_ref[...], preferred_element_type=F32, precision=HIGHEST)
    lane = lax.broadcasted_iota(I32, (tr, LANES), 1)
    lanef = lane.astype(F32)
    big = float(LANES)

    is_g = lane < N_GROUPS
    lg = jnp.where(is_g, logits, NEG)
    gmax = jnp.max(lg, axis=-1, keepdims=True)
    p_grp = 1.0 / jnp.sum(jnp.exp(lg - gmax), axis=-1, keepdims=True)
    gidx = jnp.min(jnp.where(is_g & (lg == gmax), lanef, big), axis=-1, keepdims=True)

    lo = ROUTER_LANE0 + EXPERTS_PER_GROUP * gidx
    in_e = (lanef >= lo) & (lanef < lo + EXPERTS_PER_GROUP)
    le = jnp.where(in_e, logits, NEG)
    emax = jnp.max(le, axis=-1, keepdims=True)
    i1 = jnp.min(jnp.where(in_e & (le == emax), lanef, big), axis=-1, keepdims=True)
    oh1 = lanef == i1
    le2 = jnp.where(oh1, NEG, le)
    emax2 = jnp.max(le2, axis=-1, keepdims=True)
    i2 = jnp.min(jnp.where(in_e & (le2 == emax2) & (~oh1), lanef, big), axis=-1, keepdims=True)
    oh2 = lanef == i2
    ratio = jnp.exp(emax2 - emax)
    g1 = p_grp / (1.0 + ratio)
    g2 = p_grp * ratio / (1.0 + ratio)

    onehot = (oh1 | oh2).astype(BF16)
    tri = (lax.broadcasted_iota(I32, (tr, tr), 0) > lax.broadcasted_iota(I32, (tr, tr), 1))
    before = jnp.dot(tri.astype(BF16), onehot, preferred_element_type=F32) + carry_scr[...]
    rank1 = jnp.sum(jnp.where(oh1, before, 0.0), axis=-1, keepdims=True)
    rank2 = jnp.sum(jnp.where(oh2, before, 0.0), axis=-1, keepdims=True)
    carry = carry_scr[...] + jnp.sum(onehot.astype(F32), axis=0, keepdims=True)
    carry_scr[...] = carry

    e1 = i1 - ROUTER_LANE0
    e2 = i2 - ROUTER_LANE0
    ri = jnp.where(lane == 0, e1, jnp.where(lane == 1, e2,
                                             jnp.where(lane == 2, rank1,
                                                       jnp.where(lane == 3, rank2, 0.0))))
    ri_ref[...] = ri.astype(I32)
    rg_ref[...] = jnp.where(lane == 0, g1, jnp.where(lane == 1, g2, 0.0))
    cnt_ref[...] = jnp.broadcast_to(carry, cnt_ref.shape).astype(I32)


def moe_router(x, g, w_rg, w_re, *, tr=512):
    t, d = x.shape
    tr = min(tr, t)
    w_r = jnp.zeros((d, LANES), F32)
    w_r = w_r.at[:, :N_GROUPS].set(w_rg).at[:, ROUTER_LANE0:ROUTER_LANE0 + N_EXPERTS].set(w_re)
    return pl.pallas_call(
        functools.partial(_router_kernel, tr=tr),
        grid=(t // tr,),
        in_specs=[pl.BlockSpec((tr, d), lambda i: (i, 0)),
                  pl.BlockSpec((1, d), lambda i: (0, 0)),
                  pl.BlockSpec((d, LANES), lambda i: (0, 0))],
        out_specs=[pl.BlockSpec((tr, d // 2), lambda i: (i, 0)),
                   pl.BlockSpec((tr, LANES), lambda i: (i, 0)),
                   pl.BlockSpec((tr, LANES), lambda i: (i, 0)),
                   pl.BlockSpec((8, LANES), lambda i: (0, 0))],
        out_shape=[jax.ShapeDtypeStruct((t, d // 2), U32),
                   jax.ShapeDtypeStruct((t, LANES), I32),
                   jax.ShapeDtypeStruct((t, LANES), F32),
                   jax.ShapeDtypeStruct((8, LANES), I32)],
        scratch_shapes=[pltpu.VMEM((1, LANES), F32)],
        compiler_params=_params("arbitrary"),
        name="moe_router",
    )(x, g.reshape(1, d), w_r)


DMA_UNROLL = 8


def _dispatch_kernel(pos_ref, zf_ref, h_ref, xs_ref, zero_scr, sem, zsem, *, td, nb):
    i = pl.program_id(0)
    base = i * td

    @pl.when(i == 0)
    def _():
        zero_scr[...] = jnp.zeros(zero_scr.shape, zero_scr.dtype)

        def zero_copy(j):
            return pltpu.make_async_copy(zero_scr, xs_ref.at[pl.ds(j * MOE_BLOCK, MOE_BLOCK)], zsem)

        def zstart(j, carry):
            @pl.when(zf_ref[j] == 1)
            def _():
                zero_copy(j).start()
            return carry

        def zwait(j, carry):
            @pl.when(zf_ref[j] == 1)
            def _():
                zero_copy(j).wait()
            return carry

        lax.fori_loop(0, nb, zstart, 0)
        lax.fori_loop(0, nb, zwait, 0)

    def issue(r, carry):
        for kk in range(TOP_K):
            pos = pos_ref[(base + r) * TOP_K + kk]
            pltpu.make_async_copy(h_ref.at[pl.ds(r, 1)], xs_ref.at[pl.ds(pos, 1)], sem).start()
        return carry

    lax.fori_loop(0, td, issue, 0, unroll=DMA_UNROLL)
    for kk in range(TOP_K):
        pltpu.make_async_copy(h_ref, xs_ref.at[pl.ds(0, td)], sem).wait()


def moe_dispatch(h, pos, zero_flag, n_rows, *, td=256):
    t, d = h.shape
    td = min(td, t)
    nb = n_rows // MOE_BLOCK
    grid_spec = pltpu.PrefetchScalarGridSpec(
        num_scalar_prefetch=2, grid=(t // td,),
        in_specs=[pl.BlockSpec((td, d), lambda i, *_: (i, 0))],
        out_specs=pl.BlockSpec(memory_space=pl.ANY),
        scratch_shapes=[pltpu.VMEM((MOE_BLOCK, d), h.dtype),
                        pltpu.SemaphoreType.DMA(()), pltpu.SemaphoreType.DMA(())])
    return pl.pallas_call(
        functools.partial(_dispatch_kernel, td=td, nb=nb),
        grid_spec=grid_spec,
        out_shape=jax.ShapeDtypeStruct((n_rows, d), h.dtype),
        compiler_params=_params("arbitrary"),
        name="moe_dispatch",
    )(pos, zero_flag, h)


def _expert_kernel(blk_e_ref, first_ref, ord_ref, next_ref, nused_ref, xs_ref, wg_hbm, wu_hbm, wd_hbm,
                   y_ref, wg_f, wu_f, wd_f, wg_s, wu_s, wd_s, sems, *, li):
    j = pl.program_id(0)
    used = j < nused_ref[0]

    def weight_copies(e, slot):
        return [pltpu.make_async_copy(src.at[li, e], dst.at[slot], sems.at[slot])
                for src, dst in ((wg_hbm, wg_f), (wu_hbm, wu_f), (wd_hbm, wd_f))]

    @pl.when(j == 0)
    def _():
        for c in weight_copies(blk_e_ref[0], 0):
            c.start()

    @pl.when(used & (first_ref[j] == 1))
    def _():
        slot = ord_ref[j] % 2
        for c in weight_copies(blk_e_ref[j], slot):
            c.wait()
        nxt = next_ref[j]

        @pl.when(nxt < N_EXPERTS)
        def _():
            for c in weight_copies(nxt, 1 - slot):
                c.start()

        wg_s[...] = wg_f[slot].astype(BF16)
        wu_s[...] = wu_f[slot].astype(BF16)
        wd_s[...] = wd_f[slot].astype(BF16)

    @pl.when(used)
    def _():
        x_lo, x_hi = _unpack_bf16_pairs(xs_ref[...])
        x_lo = x_lo.astype(BF16)
        x_hi = x_hi.astype(BF16)
        dh = x_lo.shape[1]

        def proj(w_s):
            return (jnp.dot(x_lo, w_s[:dh, :], preferred_element_type=F32)
                    + jnp.dot(x_hi, w_s[dh:, :], preferred_element_type=F32))

        gate = proj(wg_s)
        up = proj(wu_s)
        act = (gate * _sigmoid(gate) * up).astype(BF16)
        y_ref[...] = _pack_bf16_pairs(jnp.dot(act, wd_s[...], preferred_element_type=F32))

    @pl.when(jnp.logical_not(used))
    def _():
        y_ref[...] = jnp.zeros(y_ref.shape, y_ref.dtype)


def moe_experts(xs, blk_e, blk_first, blk_ord, blk_next, n_used, w_gate, w_up, w_down, li):
    n_rows, dp = xs.shape
    d = w_gate.shape[2]
    ff = w_gate.shape[3]
    nb = n_rows // MOE_BLOCK
    grid_spec = pltpu.PrefetchScalarGridSpec(
        num_scalar_prefetch=5, grid=(nb,),
        in_specs=[pl.BlockSpec((MOE_BLOCK, dp), lambda j, be, bf, bo, bn, nu: (jnp.minimum(j, nu[0] - 1), 0)),
                  pl.BlockSpec(memory_space=pl.ANY),
                  pl.BlockSpec(memory_space=pl.ANY),
                  pl.BlockSpec(memory_space=pl.ANY)],
        out_specs=pl.BlockSpec((MOE_BLOCK, dp), lambda j, *_: (j, 0)),
        scratch_shapes=[pltpu.VMEM((2, d, ff), F32), pltpu.VMEM((2, d, ff), F32),
                        pltpu.VMEM((2, ff, d), F32),
                        pltpu.VMEM((d, ff), BF16), pltpu.VMEM((d, ff), BF16),
                        pltpu.VMEM((ff, d), BF16),
                        pltpu.SemaphoreType.DMA((2,))])
    return pl.pallas_call(
        functools.partial(_expert_kernel, li=li),
        grid_spec=grid_spec,
        out_shape=jax.ShapeDtypeStruct((n_rows, dp), U32),
        compiler_params=_params("arbitrary"),
        name="moe_experts",
    )(blk_e, blk_first, blk_ord, blk_next, n_used, xs, w_gate, w_up, w_down)


def _combine_kernel(pos_ref, x_ref, rg_ref, gf_ref, y_ref, o_ref, ybuf, sems, *, tc, final_norm):
    i = pl.program_id(0)
    slot = i % 2

    def issue_tile(tile, dst_slot):
        base = tile * tc

        def issue(r, carry):
            for kk in range(TOP_K):
                pos = pos_ref[(base + r) * TOP_K + kk]
                pltpu.make_async_copy(y_ref.at[pl.ds(pos, 1)], ybuf.at[dst_slot, kk, pl.ds(r, 1)],
                                      sems.at[dst_slot]).start()
            return carry

        lax.fori_loop(0, tc, issue, 0, unroll=DMA_UNROLL)

    @pl.when(i == 0)
    def _():
        issue_tile(0, 0)

    @pl.when(i + 1 < pl.num_programs(0))
    def _():
        issue_tile(i + 1, 1 - slot)

    for kk in range(TOP_K):
        pltpu.make_async_copy(y_ref.at[pl.ds(0, tc)], ybuf.at[slot, kk], sems.at[slot]).wait()
    gates = rg_ref[...]
    g0 = gates[:, 0:1]
    g1 = gates[:, 1:2]
    y0_lo, y0_hi = _unpack_bf16_pairs(ybuf[slot, 0])
    y1_lo, y1_hi = _unpack_bf16_pairs(ybuf[slot, 1])
    dh = y0_lo.shape[1]
    out_lo = x_ref[:, :dh] + (g0 * y0_lo + g1 * y1_lo)
    out_hi = x_ref[:, dh:] + (g0 * y0_hi + g1 * y1_hi)
    if final_norm:
        ms = (jnp.sum(out_lo * out_lo, axis=-1, keepdims=True)
              + jnp.sum(out_hi * out_hi, axis=-1, keepdims=True)) * (1.0 / (2 * dh))
        inv = lax.rsqrt(ms + EPS)
        out_lo = out_lo * inv * gf_ref[:, :dh]
        out_hi = out_hi * inv * gf_ref[:, dh:]
    o_ref[:, :dh] = out_lo
    o_ref[:, dh:] = out_hi


def moe_combine(x, y, gates, pos, g_final, final_norm, *, tc=256):
    t, d = x.shape
    tc = min(tc, t)
    grid_spec = pltpu.PrefetchScalarGridSpec(
        num_scalar_prefetch=1, grid=(t // tc,),
        in_specs=[pl.BlockSpec((tc, d), lambda i, *_: (i, 0)),
                  pl.BlockSpec((tc, LANES), lambda i, *_: (i, 0)),
                  pl.BlockSpec((1, d), lambda i, *_: (0, 0)),
                  pl.BlockSpec(memory_space=pl.ANY)],
        out_specs=pl.BlockSpec((tc, d), lambda i, *_: (i, 0)),
        scratch_shapes=[pltpu.VMEM((2, TOP_K, tc, d // 2), U32), pltpu.SemaphoreType.DMA((2,))])
    return pl.pallas_call(
        functools.partial(_combine_kernel, tc=tc, final_norm=final_norm),
        grid_spec=grid_spec,
        out_shape=jax.ShapeDtypeStruct((t, d), F32),
        compiler_params=_params("arbitrary"),
        name="moe_combine",
    )(pos, x, gates, g_final.reshape(1, d), y)


def moe_layer(x, g_norm, w_rg, w_re, w_gate, w_up, w_down, li, g_final, final_norm):
    t, d = x.shape
    h, ri, gates, cnt = moe_router(x, g_norm, w_rg, w_re)
    e_flat = ri[:, 0:TOP_K].reshape(-1)
    rank_flat = ri[:, TOP_K:2 * TOP_K].reshape(-1)
    counts = cnt[0, ROUTER_LANE0:ROUTER_LANE0 + N_EXPERTS]
    nblk = (counts + MOE_BLOCK - 1) // MOE_BLOCK
    blk_end = jnp.cumsum(nblk)
    pad_start = ((blk_end - nblk) * MOE_BLOCK).astype(I32)
    n_used = blk_end[-1:].astype(I32)
    nb = (t * TOP_K) // MOE_BLOCK + N_EXPERTS
    blk_ids = jnp.arange(nb, dtype=I32)
    blk = jnp.minimum(blk_ids, n_used[0] - 1)
    blk_e = jnp.searchsorted(blk_end, blk, side="right").astype(I32)
    is_last = (blk_ids + 1 == blk_end[blk_e]) | (blk_ids >= n_used[0])
    zero_flag = is_last.astype(I32)
    pos = (pad_start[e_flat] + rank_flat).astype(I32)
    prev_e = jnp.concatenate([jnp.full((1,), -1, I32), blk_e[:-1]])
    blk_first = ((blk_e != prev_e) & (blk_ids < n_used[0])).astype(I32)
    blk_ord = (jnp.cumsum(blk_first) - 1).astype(I32)
    e_ids = jnp.arange(N_EXPERTS, dtype=I32)
    nonempty_id = jnp.where(nblk > 0, e_ids, N_EXPERTS)
    next_ge = lax.cummin(nonempty_id, reverse=True)
    next_after = jnp.concatenate([next_ge[1:], jnp.full((1,), N_EXPERTS, I32)])
    blk_next = next_after[blk_e].astype(I32)
    xs = moe_dispatch(h, pos, zero_flag, nb * MOE_BLOCK)
    y = moe_experts(xs, blk_e, blk_first, blk_ord, blk_next, n_used, w_gate, w_up, w_down, li)
    return moe_combine(x, y, gates, pos, g_final, final_norm)


def kernel(x, rel_bias, norm_mix, norm_ffn, norm_final, w_in_ab, lam_ab, subln_ab, w_gk_ab, b_gk_ab,
           gnorm_ab, w_out_ab, w_in_c, ln_v_g, ln_v_b, w_spatial, b_spatial, w_out_c,
           w_router_group, w_router_expert, w_exp_gate, w_exp_up, w_exp_down):
    bsz, s_len, d = x.shape
    depth = norm_mix.shape[0]
    xf = x.reshape(bsz * s_len, d)
    for layer in range(depth):
        i = layer // 2
        if layer % 2 == 0:
            lam_init = 0.8 - 0.6 * math.exp(-0.3 * layer)
            n_main = 3 * A_WIDTH + 2 * B_HEADS * B_KEY_DIM + 2 * B_WIDTH
            proj, gb = norm_proj(xf, norm_mix[layer], w_in_ab, i, n_main)
            o_a = diff_attention(proj, rel_bias, lam_ab[i], subln_ab[i], lam_init, bsz, s_len)
            o_b = gla_mixer(proj, gb, w_gk_ab[i], b_gk_ab[i], gnorm_ab[i], bsz, s_len)
            xf = out_proj(o_a, 0, o_b, 0, w_out_ab, i, xf)
        else:
            proj = norm_proj(xf, norm_mix[layer], w_in_c, i, w_in_c.shape[2], act="gelu")
            uz = spatial_gate(proj, ln_v_g[i], ln_v_b[i], w_spatial[i], b_spatial[i])
            xf = out_proj(uz, 0, uz, 1, w_out_c, i, xf)
        xf = moe_layer(xf, norm_ffn[layer], w_router_group[layer], w_router_expert[layer],
                       w_exp_gate, w_exp_up, w_exp_down, layer,
                       norm_final, layer == depth - 1)
    return xf.reshape(bsz, s_len, d)
```

```python
import functools
import math

import jax
import jax.numpy as jnp
import numpy as np
from jax import lax
from jax.experimental import pallas as pl
from jax.experimental.pallas import tpu as pltpu

F32 = jnp.float32
BF16 = jnp.bfloat16
I32 = jnp.int32
U32 = jnp.uint32
HIGHEST = lax.Precision.HIGHEST

EPS = 1e-6
NEG = -1e30
A_HEADS = 8
A_QK_DIM = 64
A_V_DIM = 128
A_WIDTH = A_HEADS * A_V_DIM
N_BUCKETS = 32
MAX_EXACT = 16
MAX_DISTANCE = 128
B_HEADS = 4
B_KEY_DIM = 128
B_VAL_DIM = 256
B_WIDTH = B_HEADS * B_VAL_DIM
GATE_RANK = 16
GATE_TEMP = 16.0
GLA_CHUNK = 64
C_GROUPS = 8
C_CHUNK = 128
N_GROUPS = 4
EXPERTS_PER_GROUP = 8
N_EXPERTS = N_GROUPS * EXPERTS_PER_GROUP
TOP_K = 2
LANES = 128
ROUTER_LANE0 = N_GROUPS
MOE_BLOCK = 256
Q_SPLIT = 2
VMEM_LIMIT = 56 * 1024 * 1024

NT_DIMS = (((1,), (1,)), ((), ()))
TN_DIMS = (((0,), (0,)), ((), ()))


def _t5_thresholds():
    n = np.arange(0, 2 * MAX_DISTANCE)
    nf = np.maximum(n, 1).astype(np.float64)
    large = MAX_EXACT + (np.log(nf / MAX_EXACT) / math.log(MAX_DISTANCE / MAX_EXACT)
                         * (N_BUCKETS - MAX_EXACT)).astype(np.int64)
    bucket = np.where(n < MAX_EXACT, n, np.minimum(large, N_BUCKETS - 1))
    return [int(np.min(n[bucket >= b])) for b in range(N_BUCKETS)]


T5_THRESHOLDS = _t5_thresholds()


def _params(*sem):
    return pltpu.CompilerParams(dimension_semantics=sem, vmem_limit_bytes=VMEM_LIMIT)


def _rms(x, g):
    ms = jnp.mean(x * x, axis=-1, keepdims=True)
    return x * lax.rsqrt(ms + EPS) * g


def _gelu_tanh(x):
    c = math.sqrt(2.0 / math.pi)
    return 0.5 * x * (1.0 + jnp.tanh(c * (x + 0.044715 * (x * x * x))))


def _sigmoid(x):
    return 1.0 / (1.0 + jnp.exp(-x))


def _pack_bf16_pairs(x):
    c = x.shape[1] // 2
    bits = lax.bitcast_convert_type(x.astype(BF16).astype(F32), U32)
    return (bits[:, :c] >> 16) | (bits[:, c:] & jnp.uint32(0xFFFF0000))


def _unpack_bf16_pairs(w):
    lo = lax.bitcast_convert_type(w << 16, F32)
    hi = lax.bitcast_convert_type(w & jnp.uint32(0xFFFF0000), F32)
    return lo, hi


def _norm_proj_kernel(*refs, act, has_extra):
    if has_extra:
        x_ref, g_ref, w_ref, we_ref, o_ref, oe_ref, h_scr = refs
    else:
        x_ref, g_ref, w_ref, o_ref, h_scr = refs

    @pl.when(pl.program_id(1) == 0)
    def _():
        hb = _rms(x_ref[...], g_ref[...]).astype(BF16)
        h_scr[...] = hb
        if has_extra:
            ye = lax.dot_general(hb, we_ref[...].astype(BF16), NT_DIMS, preferred_element_type=F32)
            oe_ref[...] = ye[:, :oe_ref.shape[1]]

    if has_extra:
        y = lax.dot_general(h_scr[...], w_ref[...].astype(BF16), NT_DIMS,
                            preferred_element_type=F32)
    else:
        y = jnp.dot(h_scr[...], w_ref[...].astype(BF16), preferred_element_type=F32)
    if act == "gelu":
        y = _gelu_tanh(y)
    o_ref[...] = y.astype(o_ref.dtype)


def norm_proj(x, g, w, li, n_cols, *, act=None, tm=1024, tn=512):
    t, d = x.shape
    tm = min(tm, t)
    n_extra = w.shape[2] - n_cols
    assert n_cols % tn == 0 and n_cols % LANES == 0 and 0 <= n_extra <= LANES
    grid = (t // tm, n_cols // tn)
    in_specs = [pl.BlockSpec((tm, d), lambda i, j: (i, 0)),
                pl.BlockSpec((1, d), lambda i, j: (0, 0))]
    out_specs = [pl.BlockSpec((tm, tn), lambda i, j: (i, j))]
    out_shape = [jax.ShapeDtypeStruct((t, n_cols), BF16)]
    args = [x, g.reshape(1, d)]
    if n_extra:
        wt = jnp.swapaxes(w, 1, 2)
        in_specs.append(pl.BlockSpec((None, tn, d), lambda i, j: (li, j, 0)))
        in_specs.append(pl.BlockSpec((None, LANES, d), lambda i, j: (li, n_cols // LANES, 0)))
        out_specs.append(pl.BlockSpec((tm, n_extra), lambda i, j: (i, 0)))
        out_shape.append(jax.ShapeDtypeStruct((t, n_extra), F32))
        args += [wt, wt]
    else:
        in_specs.append(pl.BlockSpec((None, d, tn), lambda i, j: (li, 0, j)))
        args.append(w)
    outs = pl.pallas_call(
        functools.partial(_norm_proj_kernel, act=act, has_extra=bool(n_extra)),
        grid=grid, in_specs=in_specs, out_specs=out_specs, out_shape=out_shape,
        scratch_shapes=[pltpu.VMEM((tm, d), BF16)],
        compiler_params=_params("parallel", "arbitrary"),
        name="norm_proj_" + (act or "lin"),
    )(*args)
    return outs if n_extra else outs[0]


def _out_proj_kernel(a0_ref, a1_ref, w_ref, r_ref, o_ref):
    kh = a0_ref.shape[1]
    w = w_ref[...].astype(BF16)
    y = jnp.dot(a0_ref[...], w[:kh], preferred_element_type=F32)
    y = y + jnp.dot(a1_ref[...], w[kh:], preferred_element_type=F32)
    o_ref[...] = r_ref[...] + y


def out_proj(a0, a0_blk, a1, a1_blk, w, li, res, *, tm=1024, tn=512):
    t, n = res.shape
    k = w.shape[1]
    kh = k // 2
    tm = min(tm, t)
    return pl.pallas_call(
        _out_proj_kernel,
        grid=(t // tm, n // tn),
        in_specs=[pl.BlockSpec((tm, kh), lambda i, j: (i, a0_blk)),
                  pl.BlockSpec((tm, kh), lambda i, j: (i, a1_blk)),
                  pl.BlockSpec((None, k, tn), lambda i, j: (li, 0, j)),
                  pl.BlockSpec((tm, tn), lambda i, j: (i, j))],
        out_specs=pl.BlockSpec((tm, tn), lambda i, j: (i, j)),
        out_shape=jax.ShapeDtypeStruct((t, n), F32),
        compiler_params=_params("parallel", "arbitrary"),
        name="out_proj",
    )(a0, a1, w, res)


def _diff_attn_kernel(tbl_ref, q_ref, k_ref, v_ref, lam_ref, sg_ref, o_ref,
                      m_scr, acc_scr, bias_scr, q_scr, v_scr, *, t, lam_init):
    h = pl.program_id(1)
    qi = pl.program_id(2)
    scale = A_QK_DIM ** -0.5

    @pl.when(qi == 0)
    def _build_bias():
        v_scr[:, :A_V_DIM] = v_ref[...]
        v_scr[:, A_V_DIM:] = jnp.ones((v_scr.shape[0], A_V_DIM), BF16)
        row = lax.broadcasted_iota(I32, (t, t), 0)
        col = lax.broadcasted_iota(I32, (t, t), 1)
        for d in range(2):
            rel = row - col + d * t
            acc = jnp.full((t, t), tbl_ref[h], F32)
            for b in range(1, N_BUCKETS):
                acc = jnp.where(rel >= T5_THRESHOLDS[b], tbl_ref[b * A_HEADS + h], acc)
            if d == 0:
                acc = jnp.where(rel >= 0, acc, NEG)
            bias_scr[d] = acc

    m_scr[...] = jnp.full(m_scr.shape, NEG, F32)
    acc_scr[...] = jnp.zeros(acc_scr.shape, F32)
    q = q_ref[...] * scale
    lane = lax.broadcasted_iota(I32, q.shape, 1)
    q_scr[0] = jnp.where(lane < A_QK_DIM, q, jnp.zeros_like(q))
    q_scr[1] = jnp.where(lane >= A_QK_DIM, q, jnp.zeros_like(q))

    def step(ki, bias_tile, bias_const):
        rows = pl.ds(pl.multiple_of(ki * t, t), t)
        k = k_ref[rows, :]
        v_aug = v_scr[rows, :]
        for m in range(2):
            for half in range(Q_SPLIT):
                hs = t // Q_SPLIT
                rs = slice(half * hs, (half + 1) * hs)
                s = lax.dot_general(q_scr[m, rs, :], k, NT_DIMS, preferred_element_type=F32)
                if bias_tile is not None:
                    s = s + bias_tile[rs, :]
                m_old = m_scr[m, rs, :]
                m_new = jnp.maximum(m_old, jnp.max(s, axis=-1, keepdims=True) + bias_const)
                alpha = jnp.exp(m_old - m_new)
                p = jnp.exp((s - (m_new - bias_const)).astype(BF16))
                acc_scr[m, rs, :] = alpha * acc_scr[m, rs, :] + jnp.dot(
                    p, v_aug, preferred_element_type=F32)
                m_scr[m, rs, :] = m_new

    far_bias = tbl_ref[(N_BUCKETS - 1) * A_HEADS + h]

    def far(ki, carry):
        step(ki, None, far_bias)
        return carry

    lax.fori_loop(0, jnp.maximum(qi - 1, 0), far, 0)

    @pl.when(qi >= 1)
    def _near():
        step(qi - 1, bias_scr.at[1], 0.0)

    step(qi, bias_scr.at[0], 0.0)
    lv = lam_ref[...]
    s1 = jnp.sum(lv[0:1] * lv[1:2], axis=-1, keepdims=True)
    s2 = jnp.sum(lv[2:3] * lv[3:4], axis=-1, keepdims=True)
    lam = jnp.exp(s1) - jnp.exp(s2) + lam_init
    a0 = acc_scr[0]
    a1 = acc_scr[1]
    o = (a0[:, :A_V_DIM] / a0[:, A_V_DIM:A_V_DIM + 1]
         - lam * (a1[:, :A_V_DIM] / a1[:, A_V_DIM:A_V_DIM + 1]))
    o = _rms(o, sg_ref[...]) * (1.0 - lam_init)
    o_ref[...] = o.astype(o_ref.dtype)


def diff_attention(proj, rel_bias, lam_vec, subln_g, lam_init, bsz, s_len, *, t=512):
    t = min(t, s_len)
    assert t >= MAX_DISTANCE and s_len % t == 0
    nq = s_len // t
    q_map = lambda b, h, qi: (b * nq + qi, h)
    kv_map = lambda off: (lambda b, h, qi: (b, off + h))
    const = lambda b, h, qi: (0, 0)
    return pl.pallas_call(
        functools.partial(_diff_attn_kernel, t=t, lam_init=lam_init),
        grid=(bsz, A_HEADS, nq),
        in_specs=[pl.BlockSpec(memory_space=pltpu.SMEM),
                  pl.BlockSpec((t, A_V_DIM), q_map),
                  pl.BlockSpec((s_len, A_V_DIM), kv_map(A_HEADS)),
                  pl.BlockSpec((s_len, A_V_DIM), kv_map(2 * A_HEADS)),
                  pl.BlockSpec((4, A_QK_DIM), const),
                  pl.BlockSpec((1, A_V_DIM), const)],
        out_specs=pl.BlockSpec((t, A_V_DIM), q_map),
        out_shape=jax.ShapeDtypeStruct((bsz * s_len, A_WIDTH), BF16),
        scratch_shapes=[pltpu.VMEM((2, t, 1), F32),
                        pltpu.VMEM((2, t, 2 * A_V_DIM), F32),
                        pltpu.VMEM((2, t, t), F32),
                        pltpu.VMEM((2, t, A_V_DIM), BF16),
                        pltpu.VMEM((s_len, 2 * A_V_DIM), BF16)],
        compiler_params=_params("parallel", "arbitrary", "arbitrary"),
        name="diff_attention",
    )(rel_bias.reshape(-1), proj, proj, proj, lam_vec, subln_g.reshape(1, A_V_DIM))


def _gla_kernel(q_ref, k_ref, v_ref, r_ref, gb_ref, wgk_ref, bgk_ref, gn_ref, o_ref, st_scr, *, rows):
    c_len = GLA_CHUNK
    scale = B_KEY_DIM ** -0.5

    @pl.when(pl.program_id(1) == 0)
    def _():
        st_scr[...] = jnp.zeros(st_scr.shape, F32)

    z = jnp.dot(gb_ref[...], wgk_ref[...], preferred_element_type=F32, precision=HIGHEST)
    z = z + bgk_ref[...]
    g_log = (jnp.minimum(z, 0.0) - jnp.log(1.0 + jnp.exp(-jnp.abs(z)))) * (1.0 / GATE_TEMP)
    row = lax.broadcasted_iota(I32, (rows, rows), 0)
    col = lax.broadcasted_iota(I32, (rows, rows), 1)
    ltri = ((row >= col) & ((row // c_len) == (col // c_len))).astype(F32)
    bcum = jnp.dot(ltri, g_log, preferred_element_type=F32, precision=HIGHEST)
    causal = (lax.broadcasted_iota(I32, (c_len, c_len), 0)
              >= lax.broadcasted_iota(I32, (c_len, c_len), 1))
    gn = gn_ref[...]

    for c in range(rows // c_len):
        rs = slice(c * c_len, (c + 1) * c_len)
        for hh in range(B_HEADS):
            ks = slice(hh * B_KEY_DIM, (hh + 1) * B_KEY_DIM)
            vs = slice(hh * B_VAL_DIM, (hh + 1) * B_VAL_DIM)
            b = bcum[rs, ks]
            b_last = b[c_len - 1:c_len, :]
            b_mid = b[c_len // 2:c_len // 2 + 1, :]
            q = q_ref[rs, ks].astype(F32) * scale
            k = k_ref[rs, ks].astype(F32)
            v = v_ref[rs, vs]
            st = st_scr[hh]
            q_in = (q * jnp.exp(b)).astype(BF16)
            q_ia = (q * jnp.exp(b - b_mid)).astype(BF16)
            k_ia = (k * jnp.exp(b_mid - b)).astype(BF16)
            att = lax.dot_general(q_ia, k_ia, NT_DIMS, preferred_element_type=F32)
            att = jnp.where(causal, att, 0.0)
            o = jnp.dot(att.astype(BF16), v, preferred_element_type=F32)
            o = o + lax.dot_general(q_in, st.astype(BF16), NT_DIMS, preferred_element_type=F32)
            k_st = (k * jnp.exp(b_last - b)).astype(BF16)
            st_scr[hh] = st * jnp.exp(b_last) + lax.dot_general(
                v, k_st, TN_DIMS, preferred_element_type=F32)
            on = _rms(o, gn)
            r = r_ref[rs, vs].astype(F32)
            o_ref[rs, vs] = (on * (r * _sigmoid(r))).astype(o_ref.dtype)


def gla_mixer(proj, gb, w_gk, b_gk, gnorm_g, bsz, s_len, *, rows=256):
    rows = min(rows, s_len)
    nr = s_len // rows
    kw = B_HEADS * B_KEY_DIM
    rmap = lambda blk: (lambda b, i: (b * nr + i, blk))
    return pl.pallas_call(
        functools.partial(_gla_kernel, rows=rows),
        grid=(bsz, nr),
        in_specs=[pl.BlockSpec((rows, kw), rmap(3 * A_WIDTH // kw)),
                  pl.BlockSpec((rows, kw), rmap(3 * A_WIDTH // kw + 1)),
                  pl.BlockSpec((rows, B_WIDTH), rmap((3 * A_WIDTH + 2 * kw) // B_WIDTH)),
                  pl.BlockSpec((rows, B_WIDTH), rmap((3 * A_WIDTH + 2 * kw) // B_WIDTH + 1)),
                  pl.BlockSpec((rows, GATE_RANK), rmap(0)),
                  pl.BlockSpec((GATE_RANK, kw), lambda b, i: (0, 0)),
                  pl.BlockSpec((1, kw), lambda b, i: (0, 0)),
                  pl.BlockSpec((1, B_VAL_DIM), lambda b, i: (0, 0))],
        out_specs=pl.BlockSpec((rows, B_WIDTH), rmap(0)),
        out_shape=jax.ShapeDtypeStruct((bsz * s_len, B_WIDTH), BF16),
        scratch_shapes=[pltpu.VMEM((B_HEADS, B_VAL_DIM, B_KEY_DIM), F32)],
        compiler_params=_params("parallel", "arbitrary"),
        name="gla",
    )(proj, proj, proj, proj, gb, w_gk, b_gk.reshape(1, kw), gnorm_g.reshape(1, B_VAL_DIM))


def _spatial_kernel(u_ref, v_ref, lg_ref, lb_ref, ws_ref, bs_ref, o_ref, *, rows):
    v = v_ref[...].astype(F32)
    mu = jnp.mean(v, axis=-1, keepdims=True)
    vc = v - mu
    var = jnp.mean(vc * vc, axis=-1, keepdims=True)
    vn = (vc * lax.rsqrt(var + EPS) * lg_ref[...] + lb_ref[...]).astype(BF16)
    gd = v.shape[1] // C_GROUPS
    mask = (lax.broadcasted_iota(I32, (C_CHUNK, C_CHUNK), 0)
            >= lax.broadcasted_iota(I32, (C_CHUNK, C_CHUNK), 1))
    for g in range(C_GROUPS):
        w = jnp.where(mask, ws_ref[g], 0.0).astype(BF16)
        bias = bs_ref[:, g:g + 1]
        cs = slice(g * gd, (g + 1) * gd)
        for c in range(rows // C_CHUNK):
            rs = slice(c * C_CHUNK, (c + 1) * C_CHUNK)
            z = jnp.dot(w, vn[rs, cs], preferred_element_type=F32) + bias
            o_ref[rs, cs] = (u_ref[rs, cs].astype(F32) * z).astype(o_ref.dtype)


def spatial_gate(proj, ln_g, ln_b, w_s, b_s, *, rows=512):
    t = proj.shape[0]
    cw = proj.shape[1] // 2
    rows = min(rows, t)
    return pl.pallas_call(
        functools.partial(_spatial_kernel, rows=rows),
        grid=(t // rows,),
        in_specs=[pl.BlockSpec((rows, cw), lambda i: (i, 0)),
                  pl.BlockSpec((rows, cw), lambda i: (i, 1)),
                  pl.BlockSpec((1, cw), lambda i: (0, 0)),
                  pl.BlockSpec((1, cw), lambda i: (0, 0)),
                  pl.BlockSpec((C_GROUPS, C_CHUNK, C_CHUNK), lambda i: (0, 0, 0)),
                  pl.BlockSpec((C_CHUNK, C_GROUPS), lambda i: (0, 0))],
        out_specs=pl.BlockSpec((rows, cw), lambda i: (i, 0)),
        out_shape=jax.ShapeDtypeStruct((t, cw), BF16),
        compiler_params=_params("parallel"),
        name="spatial_gate",
    )(proj, proj, ln_g.reshape(1, cw), ln_b.reshape(1, cw), w_s, b_s.T)


def _router_kernel(x_ref, g_ref, wr_ref, h_ref, ri_ref, rg_ref, cnt_ref, carry_scr, *, tr):
    @pl.when(pl.program_id(0) == 0)
    def _():
        carry_scr[...] = jnp.zeros(carry_scr.shape, F32)

    h = _rms(x_ref[...], g_ref[...])
    h_ref[...] = _pack_bf16_pairs(h)
    h_hi = h.astype(BF16)
    h_lo = (h - h_hi.astype(F32)).astype(BF16)
    w_hi = wr_ref[0]
    w_lo = wr_ref[1]
    logits = (jnp.dot(h_hi, w_hi, preferred_element_type=F32)
              + (jnp.dot(h_hi, w_lo, preferred_element_type=F32)
                 + jnp.dot(h_lo, w_hi, preferred_element_type=F32)))
    lane = lax.broadcasted_iota(I32, (tr, LANES), 1)
    lanef = lane.astype(F32)
    big = float(LANES)

    is_g = lane < N_GROUPS
    lg = jnp.where(is_g, logits, NEG)
    gmax = jnp.max(lg, axis=-1, keepdims=True)
    p_grp = 1.0 / jnp.sum(jnp.exp(lg - gmax), axis=-1, keepdims=True)
    gidx = jnp.min(jnp.where(is_g & (lg == gmax), lanef, big), axis=-1, keepdims=True)

    lo = ROUTER_LANE0 + EXPERTS_PER_GROUP * gidx
    in_e = (lanef >= lo) & (lanef < lo + EXPERTS_PER_GROUP)
    le = jnp.where(in_e, logits, NEG)
    emax = jnp.max(le, axis=-1, keepdims=True)
    i1 = jnp.min(jnp.where(in_e & (le == emax), lanef, big), axis=-1, keepdims=True)
    oh1 = lanef == i1
    le2 = jnp.where(oh1, NEG, le)
    emax2 = jnp.max(le2, axis=-1, keepdims=True)
    i2 = jnp.min(jnp.where(in_e & (le2 == emax2) & (~oh1), lanef, big), axis=-1, keepdims=True)
    oh2 = lanef == i2
    ratio = jnp.exp(emax2 - emax)
    g1 = p_grp / (1.0 + ratio)
    g2 = p_grp * ratio / (1.0 + ratio)

    onehot = (oh1 | oh2).astype(BF16)
    tri = (lax.broadcasted_iota(I32, (tr, tr), 0) > lax.broadcasted_iota(I32, (tr, tr), 1))
    before = jnp.dot(tri.astype(BF16), onehot, preferred_element_type=F32) + carry_scr[...]
    rank1 = jnp.sum(jnp.where(oh1, before, 0.0), axis=-1, keepdims=True)
    rank2 = jnp.sum(jnp.where(oh2, before, 0.0), axis=-1, keepdims=True)
    carry = carry_scr[...] + jnp.sum(onehot.astype(F32), axis=0, keepdims=True)
    carry_scr[...] = carry

    e1 = i1 - ROUTER_LANE0
    e2 = i2 - ROUTER_LANE0
    ri = jnp.where(lane == 0, e1, jnp.where(lane == 1, e2,
                                             jnp.where(lane == 2, rank1,
                                                       jnp.where(lane == 3, rank2, 0.0))))
    ri_ref[...] = ri.astype(I32)
    rg_ref[...] = jnp.where(lane == 0, g1, jnp.where(lane == 1, g2, 0.0))
    cnt_ref[...] = jnp.broadcast_to(carry, cnt_ref.shape).astype(I32)


def moe_router(x, g, w_rg, w_re, *, tr=512):
    t, d = x.shape
    tr = min(tr, t)
    w_r = jnp.zeros((d, LANES), F32)
    w_r = w_r.at[:, :N_GROUPS].set(w_rg).at[:, ROUTER_LANE0:ROUTER_LANE0 + N_EXPERTS].set(w_re)
    w_hi = w_r.astype(BF16)
    w_r = jnp.stack([w_hi, (w_r - w_hi.astype(F32)).astype(BF16)])
    return pl.pallas_call(
        functools.partial(_router_kernel, tr=tr),
        grid=(t // tr,),
        in_specs=[pl.BlockSpec((tr, d), lambda i: (i, 0)),
                  pl.BlockSpec((1, d), lambda i: (0, 0)),
                  pl.BlockSpec((2, d, LANES), lambda i: (0, 0, 0))],
        out_specs=[pl.BlockSpec((tr, d // 2), lambda i: (i, 0)),
                   pl.BlockSpec((tr, LANES), lambda i: (i, 0)),
                   pl.BlockSpec((tr, LANES), lambda i: (i, 0)),
                   pl.BlockSpec((8, LANES), lambda i: (0, 0))],
        out_shape=[jax.ShapeDtypeStruct((t, d // 2), U32),
                   jax.ShapeDtypeStruct((t, LANES), I32),
                   jax.ShapeDtypeStruct((t, LANES), F32),
                   jax.ShapeDtypeStruct((8, LANES), I32)],
        scratch_shapes=[pltpu.VMEM((1, LANES), F32)],
        compiler_params=_params("arbitrary"),
        name="moe_router",
    )(x, g.reshape(1, d), w_r)


DMA_UNROLL = 8


def _dispatch_kernel(pos_ref, zf_ref, h_ref, xs_ref, zero_scr, sem, zsem, *, td, nb):
    i = pl.program_id(0)
    base = i * td

    @pl.when(i == 0)
    def _():
        zero_scr[...] = jnp.zeros(zero_scr.shape, zero_scr.dtype)

        def zero_copy(j):
            return pltpu.make_async_copy(zero_scr, xs_ref.at[pl.ds(j * MOE_BLOCK, MOE_BLOCK)], zsem)

        def zstart(j, carry):
            @pl.when(zf_ref[j] == 1)
            def _():
                zero_copy(j).start()
            return carry

        def zwait(j, carry):
            @pl.when(zf_ref[j] == 1)
            def _():
                zero_copy(j).wait()
            return carry

        lax.fori_loop(0, nb, zstart, 0)
        lax.fori_loop(0, nb, zwait, 0)

    def issue(r, carry):
        for kk in range(TOP_K):
            pos = pos_ref[(base + r) * TOP_K + kk]
            pltpu.make_async_copy(h_ref.at[pl.ds(r, 1)], xs_ref.at[pl.ds(pos, 1)], sem).start()
        return carry

    lax.fori_loop(0, td, issue, 0, unroll=DMA_UNROLL)
    for kk in range(TOP_K):
        pltpu.make_async_copy(h_ref, xs_ref.at[pl.ds(0, td)], sem).wait()


def moe_dispatch(h, pos, zero_flag, n_rows, *, td=256):
    t, d = h.shape
    td = min(td, t)
    nb = n_rows // MOE_BLOCK
    grid_spec = pltpu.PrefetchScalarGridSpec(
        num_scalar_prefetch=2, grid=(t // td,),
        in_specs=[pl.BlockSpec((td, d), lambda i, *_: (i, 0))],
        out_specs=pl.BlockSpec(memory_space=pl.ANY),
        scratch_shapes=[pltpu.VMEM((MOE_BLOCK, d), h.dtype),
                        pltpu.SemaphoreType.DMA(()), pltpu.SemaphoreType.DMA(())])
    return pl.pallas_call(
        functools.partial(_dispatch_kernel, td=td, nb=nb),
        grid_spec=grid_spec,
        out_shape=jax.ShapeDtypeStruct((n_rows, d), h.dtype),
        compiler_params=_params("arbitrary"),
        name="moe_dispatch",
    )(pos, zero_flag, h)


def _expert_kernel(blk_e_ref, first_ref, ord_ref, next_ref, nused_ref, xs_ref, wg_hbm, wu_hbm, wd_hbm,
                   y_ref, wg_f, wu_f, wd_f, wg_s, wu_s, wd_s, sems, *, li):
    j = pl.program_id(0)
    used = j < nused_ref[0]

    def weight_copies(e, slot):
        return [pltpu.make_async_copy(src.at[li, e], dst.at[slot], sems.at[slot])
                for src, dst in ((wg_hbm, wg_f), (wu_hbm, wu_f), (wd_hbm, wd_f))]

    @pl.when(j == 0)
    def _():
        for c in weight_copies(blk_e_ref[0], 0):
            c.start()

    @pl.when(used & (first_ref[j] == 1))
    def _():
        slot = ord_ref[j] % 2
        for c in weight_copies(blk_e_ref[j], slot):
            c.wait()
        nxt = next_ref[j]

        @pl.when(nxt < N_EXPERTS)
        def _():
            for c in weight_copies(nxt, 1 - slot):
                c.start()

        wg_s[...] = wg_f[slot].astype(BF16)
        wu_s[...] = wu_f[slot].astype(BF16)
        wd_s[...] = wd_f[slot].astype(BF16)

    @pl.when(used)
    def _():
        x_lo, x_hi = _unpack_bf16_pairs(xs_ref[...])
        x_lo = x_lo.astype(BF16)
        x_hi = x_hi.astype(BF16)
        dh = x_lo.shape[1]

        def proj(w_s):
            return (jnp.dot(x_lo, w_s[:dh, :], preferred_element_type=F32)
                    + jnp.dot(x_hi, w_s[dh:, :], preferred_element_type=F32))

        gate = proj(wg_s)
        up = proj(wu_s)
        act = (gate * _sigmoid(gate) * up).astype(BF16)
        y_ref[...] = _pack_bf16_pairs(jnp.dot(act, wd_s[...], preferred_element_type=F32))

    @pl.when(jnp.logical_not(used))
    def _():
        y_ref[...] = jnp.zeros(y_ref.shape, y_ref.dtype)


def moe_experts(xs, blk_e, blk_first, blk_ord, blk_next, n_used, w_gate, w_up, w_down, li):
    n_rows, dp = xs.shape
    d = w_gate.shape[2]
    ff = w_gate.shape[3]
    nb = n_rows // MOE_BLOCK
    grid_spec = pltpu.PrefetchScalarGridSpec(
        num_scalar_prefetch=5, grid=(nb,),
        in_specs=[pl.BlockSpec((MOE_BLOCK, dp), lambda j, be, bf, bo, bn, nu: (jnp.minimum(j, nu[0] - 1), 0)),
                  pl.BlockSpec(memory_space=pl.ANY),
                  pl.BlockSpec(memory_space=pl.ANY),
                  pl.BlockSpec(memory_space=pl.ANY)],
        out_specs=pl.BlockSpec((MOE_BLOCK, dp), lambda j, *_: (j, 0)),
        scratch_shapes=[pltpu.VMEM((2, d, ff), F32), pltpu.VMEM((2, d, ff), F32),
                        pltpu.VMEM((2, ff, d), F32),
                        pltpu.VMEM((d, ff), BF16), pltpu.VMEM((d, ff), BF16),
                        pltpu.VMEM((ff, d), BF16),
                        pltpu.SemaphoreType.DMA((2,))])
    return pl.pallas_call(
        functools.partial(_expert_kernel, li=li),
        grid_spec=grid_spec,
        out_shape=jax.ShapeDtypeStruct((n_rows, dp), U32),
        compiler_params=_params("arbitrary"),
        name="moe_experts",
    )(blk_e, blk_first, blk_ord, blk_next, n_used, xs, w_gate, w_up, w_down)


def _combine_kernel(pos_ref, x_ref, rg_ref, gf_ref, y_ref, o_ref, ybuf, sems, *, tc, final_norm):
    i = pl.program_id(0)
    slot = i % 2

    def issue_tile(tile, dst_slot):
        base = tile * tc

        def issue(r, carry):
            for kk in range(TOP_K):
                pos = pos_ref[(base + r) * TOP_K + kk]
                pltpu.make_async_copy(y_ref.at[pl.ds(pos, 1)], ybuf.at[dst_slot, kk, pl.ds(r, 1)],
                                      sems.at[dst_slot]).start()
            return carry

        lax.fori_loop(0, tc, issue, 0, unroll=DMA_UNROLL)

    @pl.when(i == 0)
    def _():
        issue_tile(0, 0)

    @pl.when(i + 1 < pl.num_programs(0))
    def _():
        issue_tile(i + 1, 1 - slot)

    for kk in range(TOP_K):
        pltpu.make_async_copy(y_ref.at[pl.ds(0, tc)], ybuf.at[slot, kk], sems.at[slot]).wait()
    gates = rg_ref[...]
    g0 = gates[:, 0:1]
    g1 = gates[:, 1:2]
    y0_lo, y0_hi = _unpack_bf16_pairs(ybuf[slot, 0])
    y1_lo, y1_hi = _unpack_bf16_pairs(ybuf[slot, 1])
    dh = y0_lo.shape[1]
    out_lo = x_ref[:, :dh] + (g0 * y0_lo + g1 * y1_lo)
    out_hi = x_ref[:, dh:] + (g0 * y0_hi + g1 * y1_hi)
    if final_norm:
        ms = (jnp.sum(out_lo * out_lo, axis=-1, keepdims=True)
              + jnp.sum(out_hi * out_hi, axis=-1, keepdims=True)) * (1.0 / (2 * dh))
        inv = lax.rsqrt(ms + EPS)
        out_lo = out_lo * inv * gf_ref[:, :dh]
        out_hi = out_hi * inv * gf_ref[:, dh:]
    o_ref[:, :dh] = out_lo
    o_ref[:, dh:] = out_hi


def moe_combine(x, y, gates, pos, g_final, final_norm, *, tc=256):
    t, d = x.shape
    tc = min(tc, t)
    grid_spec = pltpu.PrefetchScalarGridSpec(
        num_scalar_prefetch=1, grid=(t // tc,),
        in_specs=[pl.BlockSpec((tc, d), lambda i, *_: (i, 0)),
                  pl.BlockSpec((tc, LANES), lambda i, *_: (i, 0)),
                  pl.BlockSpec((1, d), lambda i, *_: (0, 0)),
                  pl.BlockSpec(memory_space=pl.ANY)],
        out_specs=pl.BlockSpec((tc, d), lambda i, *_: (i, 0)),
        scratch_shapes=[pltpu.VMEM((2, TOP_K, tc, d // 2), U32), pltpu.SemaphoreType.DMA((2,))])
    return pl.pallas_call(
        functools.partial(_combine_kernel, tc=tc, final_norm=final_norm),
        grid_spec=grid_spec,
        out_shape=jax.ShapeDtypeStruct((t, d), F32),
        compiler_params=_params("arbitrary"),
        name="moe_combine",
    )(pos, x, gates, g_final.reshape(1, d), y)


def moe_layer(x, g_norm, w_rg, w_re, w_gate, w_up, w_down, li, g_final, final_norm):
    t, d = x.shape
    h, ri, gates, cnt = moe_router(x, g_norm, w_rg, w_re)
    e_flat = ri[:, 0:TOP_K].reshape(-1)
    rank_flat = ri[:, TOP_K:2 * TOP_K].reshape(-1)
    counts = cnt[0, ROUTER_LANE0:ROUTER_LANE0 + N_EXPERTS]
    nblk = (counts + MOE_BLOCK - 1) // MOE_BLOCK
    blk_end = jnp.cumsum(nblk)
    pad_start = ((blk_end - nblk) * MOE_BLOCK).astype(I32)
    n_used = blk_end[-1:].astype(I32)
    nb = (t * TOP_K) // MOE_BLOCK + N_EXPERTS
    blk_ids = jnp.arange(nb, dtype=I32)
    blk = jnp.minimum(blk_ids, n_used[0] - 1)
    blk_e = jnp.sum(blk_end[None, :] <= blk[:, None], axis=1).astype(I32)
    is_last = (blk_ids + 1 == blk_end[blk_e]) | (blk_ids >= n_used[0])
    zero_flag = is_last.astype(I32)
    pos = (pad_start[e_flat] + rank_flat).astype(I32)
    prev_e = jnp.concatenate([jnp.full((1,), -1, I32), blk_e[:-1]])
    blk_first = ((blk_e != prev_e) & (blk_ids < n_used[0])).astype(I32)
    blk_ord = (jnp.cumsum(blk_first) - 1).astype(I32)
    e_ids = jnp.arange(N_EXPERTS, dtype=I32)
    nonempty_id = jnp.where(nblk > 0, e_ids, N_EXPERTS)
    next_ge = lax.cummin(nonempty_id, reverse=True)
    next_after = jnp.concatenate([next_ge[1:], jnp.full((1,), N_EXPERTS, I32)])
    blk_next = next_after[blk_e].astype(I32)
    xs = moe_dispatch(h, pos, zero_flag, nb * MOE_BLOCK)
    y = moe_experts(xs, blk_e, blk_first, blk_ord, blk_next, n_used, w_gate, w_up, w_down, li)
    return moe_combine(x, y, gates, pos, g_final, final_norm)


def kernel(x, rel_bias, norm_mix, norm_ffn, norm_final, w_in_ab, lam_ab, subln_ab, w_gk_ab, b_gk_ab,
           gnorm_ab, w_out_ab, w_in_c, ln_v_g, ln_v_b, w_spatial, b_spatial, w_out_c,
           w_router_group, w_router_expert, w_exp_gate, w_exp_up, w_exp_down):
    bsz, s_len, d = x.shape
    depth = norm_mix.shape[0]
    xf = x.reshape(bsz * s_len, d)
    for layer in range(depth):
        i = layer // 2
        if layer % 2 == 0:
            lam_init = 0.8 - 0.6 * math.exp(-0.3 * layer)
            n_main = 3 * A_WIDTH + 2 * B_HEADS * B_KEY_DIM + 2 * B_WIDTH
            proj, gb = norm_proj(xf, norm_mix[layer], w_in_ab, i, n_main)
            o_a = diff_attention(proj, rel_bias, lam_ab[i], subln_ab[i], lam_init, bsz, s_len)
            o_b = gla_mixer(proj, gb, w_gk_ab[i], b_gk_ab[i], gnorm_ab[i], bsz, s_len)
            xf = out_proj(o_a, 0, o_b, 0, w_out_ab, i, xf)
        else:
            proj = norm_proj(xf, norm_mix[layer], w_in_c, i, w_in_c.shape[2], act="gelu")
            uz = spatial_gate(proj, ln_v_g[i], ln_v_b[i], w_spatial[i], b_spatial[i])
            xf = out_proj(uz, 0, uz, 1, w_out_c, i, xf)
        xf = moe_layer(xf, norm_ffn[layer], w_router_group[layer], w_router_expert[layer],
                       w_exp_gate, w_exp_up, w_exp_down, layer,
                       norm_final, layer == depth - 1)
    return xf.reshape(bsz, s_len, d)
```

```python
import functools
import math

import jax
import jax.numpy as jnp
import numpy as np
from jax import lax
from jax.experimental import pallas as pl
from jax.experimental.pallas import tpu as pltpu

F32 = jnp.float32
BF16 = jnp.bfloat16
I32 = jnp.int32
U32 = jnp.uint32
HIGHEST = lax.Precision.HIGHEST

EPS = 1e-6
NEG = -1e30
A_HEADS = 8
A_QK_DIM = 64
A_V_DIM = 128
A_WIDTH = A_HEADS * A_V_DIM
N_BUCKETS = 32
MAX_EXACT = 16
MAX_DISTANCE = 128
B_HEADS = 4
B_KEY_DIM = 128
B_VAL_DIM = 256
B_WIDTH = B_HEADS * B_VAL_DIM
GATE_RANK = 16
GATE_TEMP = 16.0
GLA_CHUNK = 64
C_GROUPS = 8
C_CHUNK = 128
N_GROUPS = 4
EXPERTS_PER_GROUP = 8
N_EXPERTS = N_GROUPS * EXPERTS_PER_GROUP
TOP_K = 2
LANES = 128
ROUTER_LANE0 = N_GROUPS
MOE_BLOCK = 256
FAR_TILES = 4
Q_SPLIT = 2
VMEM_LIMIT = 56 * 1024 * 1024

NT_DIMS = (((1,), (1,)), ((), ()))
TN_DIMS = (((0,), (0,)), ((), ()))


def _t5_thresholds():
    n = np.arange(0, 2 * MAX_DISTANCE)
    nf = np.maximum(n, 1).astype(np.float64)
    large = MAX_EXACT + (np.log(nf / MAX_EXACT) / math.log(MAX_DISTANCE / MAX_EXACT)
                         * (N_BUCKETS - MAX_EXACT)).astype(np.int64)
    bucket = np.where(n < MAX_EXACT, n, np.minimum(large, N_BUCKETS - 1))
    return [int(np.min(n[bucket >= b])) for b in range(N_BUCKETS)]


T5_THRESHOLDS = _t5_thresholds()


def _params(*sem):
    return pltpu.CompilerParams(dimension_semantics=sem, vmem_limit_bytes=VMEM_LIMIT)


def _rms(x, g):
    ms = jnp.mean(x * x, axis=-1, keepdims=True)
    return x * lax.rsqrt(ms + EPS) * g


def _gelu_tanh(x):
    c = math.sqrt(2.0 / math.pi)
    return 0.5 * x * (1.0 + jnp.tanh(c * (x + 0.044715 * (x * x * x))))


def _sigmoid(x):
    return 1.0 / (1.0 + jnp.exp(-x))


def _pack_bf16_pairs(x):
    c = x.shape[1] // 2
    bits = lax.bitcast_convert_type(x.astype(BF16).astype(F32), U32)
    return (bits[:, :c] >> 16) | (bits[:, c:] & jnp.uint32(0xFFFF0000))


def _unpack_bf16_pairs(w):
    lo = lax.bitcast_convert_type(w << 16, F32)
    hi = lax.bitcast_convert_type(w & jnp.uint32(0xFFFF0000), F32)
    return lo, hi


def _norm_proj_kernel(*refs, act, has_extra):
    if has_extra:
        x_ref, g_ref, w_ref, we_ref, o_ref, oe_ref, h_scr = refs
    else:
        x_ref, g_ref, w_ref, o_ref, h_scr = refs

    @pl.when(pl.program_id(1) == 0)
    def _():
        hb = _rms(x_ref[...], g_ref[...]).astype(BF16)
        h_scr[...] = hb
        if has_extra:
            ye = lax.dot_general(hb, we_ref[...].astype(BF16), NT_DIMS, preferred_element_type=F32)
            oe_ref[...] = ye[:, :oe_ref.shape[1]]

    if has_extra:
        y = lax.dot_general(h_scr[...], w_ref[...].astype(BF16), NT_DIMS,
                            preferred_element_type=F32)
    else:
        y = jnp.dot(h_scr[...], w_ref[...].astype(BF16), preferred_element_type=F32)
    if act == "gelu":
        y = _gelu_tanh(y)
    o_ref[...] = y.astype(o_ref.dtype)


def norm_proj(x, g, w, li, n_cols, *, act=None, tm=1024, tn=1024):
    t, d = x.shape
    tm = min(tm, t)
    n_extra = w.shape[2] - n_cols
    assert n_cols % tn == 0 and n_cols % LANES == 0 and 0 <= n_extra <= LANES
    grid = (t // tm, n_cols // tn)
    in_specs = [pl.BlockSpec((tm, d), lambda i, j: (i, 0)),
                pl.BlockSpec((1, d), lambda i, j: (0, 0))]
    out_specs = [pl.BlockSpec((tm, tn), lambda i, j: (i, j))]
    out_shape = [jax.ShapeDtypeStruct((t, n_cols), BF16)]
    args = [x, g.reshape(1, d)]
    if n_extra:
        wt = jnp.swapaxes(w, 1, 2)
        in_specs.append(pl.BlockSpec((None, tn, d), lambda i, j: (li, j, 0)))
        in_specs.append(pl.BlockSpec((None, LANES, d), lambda i, j: (li, n_cols // LANES, 0)))
        out_specs.append(pl.BlockSpec((tm, n_extra), lambda i, j: (i, 0)))
        out_shape.append(jax.ShapeDtypeStruct((t, n_extra), F32))
        args += [wt, wt]
    else:
        in_specs.append(pl.BlockSpec((None, d, tn), lambda i, j: (li, 0, j)))
        args.append(w)
    outs = pl.pallas_call(
        functools.partial(_norm_proj_kernel, act=act, has_extra=bool(n_extra)),
        grid=grid, in_specs=in_specs, out_specs=out_specs, out_shape=out_shape,
        scratch_shapes=[pltpu.VMEM((tm, d), BF16)],
        compiler_params=_params("parallel", "arbitrary"),
        name="norm_proj_" + (act or "lin"),
    )(*args)
    return outs if n_extra else outs[0]


def _out_proj_kernel(a0_ref, a1_ref, w_ref, r_ref, o_ref):
    kh = a0_ref.shape[1]
    w = w_ref[...].astype(BF16)
    y = jnp.dot(a0_ref[...], w[:kh], preferred_element_type=F32)
    y = y + jnp.dot(a1_ref[...], w[kh:], preferred_element_type=F32)
    o_ref[...] = r_ref[...] + y


def out_proj(a0, a0_blk, a1, a1_blk, w, li, res, *, tm=1024, tn=1024):
    t, n = res.shape
    k = w.shape[1]
    kh = k // 2
    tm = min(tm, t)
    return pl.pallas_call(
        _out_proj_kernel,
        grid=(t // tm, n // tn),
        in_specs=[pl.BlockSpec((tm, kh), lambda i, j: (i, a0_blk)),
                  pl.BlockSpec((tm, kh), lambda i, j: (i, a1_blk)),
                  pl.BlockSpec((None, k, tn), lambda i, j: (li, 0, j)),
                  pl.BlockSpec((tm, tn), lambda i, j: (i, j))],
        out_specs=pl.BlockSpec((tm, tn), lambda i, j: (i, j)),
        out_shape=jax.ShapeDtypeStruct((t, n), F32),
        compiler_params=_params("parallel", "arbitrary"),
        name="out_proj",
    )(a0, a1, w, res)


def _diff_attn_kernel(tbl_ref, q_ref, k_ref, v_ref, lam_ref, sg_ref, o_ref,
                      m_scr, acc_scr, bias_scr, q_scr, v_scr, *, t, lam_init):
    h = pl.program_id(1)
    qi = pl.program_id(2)
    scale = A_QK_DIM ** -0.5

    @pl.when(qi == 0)
    def _build_bias():
        v_scr[:, :A_V_DIM] = v_ref[...]
        v_scr[:, A_V_DIM:] = jnp.ones((v_scr.shape[0], A_V_DIM), BF16)
        row = lax.broadcasted_iota(I32, (t, t), 0)
        col = lax.broadcasted_iota(I32, (t, t), 1)
        for d in range(2):
            rel = row - col + d * t
            acc = jnp.full((t, t), tbl_ref[h], F32)
            for b in range(1, N_BUCKETS):
                acc = jnp.where(rel >= T5_THRESHOLDS[b], tbl_ref[b * A_HEADS + h], acc)
            if d == 0:
                acc = jnp.where(rel >= 0, acc, NEG)
            bias_scr[:, (1 - d) * t:(2 - d) * t] = acc

    m_scr[...] = jnp.full(m_scr.shape, NEG, F32)
    acc_scr[...] = jnp.zeros(acc_scr.shape, F32)
    q = q_ref[...] * scale
    lane = lax.broadcasted_iota(I32, q.shape, 1)
    q_scr[0] = jnp.where(lane < A_QK_DIM, q, jnp.zeros_like(q))
    q_scr[1] = jnp.where(lane >= A_QK_DIM, q, jnp.zeros_like(q))

    def step(ki, n_tiles, bias_cols, bias_const):
        rows = pl.ds(pl.multiple_of(ki * t, t), n_tiles * t)
        k = k_ref[rows, :]
        v_aug = v_scr[rows, :]
        for m in range(2):
            for half in range(Q_SPLIT):
                hs = t // Q_SPLIT
                rs = slice(half * hs, (half + 1) * hs)
                s = lax.dot_general(q_scr[m, rs, :], k, NT_DIMS, preferred_element_type=F32)
                if bias_cols is not None:
                    s = s + bias_scr[rs, bias_cols]
                m_old = m_scr[m, rs, :]
                m_new = jnp.maximum(m_old, jnp.max(s, axis=-1, keepdims=True) + bias_const)
                alpha = jnp.exp(m_old - m_new)
                p = jnp.exp((s - (m_new - bias_const)).astype(BF16))
                acc_scr[m, rs, :] = alpha * acc_scr[m, rs, :] + jnp.dot(
                    p, v_aug, preferred_element_type=F32)
                m_scr[m, rs, :] = m_new

    far_bias = tbl_ref[(N_BUCKETS - 1) * A_HEADS + h]

    n_far = jnp.maximum(qi - 1, 0)
    n_groups = n_far // FAR_TILES
    rest = n_far - n_groups * FAR_TILES

    def far(i, carry):
        step(i * FAR_TILES, FAR_TILES, None, far_bias)
        return carry

    lax.fori_loop(0, n_groups, far, 0)

    @pl.when(rest >= 2)
    def _far_pair():
        step(n_groups * FAR_TILES, 2, None, far_bias)

    @pl.when(rest % 2 == 1)
    def _far_single():
        step(n_far - 1, 1, None, far_bias)

    @pl.when(qi >= 1)
    def _near_and_diag():
        step(qi - 1, 2, slice(0, 2 * t), 0.0)

    @pl.when(qi == 0)
    def _diag_only():
        step(0, 1, slice(t, 2 * t), 0.0)

    lv = lam_ref[...]
    s1 = jnp.sum(lv[0:1] * lv[1:2], axis=-1, keepdims=True)
    s2 = jnp.sum(lv[2:3] * lv[3:4], axis=-1, keepdims=True)
    lam = jnp.exp(s1) - jnp.exp(s2) + lam_init
    a0 = acc_scr[0]
    a1 = acc_scr[1]
    o = (a0[:, :A_V_DIM] / a0[:, A_V_DIM:A_V_DIM + 1]
         - lam * (a1[:, :A_V_DIM] / a1[:, A_V_DIM:A_V_DIM + 1]))
    o = _rms(o, sg_ref[...]) * (1.0 - lam_init)
    o_ref[...] = o.astype(o_ref.dtype)


def diff_attention(proj, rel_bias, lam_vec, subln_g, lam_init, bsz, s_len, *, t=512):
    t = min(t, s_len)
    assert t >= MAX_DISTANCE and s_len % t == 0 and FAR_TILES == 4
    nq = s_len // t
    q_map = lambda b, h, qi: (b * nq + qi, h)
    kv_map = lambda off: (lambda b, h, qi: (b, off + h))
    const = lambda b, h, qi: (0, 0)
    return pl.pallas_call(
        functools.partial(_diff_attn_kernel, t=t, lam_init=lam_init),
        grid=(bsz, A_HEADS, nq),
        in_specs=[pl.BlockSpec(memory_space=pltpu.SMEM),
                  pl.BlockSpec((t, A_V_DIM), q_map),
                  pl.BlockSpec((s_len, A_V_DIM), kv_map(A_HEADS)),
                  pl.BlockSpec((s_len, A_V_DIM), kv_map(2 * A_HEADS)),
                  pl.BlockSpec((4, A_QK_DIM), const),
                  pl.BlockSpec((1, A_V_DIM), const)],
        out_specs=pl.BlockSpec((t, A_V_DIM), q_map),
        out_shape=jax.ShapeDtypeStruct((bsz * s_len, A_WIDTH), BF16),
        scratch_shapes=[pltpu.VMEM((2, t, 1), F32),
                        pltpu.VMEM((2, t, 2 * A_V_DIM), F32),
                        pltpu.VMEM((t, 2 * t), F32),
                        pltpu.VMEM((2, t, A_V_DIM), BF16),
                        pltpu.VMEM((s_len, 2 * A_V_DIM), BF16)],
        compiler_params=_params("parallel", "arbitrary", "arbitrary"),
        name="diff_attention",
    )(rel_bias.reshape(-1), proj, proj, proj, lam_vec, subln_g.reshape(1, A_V_DIM))


def _gla_kernel(q_ref, k_ref, v_ref, r_ref, gb_ref, wgk_ref, bgk_ref, gn_ref, o_ref, st_scr, *, rows):
    c_len = GLA_CHUNK
    scale = B_KEY_DIM ** -0.5

    @pl.when(pl.program_id(1) == 0)
    def _():
        st_scr[...] = jnp.zeros(st_scr.shape, F32)

    z = jnp.dot(gb_ref[...], wgk_ref[...], preferred_element_type=F32, precision=HIGHEST)
    z = z + bgk_ref[...]
    g_log = (jnp.minimum(z, 0.0) - jnp.log(1.0 + jnp.exp(-jnp.abs(z)))) * (1.0 / GATE_TEMP)
    row = lax.broadcasted_iota(I32, (rows, rows), 0)
    col = lax.broadcasted_iota(I32, (rows, rows), 1)
    ltri = ((row >= col) & ((row // c_len) == (col // c_len))).astype(F32)
    bcum = jnp.dot(ltri, g_log, preferred_element_type=F32, precision=HIGHEST)
    causal = (lax.broadcasted_iota(I32, (c_len, c_len), 0)
              >= lax.broadcasted_iota(I32, (c_len, c_len), 1))
    gn = gn_ref[...]

    for c in range(rows // c_len):
        rs = slice(c * c_len, (c + 1) * c_len)
        for hh in range(B_HEADS):
            ks = slice(hh * B_KEY_DIM, (hh + 1) * B_KEY_DIM)
            vs = slice(hh * B_VAL_DIM, (hh + 1) * B_VAL_DIM)
            b = bcum[rs, ks]
            b_last = b[c_len - 1:c_len, :]
            b_mid = b[c_len // 2:c_len // 2 + 1, :]
            q = q_ref[rs, ks].astype(F32) * scale
            k = k_ref[rs, ks].astype(F32)
            v = v_ref[rs, vs]
            st = st_scr[hh]
            q_in = (q * jnp.exp(b)).astype(BF16)
            q_ia = (q * jnp.exp(b - b_mid)).astype(BF16)
            k_ia = (k * jnp.exp(b_mid - b)).astype(BF16)
            att = lax.dot_general(q_ia, k_ia, NT_DIMS, preferred_element_type=F32)
            att = jnp.where(causal, att, 0.0)
            o = jnp.dot(att.astype(BF16), v, preferred_element_type=F32)
            o = o + lax.dot_general(q_in, st.astype(BF16), NT_DIMS, preferred_element_type=F32)
            k_st = (k * jnp.exp(b_last - b)).astype(BF16)
            st_scr[hh] = st * jnp.exp(b_last) + lax.dot_general(
                v, k_st, TN_DIMS, preferred_element_type=F32)
            on = _rms(o, gn)
            r = r_ref[rs, vs].astype(F32)
            o_ref[rs, vs] = (on * (r * _sigmoid(r))).astype(o_ref.dtype)


def gla_mixer(proj, gb, w_gk, b_gk, gnorm_g, bsz, s_len, *, rows=256):
    rows = min(rows, s_len)
    nr = s_len // rows
    kw = B_HEADS * B_KEY_DIM
    rmap = lambda blk: (lambda b, i: (b * nr + i, blk))
    return pl.pallas_call(
        functools.partial(_gla_kernel, rows=rows),
        grid=(bsz, nr),
        in_specs=[pl.BlockSpec((rows, kw), rmap(3 * A_WIDTH // kw)),
                  pl.BlockSpec((rows, kw), rmap(3 * A_WIDTH // kw + 1)),
                  pl.BlockSpec((rows, B_WIDTH), rmap((3 * A_WIDTH + 2 * kw) // B_WIDTH)),
                  pl.BlockSpec((rows, B_WIDTH), rmap((3 * A_WIDTH + 2 * kw) // B_WIDTH + 1)),
                  pl.BlockSpec((rows, GATE_RANK), rmap(0)),
                  pl.BlockSpec((GATE_RANK, kw), lambda b, i: (0, 0)),
                  pl.BlockSpec((1, kw), lambda b, i: (0, 0)),
                  pl.BlockSpec((1, B_VAL_DIM), lambda b, i: (0, 0))],
        out_specs=pl.BlockSpec((rows, B_WIDTH), rmap(0)),
        out_shape=jax.ShapeDtypeStruct((bsz * s_len, B_WIDTH), BF16),
        scratch_shapes=[pltpu.VMEM((B_HEADS, B_VAL_DIM, B_KEY_DIM), F32)],
        compiler_params=_params("parallel", "arbitrary"),
        name="gla",
    )(proj, proj, proj, proj, gb, w_gk, b_gk.reshape(1, kw), gnorm_g.reshape(1, B_VAL_DIM))


def _spatial_kernel(u_ref, v_ref, lg_ref, lb_ref, ws_ref, bs_ref, o_ref, *, rows):
    v = v_ref[...].astype(F32)
    mu = jnp.mean(v, axis=-1, keepdims=True)
    vc = v - mu
    var = jnp.mean(vc * vc, axis=-1, keepdims=True)
    vn = (vc * lax.rsqrt(var + EPS) * lg_ref[...] + lb_ref[...]).astype(BF16)
    gd = v.shape[1] // C_GROUPS
    mask = (lax.broadcasted_iota(I32, (C_CHUNK, C_CHUNK), 0)
            >= lax.broadcasted_iota(I32, (C_CHUNK, C_CHUNK), 1))
    for g in range(C_GROUPS):
        w = jnp.where(mask, ws_ref[g], 0.0).astype(BF16)
        bias = bs_ref[:, g:g + 1]
        cs = slice(g * gd, (g + 1) * gd)
        for c in range(rows // C_CHUNK):
            rs = slice(c * C_CHUNK, (c + 1) * C_CHUNK)
            z = jnp.dot(w, vn[rs, cs], preferred_element_type=F32) + bias
            o_ref[rs, cs] = (u_ref[rs, cs].astype(F32) * z).astype(o_ref.dtype)


def spatial_gate(proj, ln_g, ln_b, w_s, b_s, *, rows=512):
    t = proj.shape[0]
    cw = proj.shape[1] // 2
    rows = min(rows, t)
    return pl.pallas_call(
        functools.partial(_spatial_kernel, rows=rows),
        grid=(t // rows,),
        in_specs=[pl.BlockSpec((rows, cw), lambda i: (i, 0)),
                  pl.BlockSpec((rows, cw), lambda i: (i, 1)),
                  pl.BlockSpec((1, cw), lambda i: (0, 0)),
                  pl.BlockSpec((1, cw), lambda i: (0, 0)),
                  pl.BlockSpec((C_GROUPS, C_CHUNK, C_CHUNK), lambda i: (0, 0, 0)),
                  pl.BlockSpec((C_CHUNK, C_GROUPS), lambda i: (0, 0))],
        out_specs=pl.BlockSpec((rows, cw), lambda i: (i, 0)),
        out_shape=jax.ShapeDtypeStruct((t, cw), BF16),
        compiler_params=_params("parallel"),
        name="spatial_gate",
    )(proj, proj, ln_g.reshape(1, cw), ln_b.reshape(1, cw), w_s, b_s.T)


def _router_kernel(x_ref, g_ref, wr_ref, h_ref, ri_ref, rg_ref, cnt_ref, carry_scr, *, tr):
    @pl.when(pl.program_id(0) == 0)
    def _():
        carry_scr[...] = jnp.zeros(carry_scr.shape, F32)

    h = _rms(x_ref[...], g_ref[...])
    h_ref[...] = _pack_bf16_pairs(h)
    h_hi = h.astype(BF16)
    h_lo = (h - h_hi.astype(F32)).astype(BF16)
    w_hi = wr_ref[0]
    w_lo = wr_ref[1]
    logits = (jnp.dot(h_hi, w_hi, preferred_element_type=F32)
              + (jnp.dot(h_hi, w_lo, preferred_element_type=F32)
                 + jnp.dot(h_lo, w_hi, preferred_element_type=F32)))
    lane = lax.broadcasted_iota(I32, (tr, LANES), 1)
    lanef = lane.astype(F32)
    big = float(LANES)

    is_g = lane < N_GROUPS
    lg = jnp.where(is_g, logits, NEG)
    gmax = jnp.max(lg, axis=-1, keepdims=True)
    p_grp = 1.0 / jnp.sum(jnp.exp(lg - gmax), axis=-1, keepdims=True)
    gidx = jnp.min(jnp.where(is_g & (lg == gmax), lanef, big), axis=-1, keepdims=True)

    lo = ROUTER_LANE0 + EXPERTS_PER_GROUP * gidx
    in_e = (lanef >= lo) & (lanef < lo + EXPERTS_PER_GROUP)
    le = jnp.where(in_e, logits, NEG)
    emax = jnp.max(le, axis=-1, keepdims=True)
    i1 = jnp.min(jnp.where(in_e & (le == emax), lanef, big), axis=-1, keepdims=True)
    oh1 = lanef == i1
    le2 = jnp.where(oh1, NEG, le)
    emax2 = jnp.max(le2, axis=-1, keepdims=True)
    i2 = jnp.min(jnp.where(in_e & (le2 == emax2) & (~oh1), lanef, big), axis=-1, keepdims=True)
    oh2 = lanef == i2
    ratio = jnp.exp(emax2 - emax)
    g1 = p_grp / (1.0 + ratio)
    g2 = p_grp * ratio / (1.0 + ratio)

    onehot = (oh1 | oh2).astype(BF16)
    tri = (lax.broadcasted_iota(I32, (tr, tr), 0) > lax.broadcasted_iota(I32, (tr, tr), 1))
    before = jnp.dot(tri.astype(BF16), onehot, preferred_element_type=F32) + carry_scr[...]
    rank1 = jnp.sum(jnp.where(oh1, before, 0.0), axis=-1, keepdims=True)
    rank2 = jnp.sum(jnp.where(oh2, before, 0.0), axis=-1, keepdims=True)
    carry = carry_scr[...] + jnp.sum(onehot.astype(F32), axis=0, keepdims=True)
    carry_scr[...] = carry

    e1 = i1 - ROUTER_LANE0
    e2 = i2 - ROUTER_LANE0
    ri = jnp.where(lane == 0, e1, jnp.where(lane == 1, e2,
                                             jnp.where(lane == 2, rank1,
                                                       jnp.where(lane == 3, rank2, 0.0))))
    ri_ref[...] = ri.astype(I32)
    rg_ref[...] = jnp.where(lane == 0, g1, jnp.where(lane == 1, g2, 0.0))
    cnt_ref[...] = jnp.broadcast_to(carry, cnt_ref.shape).astype(I32)


def moe_router(x, g, w_rg, w_re, *, tr=512):
    t, d = x.shape
    tr = min(tr, t)
    w_r = jnp.zeros((d, LANES), F32)
    w_r = w_r.at[:, :N_GROUPS].set(w_rg).at[:, ROUTER_LANE0:ROUTER_LANE0 + N_EXPERTS].set(w_re)
    w_hi = w_r.astype(BF16)
    w_r = jnp.stack([w_hi, (w_r - w_hi.astype(F32)).astype(BF16)])
    return pl.pallas_call(
        functools.partial(_router_kernel, tr=tr),
        grid=(t // tr,),
        in_specs=[pl.BlockSpec((tr, d), lambda i: (i, 0)),
                  pl.BlockSpec((1, d), lambda i: (0, 0)),
                  pl.BlockSpec((2, d, LANES), lambda i: (0, 0, 0))],
        out_specs=[pl.BlockSpec((tr, d // 2), lambda i: (i, 0)),
                   pl.BlockSpec((tr, LANES), lambda i: (i, 0)),
                   pl.BlockSpec((tr, LANES), lambda i: (i, 0)),
                   pl.BlockSpec((8, LANES), lambda i: (0, 0))],
        out_shape=[jax.ShapeDtypeStruct((t, d // 2), U32),
                   jax.ShapeDtypeStruct((t, LANES), I32),
                   jax.ShapeDtypeStruct((t, LANES), F32),
                   jax.ShapeDtypeStruct((8, LANES), I32)],
        scratch_shapes=[pltpu.VMEM((1, LANES), F32)],
        compiler_params=_params("arbitrary"),
        name="moe_router",
    )(x, g.reshape(1, d), w_r)


DMA_UNROLL = 8


def _dispatch_kernel(pos_ref, zf_ref, h_ref, xs_ref, zero_scr, sem, zsem, *, td, nb):
    i = pl.program_id(0)
    base = i * td

    @pl.when(i == 0)
    def _():
        zero_scr[...] = jnp.zeros(zero_scr.shape, zero_scr.dtype)

        def zero_copy(j):
            return pltpu.make_async_copy(zero_scr, xs_ref.at[pl.ds(j * MOE_BLOCK, MOE_BLOCK)], zsem)

        def zstart(j, carry):
            @pl.when(zf_ref[j] == 1)
            def _():
                zero_copy(j).start()
            return carry

        def zwait(j, carry):
            @pl.when(zf_ref[j] == 1)
            def _():
                zero_copy(j).wait()
            return carry

        lax.fori_loop(0, nb, zstart, 0)
        lax.fori_loop(0, nb, zwait, 0)

    def issue(r, carry):
        for kk in range(TOP_K):
            pos = pos_ref[(base + r) * TOP_K + kk]
            pltpu.make_async_copy(h_ref.at[pl.ds(r, 1)], xs_ref.at[pl.ds(pos, 1)], sem).start()
        return carry

    lax.fori_loop(0, td, issue, 0, unroll=DMA_UNROLL)
    for kk in range(TOP_K):
        pltpu.make_async_copy(h_ref, xs_ref.at[pl.ds(0, td)], sem).wait()


def moe_dispatch(h, pos, zero_flag, n_rows, *, td=256):
    t, d = h.shape
    td = min(td, t)
    nb = n_rows // MOE_BLOCK
    grid_spec = pltpu.PrefetchScalarGridSpec(
        num_scalar_prefetch=2, grid=(t // td,),
        in_specs=[pl.BlockSpec((td, d), lambda i, *_: (i, 0))],
        out_specs=pl.BlockSpec(memory_space=pl.ANY),
        scratch_shapes=[pltpu.VMEM((MOE_BLOCK, d), h.dtype),
                        pltpu.SemaphoreType.DMA(()), pltpu.SemaphoreType.DMA(())])
    return pl.pallas_call(
        functools.partial(_dispatch_kernel, td=td, nb=nb),
        grid_spec=grid_spec,
        out_shape=jax.ShapeDtypeStruct((n_rows, d), h.dtype),
        compiler_params=_params("arbitrary"),
        name="moe_dispatch",
    )(pos, zero_flag, h)


def _expert_kernel(blk_e_ref, first_ref, ord_ref, next_ref, nused_ref, xs_ref, wg_hbm, wu_hbm, wd_hbm,
                   y_ref, wg_f, wu_f, wd_f, wg_s, wu_s, wd_s, sems, *, li):
    j = pl.program_id(0)
    used = j < nused_ref[0]

    def weight_copies(e, slot):
        return [pltpu.make_async_copy(src.at[li, e], dst.at[slot], sems.at[slot])
                for src, dst in ((wg_hbm, wg_f), (wu_hbm, wu_f), (wd_hbm, wd_f))]

    @pl.when(j == 0)
    def _():
        for c in weight_copies(blk_e_ref[0], 0):
            c.start()

    @pl.when(used & (first_ref[j] == 1))
    def _():
        slot = ord_ref[j] % 2
        for c in weight_copies(blk_e_ref[j], slot):
            c.wait()
        nxt = next_ref[j]

        @pl.when(nxt < N_EXPERTS)
        def _():
            for c in weight_copies(nxt, 1 - slot):
                c.start()

        wg_s[...] = wg_f[slot].astype(BF16)
        wu_s[...] = wu_f[slot].astype(BF16)
        wd_s[...] = wd_f[slot].astype(BF16)

    @pl.when(used)
    def _():
        x_lo, x_hi = _unpack_bf16_pairs(xs_ref[...])
        x_lo = x_lo.astype(BF16)
        x_hi = x_hi.astype(BF16)
        dh = x_lo.shape[1]

        def proj(w_s):
            return (jnp.dot(x_lo, w_s[:dh, :], preferred_element_type=F32)
                    + jnp.dot(x_hi, w_s[dh:, :], preferred_element_type=F32))

        gate = proj(wg_s)
        up = proj(wu_s)
        act = (gate * _sigmoid(gate) * up).astype(BF16)
        y_ref[...] = _pack_bf16_pairs(jnp.dot(act, wd_s[...], preferred_element_type=F32))

    @pl.when(jnp.logical_not(used))
    def _():
        y_ref[...] = jnp.zeros(y_ref.shape, y_ref.dtype)


def moe_experts(xs, blk_e, blk_first, blk_ord, blk_next, n_used, w_gate, w_up, w_down, li):
    n_rows, dp = xs.shape
    d = w_gate.shape[2]
    ff = w_gate.shape[3]
    nb = n_rows // MOE_BLOCK
    grid_spec = pltpu.PrefetchScalarGridSpec(
        num_scalar_prefetch=5, grid=(nb,),
        in_specs=[pl.BlockSpec((MOE_BLOCK, dp), lambda j, be, bf, bo, bn, nu: (jnp.minimum(j, nu[0] - 1), 0)),
                  pl.BlockSpec(memory_space=pl.ANY),
                  pl.BlockSpec(memory_space=pl.ANY),
                  pl.BlockSpec(memory_space=pl.ANY)],
        out_specs=pl.BlockSpec((MOE_BLOCK, dp), lambda j, *_: (j, 0)),
        scratch_shapes=[pltpu.VMEM((2, d, ff), F32), pltpu.VMEM((2, d, ff), F32),
                        pltpu.VMEM((2, ff, d), F32),
                        pltpu.VMEM((d, ff), BF16), pltpu.VMEM((d, ff), BF16),
                        pltpu.VMEM((ff, d), BF16),
                        pltpu.SemaphoreType.DMA((2,))])
    return pl.pallas_call(
        functools.partial(_expert_kernel, li=li),
        grid_spec=grid_spec,
        out_shape=jax.ShapeDtypeStruct((n_rows, dp), U32),
        compiler_params=_params("arbitrary"),
        name="moe_experts",
    )(blk_e, blk_first, blk_ord, blk_next, n_used, xs, w_gate, w_up, w_down)


def _combine_kernel(pos_ref, x_ref, rg_ref, gf_ref, y_ref, o_ref, ybuf, sems, *, tc, final_norm):
    i = pl.program_id(0)
    slot = i % 2

    def issue_tile(tile, dst_slot):
        base = tile * tc

        def issue(r, carry):
            for kk in range(TOP_K):
                pos = pos_ref[(base + r) * TOP_K + kk]
                pltpu.make_async_copy(y_ref.at[pl.ds(pos, 1)], ybuf.at[dst_slot, kk, pl.ds(r, 1)],
                                      sems.at[dst_slot]).start()
            return carry

        lax.fori_loop(0, tc, issue, 0, unroll=DMA_UNROLL)

    @pl.when(i == 0)
    def _():
        issue_tile(0, 0)

    @pl.when(i + 1 < pl.num_programs(0))
    def _():
        issue_tile(i + 1, 1 - slot)

    for kk in range(TOP_K):
        pltpu.make_async_copy(y_ref.at[pl.ds(0, tc)], ybuf.at[slot, kk], sems.at[slot]).wait()
    gates = rg_ref[...]
    g0 = gates[:, 0:1]
    g1 = gates[:, 1:2]
    y0_lo, y0_hi = _unpack_bf16_pairs(ybuf[slot, 0])
    y1_lo, y1_hi = _unpack_bf16_pairs(ybuf[slot, 1])
    dh = y0_lo.shape[1]
    out_lo = x_ref[:, :dh] + (g0 * y0_lo + g1 * y1_lo)
    out_hi = x_ref[:, dh:] + (g0 * y0_hi + g1 * y1_hi)
    if final_norm:
        ms = (jnp.sum(out_lo * out_lo, axis=-1, keepdims=True)
              + jnp.sum(out_hi * out_hi, axis=-1, keepdims=True)) * (1.0 / (2 * dh))
        inv = lax.rsqrt(ms + EPS)
        out_lo = out_lo * inv * gf_ref[:, :dh]
        out_hi = out_hi * inv * gf_ref[:, dh:]
    o_ref[:, :dh] = out_lo
    o_ref[:, dh:] = out_hi


def moe_combine(x, y, gates, pos, g_final, final_norm, *, tc=256):
    t, d = x.shape
    tc = min(tc, t)
    grid_spec = pltpu.PrefetchScalarGridSpec(
        num_scalar_prefetch=1, grid=(t // tc,),
        in_specs=[pl.BlockSpec((tc, d), lambda i, *_: (i, 0)),
                  pl.BlockSpec((tc, LANES), lambda i, *_: (i, 0)),
                  pl.BlockSpec((1, d), lambda i, *_: (0, 0)),
                  pl.BlockSpec(memory_space=pl.ANY)],
        out_specs=pl.BlockSpec((tc, d), lambda i, *_: (i, 0)),
        scratch_shapes=[pltpu.VMEM((2, TOP_K, tc, d // 2), U32), pltpu.SemaphoreType.DMA((2,))])
    return pl.pallas_call(
        functools.partial(_combine_kernel, tc=tc, final_norm=final_norm),
        grid_spec=grid_spec,
        out_shape=jax.ShapeDtypeStruct((t, d), F32),
        compiler_params=_params("arbitrary"),
        name="moe_combine",
    )(pos, x, gates, g_final.reshape(1, d), y)


def moe_layer(x, g_norm, w_rg, w_re, w_gate, w_up, w_down, li, g_final, final_norm):
    t, d = x.shape
    h, ri, gates, cnt = moe_router(x, g_norm, w_rg, w_re)
    e_flat = ri[:, 0:TOP_K].reshape(-1)
    rank_flat = ri[:, TOP_K:2 * TOP_K].reshape(-1)
    counts = cnt[0, ROUTER_LANE0:ROUTER_LANE0 + N_EXPERTS]
    nblk = (counts + MOE_BLOCK - 1) // MOE_BLOCK
    blk_end = jnp.cumsum(nblk)
    pad_start = ((blk_end - nblk) * MOE_BLOCK).astype(I32)
    n_used = blk_end[-1:].astype(I32)
    nb = (t * TOP_K) // MOE_BLOCK + N_EXPERTS
    blk_ids = jnp.arange(nb, dtype=I32)
    blk = jnp.minimum(blk_ids, n_used[0] - 1)
    blk_e = jnp.sum(blk_end[None, :] <= blk[:, None], axis=1).astype(I32)
    is_last = (blk_ids + 1 == blk_end[blk_e]) | (blk_ids >= n_used[0])
    zero_flag = is_last.astype(I32)
    pos = (pad_start[e_flat] + rank_flat).astype(I32)
    prev_e = jnp.concatenate([jnp.full((1,), -1, I32), blk_e[:-1]])
    blk_first = ((blk_e != prev_e) & (blk_ids < n_used[0])).astype(I32)
    blk_ord = (jnp.cumsum(blk_first) - 1).astype(I32)
    e_ids = jnp.arange(N_EXPERTS, dtype=I32)
    nonempty_id = jnp.where(nblk > 0, e_ids, N_EXPERTS)
    next_ge = lax.cummin(nonempty_id, reverse=True)
    next_after = jnp.concatenate([next_ge[1:], jnp.full((1,), N_EXPERTS, I32)])
    blk_next = next_after[blk_e].astype(I32)
    xs = moe_dispatch(h, pos, zero_flag, nb * MOE_BLOCK)
    y = moe_experts(xs, blk_e, blk_first, blk_ord, blk_next, n_used, w_gate, w_up, w_down, li)
    return moe_combine(x, y, gates, pos, g_final, final_norm)


def kernel(x, rel_bias, norm_mix, norm_ffn, norm_final, w_in_ab, lam_ab, subln_ab, w_gk_ab, b_gk_ab,
           gnorm_ab, w_out_ab, w_in_c, ln_v_g, ln_v_b, w_spatial, b_spatial, w_out_c,
           w_router_group, w_router_expert, w_exp_gate, w_exp_up, w_exp_down):
    bsz, s_len, d = x.shape
    depth = norm_mix.shape[0]
    xf = x.reshape(bsz * s_len, d)
    for layer in range(depth):
        i = layer // 2
        if layer % 2 == 0:
            lam_init = 0.8 - 0.6 * math.exp(-0.3 * layer)
            n_main = 3 * A_WIDTH + 2 * B_HEADS * B_KEY_DIM + 2 * B_WIDTH
            proj, gb = norm_proj(xf, norm_mix[layer], w_in_ab, i, n_main)
            o_a = diff_attention(proj, rel_bias, lam_ab[i], subln_ab[i], lam_init, bsz, s_len)
            o_b = gla_mixer(proj, gb, w_gk_ab[i], b_gk_ab[i], gnorm_ab[i], bsz, s_len)
            xf = out_proj(o_a, 0, o_b, 0, w_out_ab, i, xf)
        else:
            proj = norm_proj(xf, norm_mix[layer], w_in_c, i, w_in_c.shape[2], act="gelu")
            uz = spatial_gate(proj, ln_v_g[i], ln_v_b[i], w_spatial[i], b_spatial[i])
            xf = out_proj(uz, 0, uz, 1, w_out_c, i, xf)
        xf = moe_layer(xf, norm_ffn[layer], w_router_group[layer], w_router_expert[layer],
                       w_exp_gate, w_exp_up, w_exp_down, layer,
                       norm_final, layer == depth - 1)
    return xf.reshape(bsz, s_len, d)
```

```python
import functools
import math

import jax
import jax.numpy as jnp
import numpy as np
from jax import lax
from jax.experimental import pallas as pl
from jax.experimental.pallas import tpu as pltpu

F32 = jnp.float32
BF16 = jnp.bfloat16
I32 = jnp.int32
U32 = jnp.uint32
HIGHEST = lax.Precision.HIGHEST

EPS = 1e-6
NEG = -1e30
A_HEADS = 8
A_QK_DIM = 64
A_V_DIM = 128
A_WIDTH = A_HEADS * A_V_DIM
N_BUCKETS = 32
MAX_EXACT = 16
MAX_DISTANCE = 128
B_HEADS = 4
B_KEY_DIM = 128
B_VAL_DIM = 256
B_WIDTH = B_HEADS * B_VAL_DIM
GATE_RANK = 16
GATE_TEMP = 16.0
GLA_CHUNK = 64
C_GROUPS = 8
C_CHUNK = 128
N_GROUPS = 4
EXPERTS_PER_GROUP = 8
N_EXPERTS = N_GROUPS * EXPERTS_PER_GROUP
TOP_K = 2
LANES = 128
ROUTER_LANE0 = N_GROUPS
MOE_BLOCK = 256
FAR_TILES = 4
Q_SPLIT_SINGLE = 2
Q_SPLIT_WIDE = 4
VMEM_LIMIT = 56 * 1024 * 1024

NT_DIMS = (((1,), (1,)), ((), ()))
TN_DIMS = (((0,), (0,)), ((), ()))


def _t5_thresholds():
    n = np.arange(0, 2 * MAX_DISTANCE)
    nf = np.maximum(n, 1).astype(np.float64)
    large = MAX_EXACT + (np.log(nf / MAX_EXACT) / math.log(MAX_DISTANCE / MAX_EXACT)
                         * (N_BUCKETS - MAX_EXACT)).astype(np.int64)
    bucket = np.where(n < MAX_EXACT, n, np.minimum(large, N_BUCKETS - 1))
    return [int(np.min(n[bucket >= b])) for b in range(N_BUCKETS)]


T5_THRESHOLDS = _t5_thresholds()


def _params(*sem):
    return pltpu.CompilerParams(dimension_semantics=sem, vmem_limit_bytes=VMEM_LIMIT)


def _rms(x, g):
    ms = jnp.mean(x * x, axis=-1, keepdims=True)
    return x * lax.rsqrt(ms + EPS) * g


def _gelu_tanh(x):
    c = math.sqrt(2.0 / math.pi)
    return 0.5 * x * (1.0 + jnp.tanh(c * (x + 0.044715 * (x * x * x))))


def _sigmoid(x):
    return 1.0 / (1.0 + jnp.exp(-x))


def _pack_bf16_pairs(x):
    c = x.shape[1] // 2
    bits = lax.bitcast_convert_type(x.astype(BF16).astype(F32), U32)
    return (bits[:, :c] >> 16) | (bits[:, c:] & jnp.uint32(0xFFFF0000))


def _unpack_bf16_pairs(w):
    lo = lax.bitcast_convert_type(w << 16, F32)
    hi = lax.bitcast_convert_type(w & jnp.uint32(0xFFFF0000), F32)
    return lo, hi


def _norm_proj_kernel(*refs, act, has_extra):
    if has_extra:
        x_ref, g_ref, w_ref, we_ref, o_ref, oe_ref, h_scr = refs
    else:
        x_ref, g_ref, w_ref, o_ref, h_scr = refs

    @pl.when(pl.program_id(1) == 0)
    def _():
        hb = _rms(x_ref[...], g_ref[...]).astype(BF16)
        h_scr[...] = hb
        if has_extra:
            ye = lax.dot_general(hb, we_ref[...].astype(BF16), NT_DIMS, preferred_element_type=F32)
            oe_ref[...] = ye[:, :oe_ref.shape[1]]

    if has_extra:
        y = lax.dot_general(h_scr[...], w_ref[...].astype(BF16), NT_DIMS,
                            preferred_element_type=F32)
    else:
        y = jnp.dot(h_scr[...], w_ref[...].astype(BF16), preferred_element_type=F32)
    if act == "gelu":
        y = _gelu_tanh(y)
    o_ref[...] = y.astype(o_ref.dtype)


def norm_proj(x, g, w, li, n_cols, *, act=None, tm=1024, tn=1024):
    t, d = x.shape
    tm = min(tm, t)
    n_extra = w.shape[2] - n_cols
    assert n_cols % tn == 0 and n_cols % LANES == 0 and 0 <= n_extra <= LANES
    grid = (t // tm, n_cols // tn)
    in_specs = [pl.BlockSpec((tm, d), lambda i, j: (i, 0)),
                pl.BlockSpec((1, d), lambda i, j: (0, 0))]
    out_specs = [pl.BlockSpec((tm, tn), lambda i, j: (i, j))]
    out_shape = [jax.ShapeDtypeStruct((t, n_cols), BF16)]
    args = [x, g.reshape(1, d)]
    if n_extra:
        wt = jnp.swapaxes(w, 1, 2)
        in_specs.append(pl.BlockSpec((None, tn, d), lambda i, j: (li, j, 0)))
        in_specs.append(pl.BlockSpec((None, LANES, d), lambda i, j: (li, n_cols // LANES, 0)))
        out_specs.append(pl.BlockSpec((tm, n_extra), lambda i, j: (i, 0)))
        out_shape.append(jax.ShapeDtypeStruct((t, n_extra), F32))
        args += [wt, wt]
    else:
        in_specs.append(pl.BlockSpec((None, d, tn), lambda i, j: (li, 0, j)))
        args.append(w)
    outs = pl.pallas_call(
        functools.partial(_norm_proj_kernel, act=act, has_extra=bool(n_extra)),
        grid=grid, in_specs=in_specs, out_specs=out_specs, out_shape=out_shape,
        scratch_shapes=[pltpu.VMEM((tm, d), BF16)],
        compiler_params=_params("parallel", "arbitrary"),
        name="norm_proj_" + (act or "lin"),
    )(*args)
    return outs if n_extra else outs[0]


def _out_proj_kernel(a0_ref, a1_ref, w_ref, r_ref, o_ref):
    kh = a0_ref.shape[1]
    w = w_ref[...].astype(BF16)
    y = jnp.dot(a0_ref[...], w[:kh], preferred_element_type=F32)
    y = y + jnp.dot(a1_ref[...], w[kh:], preferred_element_type=F32)
    o_ref[...] = r_ref[...] + y


def out_proj(a0, a0_blk, a1, a1_blk, w, li, res, *, tm=1024, tn=1024):
    t, n = res.shape
    k = w.shape[1]
    kh = k // 2
    tm = min(tm, t)
    return pl.pallas_call(
        _out_proj_kernel,
        grid=(t // tm, n // tn),
        in_specs=[pl.BlockSpec((tm, kh), lambda i, j: (i, a0_blk)),
                  pl.BlockSpec((tm, kh), lambda i, j: (i, a1_blk)),
                  pl.BlockSpec((None, k, tn), lambda i, j: (li, 0, j)),
                  pl.BlockSpec((tm, tn), lambda i, j: (i, j))],
        out_specs=pl.BlockSpec((tm, tn), lambda i, j: (i, j)),
        out_shape=jax.ShapeDtypeStruct((t, n), F32),
        compiler_params=_params("parallel", "arbitrary"),
        name="out_proj",
    )(a0, a1, w, res)


def _diff_attn_kernel(tbl_ref, q_ref, k_ref, v_ref, lam_ref, sg_ref, o_ref,
                      m_scr, acc_scr, bias_scr, q_scr, v_scr, *, t, lam_init):
    h = pl.program_id(1)
    qi = pl.program_id(2)
    scale = A_QK_DIM ** -0.5

    @pl.when(qi == 0)
    def _build_bias():
        v_scr[:, :A_V_DIM] = v_ref[...]
        v_scr[:, A_V_DIM:] = jnp.ones((v_scr.shape[0], A_V_DIM), BF16)
        row = lax.broadcasted_iota(I32, (t, t), 0)
        col = lax.broadcasted_iota(I32, (t, t), 1)
        for d in range(2):
            rel = row - col + d * t
            acc = jnp.full((t, t), tbl_ref[h], F32)
            for b in range(1, N_BUCKETS):
                acc = jnp.where(rel >= T5_THRESHOLDS[b], tbl_ref[b * A_HEADS + h], acc)
            if d == 0:
                acc = jnp.where(rel >= 0, acc, NEG)
            bias_scr[:, (1 - d) * t:(2 - d) * t] = acc

    m_scr[...] = jnp.full(m_scr.shape, NEG, F32)
    acc_scr[...] = jnp.zeros(acc_scr.shape, F32)
    q = q_ref[...] * scale
    lane = lax.broadcasted_iota(I32, q.shape, 1)
    q_scr[0] = jnp.where(lane < A_QK_DIM, q, jnp.zeros_like(q))
    q_scr[1] = jnp.where(lane >= A_QK_DIM, q, jnp.zeros_like(q))

    def step(ki, n_tiles, bias_cols, bias_const):
        rows = pl.ds(pl.multiple_of(ki * t, t), n_tiles * t)
        k = k_ref[rows, :]
        v_aug = v_scr[rows, :]
        for m in range(2):
            q_split = Q_SPLIT_WIDE if n_tiles > 1 else Q_SPLIT_SINGLE
            for half in range(q_split):
                hs = t // q_split
                rs = slice(half * hs, (half + 1) * hs)
                s = lax.dot_general(q_scr[m, rs, :], k, NT_DIMS, preferred_element_type=F32)
                if bias_cols is not None:
                    s = s + bias_scr[rs, bias_cols]
                m_old = m_scr[m, rs, :]
                m_new = jnp.maximum(m_old, jnp.max(s, axis=-1, keepdims=True) + bias_const)
                alpha = jnp.exp(m_old - m_new)
                p = jnp.exp((s - (m_new - bias_const)).astype(BF16))
                acc_scr[m, rs, :] = alpha * acc_scr[m, rs, :] + jnp.dot(
                    p, v_aug, preferred_element_type=F32)
                m_scr[m, rs, :] = m_new

    far_bias = tbl_ref[(N_BUCKETS - 1) * A_HEADS + h]

    n_far = jnp.maximum(qi - 1, 0)
    n_groups = n_far // FAR_TILES
    rest = n_far - n_groups * FAR_TILES

    def far(i, carry):
        step(i * FAR_TILES, FAR_TILES, None, far_bias)
        return carry

    lax.fori_loop(0, n_groups, far, 0)

    @pl.when(rest >= 2)
    def _far_pair():
        step(n_groups * FAR_TILES, 2, None, far_bias)

    @pl.when(rest % 2 == 1)
    def _far_single():
        step(n_far - 1, 1, None, far_bias)

    @pl.when(qi >= 1)
    def _near_and_diag():
        step(qi - 1, 2, slice(0, 2 * t), 0.0)

    @pl.when(qi == 0)
    def _diag_only():
        step(0, 1, slice(t, 2 * t), 0.0)

    lv = lam_ref[...]
    s1 = jnp.sum(lv[0:1] * lv[1:2], axis=-1, keepdims=True)
    s2 = jnp.sum(lv[2:3] * lv[3:4], axis=-1, keepdims=True)
    lam = jnp.exp(s1) - jnp.exp(s2) + lam_init
    a0 = acc_scr[0]
    a1 = acc_scr[1]
    o = (a0[:, :A_V_DIM] / a0[:, A_V_DIM:A_V_DIM + 1]
         - lam * (a1[:, :A_V_DIM] / a1[:, A_V_DIM:A_V_DIM + 1]))
    o = _rms(o, sg_ref[...]) * (1.0 - lam_init)
    o_ref[...] = o.astype(o_ref.dtype)


def diff_attention(proj, rel_bias, lam_vec, subln_g, lam_init, bsz, s_len, *, t=512):
    t = min(t, s_len)
    assert t >= MAX_DISTANCE and s_len % t == 0 and FAR_TILES == 4
    nq = s_len // t
    q_map = lambda b, h, qi: (b * nq + qi, h)
    kv_map = lambda off: (lambda b, h, qi: (b, off + h))
    const = lambda b, h, qi: (0, 0)
    return pl.pallas_call(
        functools.partial(_diff_attn_kernel, t=t, lam_init=lam_init),
        grid=(bsz, A_HEADS, nq),
        in_specs=[pl.BlockSpec(memory_space=pltpu.SMEM),
                  pl.BlockSpec((t, A_V_DIM), q_map),
                  pl.BlockSpec((s_len, A_V_DIM), kv_map(A_HEADS)),
                  pl.BlockSpec((s_len, A_V_DIM), kv_map(2 * A_HEADS)),
                  pl.BlockSpec((4, A_QK_DIM), const),
                  pl.BlockSpec((1, A_V_DIM), const)],
        out_specs=pl.BlockSpec((t, A_V_DIM), q_map),
        out_shape=jax.ShapeDtypeStruct((bsz * s_len, A_WIDTH), BF16),
        scratch_shapes=[pltpu.VMEM((2, t, 1), F32),
                        pltpu.VMEM((2, t, 2 * A_V_DIM), F32),
                        pltpu.VMEM((t, 2 * t), F32),
                        pltpu.VMEM((2, t, A_V_DIM), BF16),
                        pltpu.VMEM((s_len, 2 * A_V_DIM), BF16)],
        compiler_params=_params("parallel", "arbitrary", "arbitrary"),
        name="diff_attention",
    )(rel_bias.reshape(-1), proj, proj, proj, lam_vec, subln_g.reshape(1, A_V_DIM))


def _gla_kernel(q_ref, k_ref, v_ref, r_ref, gb_ref, wgk_ref, bgk_ref, gn_ref, o_ref, st_scr, *, rows):
    c_len = GLA_CHUNK
    scale = B_KEY_DIM ** -0.5

    @pl.when(pl.program_id(1) == 0)
    def _():
        st_scr[...] = jnp.zeros(st_scr.shape, F32)

    z = jnp.dot(gb_ref[...], wgk_ref[...], preferred_element_type=F32, precision=HIGHEST)
    z = z + bgk_ref[...]
    g_log = (jnp.minimum(z, 0.0) - jnp.log(1.0 + jnp.exp(-jnp.abs(z)))) * (1.0 / GATE_TEMP)
    row = lax.broadcasted_iota(I32, (rows, rows), 0)
    col = lax.broadcasted_iota(I32, (rows, rows), 1)
    ltri = ((row >= col) & ((row // c_len) == (col // c_len))).astype(F32)
    bcum = jnp.dot(ltri, g_log, preferred_element_type=F32, precision=HIGHEST)
    causal = (lax.broadcasted_iota(I32, (c_len, c_len), 0)
              >= lax.broadcasted_iota(I32, (c_len, c_len), 1))
    gn = gn_ref[...]

    for c in range(rows // c_len):
        rs = slice(c * c_len, (c + 1) * c_len)
        for hh in range(B_HEADS):
            ks = slice(hh * B_KEY_DIM, (hh + 1) * B_KEY_DIM)
            vs = slice(hh * B_VAL_DIM, (hh + 1) * B_VAL_DIM)
            b = bcum[rs, ks]
            b_last = b[c_len - 1:c_len, :]
            b_mid = b[c_len // 2:c_len // 2 + 1, :]
            q = q_ref[rs, ks].astype(F32) * scale
            k = k_ref[rs, ks].astype(F32)
            v = v_ref[rs, vs]
            st = st_scr[hh]
            q_in = (q * jnp.exp(b)).astype(BF16)
            q_ia = (q * jnp.exp(b - b_mid)).astype(BF16)
            k_ia = (k * jnp.exp(b_mid - b)).astype(BF16)
            att = lax.dot_general(q_ia, k_ia, NT_DIMS, preferred_element_type=F32)
            att = jnp.where(causal, att, 0.0)
            o = jnp.dot(att.astype(BF16), v, preferred_element_type=F32)
            o = o + lax.dot_general(q_in, st.astype(BF16), NT_DIMS, preferred_element_type=F32)
            k_st = (k * jnp.exp(b_last - b)).astype(BF16)
            st_scr[hh] = st * jnp.exp(b_last) + lax.dot_general(
                v, k_st, TN_DIMS, preferred_element_type=F32)
            on = _rms(o, gn)
            r = r_ref[rs, vs].astype(F32)
            o_ref[rs, vs] = (on * (r * _sigmoid(r))).astype(o_ref.dtype)


def gla_mixer(proj, gb, w_gk, b_gk, gnorm_g, bsz, s_len, *, rows=256):
    rows = min(rows, s_len)
    nr = s_len // rows
    kw = B_HEADS * B_KEY_DIM
    rmap = lambda blk: (lambda b, i: (b * nr + i, blk))
    return pl.pallas_call(
        functools.partial(_gla_kernel, rows=rows),
        grid=(bsz, nr),
        in_specs=[pl.BlockSpec((rows, kw), rmap(3 * A_WIDTH // kw)),
                  pl.BlockSpec((rows, kw), rmap(3 * A_WIDTH // kw + 1)),
                  pl.BlockSpec((rows, B_WIDTH), rmap((3 * A_WIDTH + 2 * kw) // B_WIDTH)),
                  pl.BlockSpec((rows, B_WIDTH), rmap((3 * A_WIDTH + 2 * kw) // B_WIDTH + 1)),
                  pl.BlockSpec((rows, GATE_RANK), rmap(0)),
                  pl.BlockSpec((GATE_RANK, kw), lambda b, i: (0, 0)),
                  pl.BlockSpec((1, kw), lambda b, i: (0, 0)),
                  pl.BlockSpec((1, B_VAL_DIM), lambda b, i: (0, 0))],
        out_specs=pl.BlockSpec((rows, B_WIDTH), rmap(0)),
        out_shape=jax.ShapeDtypeStruct((bsz * s_len, B_WIDTH), BF16),
        scratch_shapes=[pltpu.VMEM((B_HEADS, B_VAL_DIM, B_KEY_DIM), F32)],
        compiler_params=_params("parallel", "arbitrary"),
        name="gla",
    )(proj, proj, proj, proj, gb, w_gk, b_gk.reshape(1, kw), gnorm_g.reshape(1, B_VAL_DIM))


def _spatial_kernel(u_ref, v_ref, lg_ref, lb_ref, ws_ref, bs_ref, o_ref, *, rows):
    v = v_ref[...].astype(F32)
    mu = jnp.mean(v, axis=-1, keepdims=True)
    vc = v - mu
    var = jnp.mean(vc * vc, axis=-1, keepdims=True)
    vn = (vc * lax.rsqrt(var + EPS) * lg_ref[...] + lb_ref[...]).astype(BF16)
    gd = v.shape[1] // C_GROUPS
    mask = (lax.broadcasted_iota(I32, (C_CHUNK, C_CHUNK), 0)
            >= lax.broadcasted_iota(I32, (C_CHUNK, C_CHUNK), 1))
    for g in range(C_GROUPS):
        w = jnp.where(mask, ws_ref[g], 0.0).astype(BF16)
        bias = bs_ref[:, g:g + 1]
        cs = slice(g * gd, (g + 1) * gd)
        for c in range(rows // C_CHUNK):
            rs = slice(c * C_CHUNK, (c + 1) * C_CHUNK)
            z = jnp.dot(w, vn[rs, cs], preferred_element_type=F32) + bias
            o_ref[rs, cs] = (u_ref[rs, cs].astype(F32) * z).astype(o_ref.dtype)


def spatial_gate(proj, ln_g, ln_b, w_s, b_s, *, rows=512):
    t = proj.shape[0]
    cw = proj.shape[1] // 2
    rows = min(rows, t)
    return pl.pallas_call(
        functools.partial(_spatial_kernel, rows=rows),
        grid=(t // rows,),
        in_specs=[pl.BlockSpec((rows, cw), lambda i: (i, 0)),
                  pl.BlockSpec((rows, cw), lambda i: (i, 1)),
                  pl.BlockSpec((1, cw), lambda i: (0, 0)),
                  pl.BlockSpec((1, cw), lambda i: (0, 0)),
                  pl.BlockSpec((C_GROUPS, C_CHUNK, C_CHUNK), lambda i: (0, 0, 0)),
                  pl.BlockSpec((C_CHUNK, C_GROUPS), lambda i: (0, 0))],
        out_specs=pl.BlockSpec((rows, cw), lambda i: (i, 0)),
        out_shape=jax.ShapeDtypeStruct((t, cw), BF16),
        compiler_params=_params("parallel"),
        name="spatial_gate",
    )(proj, proj, ln_g.reshape(1, cw), ln_b.reshape(1, cw), w_s, b_s.T)


def _router_kernel(x_ref, g_ref, wr_ref, h_ref, ri_ref, rg_ref, cnt_ref, carry_scr, *, tr):
    @pl.when(pl.program_id(0) == 0)
    def _():
        carry_scr[...] = jnp.zeros(carry_scr.shape, F32)

    h = _rms(x_ref[...], g_ref[...])
    h_ref[...] = _pack_bf16_pairs(h)
    h_hi = h.astype(BF16)
    h_lo = (h - h_hi.astype(F32)).astype(BF16)
    w_hi = wr_ref[0]
    w_lo = wr_ref[1]
    logits = (jnp.dot(h_hi, w_hi, preferred_element_type=F32)
              + (jnp.dot(h_hi, w_lo, preferred_element_type=F32)
                 + jnp.dot(h_lo, w_hi, preferred_element_type=F32)))
    lane = lax.broadcasted_iota(I32, (tr, LANES), 1)
    lanef = lane.astype(F32)
    big = float(LANES)

    is_g = lane < N_GROUPS
    lg = jnp.where(is_g, logits, NEG)
    gmax = jnp.max(lg, axis=-1, keepdims=True)
    p_grp = 1.0 / jnp.sum(jnp.exp(lg - gmax), axis=-1, keepdims=True)
    gidx = jnp.min(jnp.where(is_g & (lg == gmax), lanef, big), axis=-1, keepdims=True)

    lo = ROUTER_LANE0 + EXPERTS_PER_GROUP * gidx
    in_e = (lanef >= lo) & (lanef < lo + EXPERTS_PER_GROUP)
    le = jnp.where(in_e, logits, NEG)
    emax = jnp.max(le, axis=-1, keepdims=True)
    i1 = jnp.min(jnp.where(in_e & (le == emax), lanef, big), axis=-1, keepdims=True)
    oh1 = lanef == i1
    le2 = jnp.where(oh1, NEG, le)
    emax2 = jnp.max(le2, axis=-1, keepdims=True)
    i2 = jnp.min(jnp.where(in_e & (le2 == emax2) & (~oh1), lanef, big), axis=-1, keepdims=True)
    oh2 = lanef == i2
    ratio = jnp.exp(emax2 - emax)
    g1 = p_grp / (1.0 + ratio)
    g2 = p_grp * ratio / (1.0 + ratio)

    onehot = (oh1 | oh2).astype(BF16)
    tri = (lax.broadcasted_iota(I32, (tr, tr), 0) > lax.broadcasted_iota(I32, (tr, tr), 1))
    before = jnp.dot(tri.astype(BF16), onehot, preferred_element_type=F32) + carry_scr[...]
    rank1 = jnp.sum(jnp.where(oh1, before, 0.0), axis=-1, keepdims=True)
    rank2 = jnp.sum(jnp.where(oh2, before, 0.0), axis=-1, keepdims=True)
    carry = carry_scr[...] + jnp.sum(onehot.astype(F32), axis=0, keepdims=True)
    carry_scr[...] = carry

    e1 = i1 - ROUTER_LANE0
    e2 = i2 - ROUTER_LANE0
    ri = jnp.where(lane == 0, e1, jnp.where(lane == 1, e2,
                                             jnp.where(lane == 2, rank1,
                                                       jnp.where(lane == 3, rank2, 0.0))))
    ri_ref[...] = ri.astype(I32)
    rg_ref[...] = jnp.where(lane == 0, g1, jnp.where(lane == 1, g2, 0.0))
    cnt_ref[...] = jnp.broadcast_to(carry, cnt_ref.shape).astype(I32)


def moe_router(x, g, w_rg, w_re, *, tr=512):
    t, d = x.shape
    tr = min(tr, t)
    w_r = jnp.zeros((d, LANES), F32)
    w_r = w_r.at[:, :N_GROUPS].set(w_rg).at[:, ROUTER_LANE0:ROUTER_LANE0 + N_EXPERTS].set(w_re)
    w_hi = w_r.astype(BF16)
    w_r = jnp.stack([w_hi, (w_r - w_hi.astype(F32)).astype(BF16)])
    return pl.pallas_call(
        functools.partial(_router_kernel, tr=tr),
        grid=(t // tr,),
        in_specs=[pl.BlockSpec((tr, d), lambda i: (i, 0)),
                  pl.BlockSpec((1, d), lambda i: (0, 0)),
                  pl.BlockSpec((2, d, LANES), lambda i: (0, 0, 0))],
        out_specs=[pl.BlockSpec((tr, d // 2), lambda i: (i, 0)),
                   pl.BlockSpec((tr, LANES), lambda i: (i, 0)),
                   pl.BlockSpec((tr, LANES), lambda i: (i, 0)),
                   pl.BlockSpec((8, LANES), lambda i: (0, 0))],
        out_shape=[jax.ShapeDtypeStruct((t, d // 2), U32),
                   jax.ShapeDtypeStruct((t, LANES), I32),
                   jax.ShapeDtypeStruct((t, LANES), F32),
                   jax.ShapeDtypeStruct((8, LANES), I32)],
        scratch_shapes=[pltpu.VMEM((1, LANES), F32)],
        compiler_params=_params("arbitrary"),
        name="moe_router",
    )(x, g.reshape(1, d), w_r)


DMA_UNROLL = 8


def _dispatch_kernel(pos_ref, zf_ref, h_ref, xs_ref, zero_scr, sem, zsem, *, td, nb):
    i = pl.program_id(0)
    base = i * td

    @pl.when(i == 0)
    def _():
        zero_scr[...] = jnp.zeros(zero_scr.shape, zero_scr.dtype)

        def zero_copy(j):
            return pltpu.make_async_copy(zero_scr, xs_ref.at[pl.ds(j * MOE_BLOCK, MOE_BLOCK)], zsem)

        def zstart(j, carry):
            @pl.when(zf_ref[j] == 1)
            def _():
                zero_copy(j).start()
            return carry

        def zwait(j, carry):
            @pl.when(zf_ref[j] == 1)
            def _():
                zero_copy(j).wait()
            return carry

        lax.fori_loop(0, nb, zstart, 0)
        lax.fori_loop(0, nb, zwait, 0)

    def issue(r, carry):
        for kk in range(TOP_K):
            pos = pos_ref[(base + r) * TOP_K + kk]
            pltpu.make_async_copy(h_ref.at[pl.ds(r, 1)], xs_ref.at[pl.ds(pos, 1)], sem).start()
        return carry

    lax.fori_loop(0, td, issue, 0, unroll=DMA_UNROLL)
    for kk in range(TOP_K):
        pltpu.make_async_copy(h_ref, xs_ref.at[pl.ds(0, td)], sem).wait()


def moe_dispatch(h, pos, zero_flag, n_rows, *, td=256):
    t, d = h.shape
    td = min(td, t)
    nb = n_rows // MOE_BLOCK
    grid_spec = pltpu.PrefetchScalarGridSpec(
        num_scalar_prefetch=2, grid=(t // td,),
        in_specs=[pl.BlockSpec((td, d), lambda i, *_: (i, 0))],
        out_specs=pl.BlockSpec(memory_space=pl.ANY),
        scratch_shapes=[pltpu.VMEM((MOE_BLOCK, d), h.dtype),
                        pltpu.SemaphoreType.DMA(()), pltpu.SemaphoreType.DMA(())])
    return pl.pallas_call(
        functools.partial(_dispatch_kernel, td=td, nb=nb),
        grid_spec=grid_spec,
        out_shape=jax.ShapeDtypeStruct((n_rows, d), h.dtype),
        compiler_params=_params("arbitrary"),
        name="moe_dispatch",
    )(pos, zero_flag, h)


def _expert_kernel(blk_e_ref, first_ref, ord_ref, next_ref, nused_ref, xs_ref, wg_hbm, wu_hbm, wd_hbm,
                   y_ref, wg_f, wu_f, wd_f, wg_s, wu_s, wd_s, sems, *, li):
    j = pl.program_id(0)
    used = j < nused_ref[0]

    def weight_copies(e, slot):
        return [pltpu.make_async_copy(src.at[li, e], dst.at[slot], sems.at[slot])
                for src, dst in ((wg_hbm, wg_f), (wu_hbm, wu_f), (wd_hbm, wd_f))]

    @pl.when(j == 0)
    def _():
        for c in weight_copies(blk_e_ref[0], 0):
            c.start()

    @pl.when(used & (first_ref[j] == 1))
    def _():
        slot = ord_ref[j] % 2
        for c in weight_copies(blk_e_ref[j], slot):
            c.wait()
        nxt = next_ref[j]

        @pl.when(nxt < N_EXPERTS)
        def _():
            for c in weight_copies(nxt, 1 - slot):
                c.start()

        wg_s[...] = wg_f[slot].astype(BF16)
        wu_s[...] = wu_f[slot].astype(BF16)
        wd_s[...] = wd_f[slot].astype(BF16)

    @pl.when(used)
    def _():
        x_lo, x_hi = _unpack_bf16_pairs(xs_ref[...])
        x_lo = x_lo.astype(BF16)
        x_hi = x_hi.astype(BF16)
        dh = x_lo.shape[1]

        def proj(w_s):
            return (jnp.dot(x_lo, w_s[:dh, :], preferred_element_type=F32)
                    + jnp.dot(x_hi, w_s[dh:, :], preferred_element_type=F32))

        gate = proj(wg_s)
        up = proj(wu_s)
        act = (gate * _sigmoid(gate) * up).astype(BF16)
        y_ref[...] = _pack_bf16_pairs(jnp.dot(act, wd_s[...], preferred_element_type=F32))

    @pl.when(jnp.logical_not(used))
    def _():
        y_ref[...] = jnp.zeros(y_ref.shape, y_ref.dtype)


def moe_experts(xs, blk_e, blk_first, blk_ord, blk_next, n_used, w_gate, w_up, w_down, li):
    n_rows, dp = xs.shape
    d = w_gate.shape[2]
    ff = w_gate.shape[3]
    nb = n_rows // MOE_BLOCK
    grid_spec = pltpu.PrefetchScalarGridSpec(
        num_scalar_prefetch=5, grid=(nb,),
        in_specs=[pl.BlockSpec((MOE_BLOCK, dp), lambda j, be, bf, bo, bn, nu: (jnp.minimum(j, nu[0] - 1), 0)),
                  pl.BlockSpec(memory_space=pl.ANY),
                  pl.BlockSpec(memory_space=pl.ANY),
                  pl.BlockSpec(memory_space=pl.ANY)],
        out_specs=pl.BlockSpec((MOE_BLOCK, dp), lambda j, *_: (j, 0)),
        scratch_shapes=[pltpu.VMEM((2, d, ff), F32), pltpu.VMEM((2, d, ff), F32),
                        pltpu.VMEM((2, ff, d), F32),
                        pltpu.VMEM((d, ff), BF16), pltpu.VMEM((d, ff), BF16),
                        pltpu.VMEM((ff, d), BF16),
                        pltpu.SemaphoreType.DMA((2,))])
    return pl.pallas_call(
        functools.partial(_expert_kernel, li=li),
        grid_spec=grid_spec,
        out_shape=jax.ShapeDtypeStruct((n_rows, dp), U32),
        compiler_params=_params("arbitrary"),
        name="moe_experts",
    )(blk_e, blk_first, blk_ord, blk_next, n_used, xs, w_gate, w_up, w_down)


def _combine_kernel(pos_ref, x_ref, rg_ref, gf_ref, y_ref, o_ref, ybuf, sems, *, tc, final_norm):
    i = pl.program_id(0)
    slot = i % 2

    def issue_tile(tile, dst_slot):
        base = tile * tc

        def issue(r, carry):
            for kk in range(TOP_K):
                pos = pos_ref[(base + r) * TOP_K + kk]
                pltpu.make_async_copy(y_ref.at[pl.ds(pos, 1)], ybuf.at[dst_slot, kk, pl.ds(r, 1)],
                                      sems.at[dst_slot]).start()
            return carry

        lax.fori_loop(0, tc, issue, 0, unroll=DMA_UNROLL)

    @pl.when(i == 0)
    def _():
        issue_tile(0, 0)

    @pl.when(i + 1 < pl.num_programs(0))
    def _():
        issue_tile(i + 1, 1 - slot)

    for kk in range(TOP_K):
        pltpu.make_async_copy(y_ref.at[pl.ds(0, tc)], ybuf.at[slot, kk], sems.at[slot]).wait()
    gates = rg_ref[...]
    g0 = gates[:, 0:1]
    g1 = gates[:, 1:2]
    y0_lo, y0_hi = _unpack_bf16_pairs(ybuf[slot, 0])
    y1_lo, y1_hi = _unpack_bf16_pairs(ybuf[slot, 1])
    dh = y0_lo.shape[1]
    out_lo = x_ref[:, :dh] + (g0 * y0_lo + g1 * y1_lo)
    out_hi = x_ref[:, dh:] + (g0 * y0_hi + g1 * y1_hi)
    if final_norm:
        ms = (jnp.sum(out_lo * out_lo, axis=-1, keepdims=True)
              + jnp.sum(out_hi * out_hi, axis=-1, keepdims=True)) * (1.0 / (2 * dh))
        inv = lax.rsqrt(ms + EPS)
        out_lo = out_lo * inv * gf_ref[:, :dh]
        out_hi = out_hi * inv * gf_ref[:, dh:]
    o_ref[:, :dh] = out_lo
    o_ref[:, dh:] = out_hi


def moe_combine(x, y, gates, pos, g_final, final_norm, *, tc=256):
    t, d = x.shape
    tc = min(tc, t)
    grid_spec = pltpu.PrefetchScalarGridSpec(
        num_scalar_prefetch=1, grid=(t // tc,),
        in_specs=[pl.BlockSpec((tc, d), lambda i, *_: (i, 0)),
                  pl.BlockSpec((tc, LANES), lambda i, *_: (i, 0)),
                  pl.BlockSpec((1, d), lambda i, *_: (0, 0)),
                  pl.BlockSpec(memory_space=pl.ANY)],
        out_specs=pl.BlockSpec((tc, d), lambda i, *_: (i, 0)),
        scratch_shapes=[pltpu.VMEM((2, TOP_K, tc, d // 2), U32), pltpu.SemaphoreType.DMA((2,))])
    return pl.pallas_call(
        functools.partial(_combine_kernel, tc=tc, final_norm=final_norm),
        grid_spec=grid_spec,
        out_shape=jax.ShapeDtypeStruct((t, d), F32),
        compiler_params=_params("arbitrary"),
        name="moe_combine",
    )(pos, x, gates, g_final.reshape(1, d), y)


def moe_layer(x, g_norm, w_rg, w_re, w_gate, w_up, w_down, li, g_final, final_norm):
    t, d = x.shape
    h, ri, gates, cnt = moe_router(x, g_norm, w_rg, w_re)
    e_flat = ri[:, 0:TOP_K].reshape(-1)
    rank_flat = ri[:, TOP_K:2 * TOP_K].reshape(-1)
    counts = cnt[0, ROUTER_LANE0:ROUTER_LANE0 + N_EXPERTS]
    nblk = (counts + MOE_BLOCK - 1) // MOE_BLOCK
    blk_end = jnp.cumsum(nblk)
    pad_start = ((blk_end - nblk) * MOE_BLOCK).astype(I32)
    n_used = blk_end[-1:].astype(I32)
    nb = (t * TOP_K) // MOE_BLOCK + N_EXPERTS
    blk_ids = jnp.arange(nb, dtype=I32)
    blk = jnp.minimum(blk_ids, n_used[0] - 1)
    blk_e = jnp.sum(blk_end[None, :] <= blk[:, None], axis=1).astype(I32)
    is_last = (blk_ids + 1 == blk_end[blk_e]) | (blk_ids >= n_used[0])
    zero_flag = is_last.astype(I32)
    pos = (pad_start[e_flat] + rank_flat).astype(I32)
    prev_e = jnp.concatenate([jnp.full((1,), -1, I32), blk_e[:-1]])
    blk_first = ((blk_e != prev_e) & (blk_ids < n_used[0])).astype(I32)
    blk_ord = (jnp.cumsum(blk_first) - 1).astype(I32)
    e_ids = jnp.arange(N_EXPERTS, dtype=I32)
    nonempty_id = jnp.where(nblk > 0, e_ids, N_EXPERTS)
    next_ge = lax.cummin(nonempty_id, reverse=True)
    next_after = jnp.concatenate([next_ge[1:], jnp.full((1,), N_EXPERTS, I32)])
    blk_next = next_after[blk_e].astype(I32)
    xs = moe_dispatch(h, pos, zero_flag, nb * MOE_BLOCK)
    y = moe_experts(xs, blk_e, blk_first, blk_ord, blk_next, n_used, w_gate, w_up, w_down, li)
    return moe_combine(x, y, gates, pos, g_final, final_norm)


def kernel(x, rel_bias, norm_mix, norm_ffn, norm_final, w_in_ab, lam_ab, subln_ab, w_gk_ab, b_gk_ab,
           gnorm_ab, w_out_ab, w_in_c, ln_v_g, ln_v_b, w_spatial, b_spatial, w_out_c,
           w_router_group, w_router_expert, w_exp_gate, w_exp_up, w_exp_down):
    bsz, s_len, d = x.shape
    depth = norm_mix.shape[0]
    xf = x.reshape(bsz * s_len, d)
    for layer in range(depth):
        i = layer // 2
        if layer % 2 == 0:
            lam_init = 0.8 - 0.6 * math.exp(-0.3 * layer)
            n_main = 3 * A_WIDTH + 2 * B_HEADS * B_KEY_DIM + 2 * B_WIDTH
            proj, gb = norm_proj(xf, norm_mix[layer], w_in_ab, i, n_main)
            o_a = diff_attention(proj, rel_bias, lam_ab[i], subln_ab[i], lam_init, bsz, s_len)
            o_b = gla_mixer(proj, gb, w_gk_ab[i], b_gk_ab[i], gnorm_ab[i], bsz, s_len)
            xf = out_proj(o_a, 0, o_b, 0, w_out_ab, i, xf)
        else:
            proj = norm_proj(xf, norm_mix[layer], w_in_c, i, w_in_c.shape[2], act="gelu")
            uz = spatial_gate(proj, ln_v_g[i], ln_v_b[i], w_spatial[i], b_spatial[i])
            xf = out_proj(uz, 0, uz, 1, w_out_c, i, xf)
        xf = moe_layer(xf, norm_ffn[layer], w_router_group[layer], w_router_expert[layer],
                       w_exp_gate, w_exp_up, w_exp_down, layer,
                       norm_final, layer == depth - 1)
    return xf.reshape(bsz, s_len, d)
```

```python
import functools
import math

import jax
import jax.numpy as jnp
import numpy as np
from jax import lax
from jax.experimental import pallas as pl
from jax.experimental.pallas import tpu as pltpu

F32 = jnp.float32
BF16 = jnp.bfloat16
I32 = jnp.int32
U32 = jnp.uint32
HIGHEST = lax.Precision.HIGHEST

EPS = 1e-6
NEG = -1e30
A_HEADS = 8
A_QK_DIM = 64
A_V_DIM = 128
A_WIDTH = A_HEADS * A_V_DIM
N_BUCKETS = 32
MAX_EXACT = 16
MAX_DISTANCE = 128
B_HEADS = 4
B_KEY_DIM = 128
B_VAL_DIM = 256
B_WIDTH = B_HEADS * B_VAL_DIM
GATE_RANK = 16
GATE_TEMP = 16.0
GLA_CHUNK = 64
C_GROUPS = 8
C_CHUNK = 128
N_GROUPS = 4
EXPERTS_PER_GROUP = 8
N_EXPERTS = N_GROUPS * EXPERTS_PER_GROUP
TOP_K = 2
LANES = 128
ROUTER_LANE0 = N_GROUPS
MOE_BLOCK = 256
FAR_TILES = 4
Q_SPLIT_SINGLE = 2
Q_SPLIT_WIDE = 4
VMEM_LIMIT = 56 * 1024 * 1024

NT_DIMS = (((1,), (1,)), ((), ()))
TN_DIMS = (((0,), (0,)), ((), ()))


def _t5_thresholds():
    n = np.arange(0, 2 * MAX_DISTANCE)
    nf = np.maximum(n, 1).astype(np.float64)
    large = MAX_EXACT + (np.log(nf / MAX_EXACT) / math.log(MAX_DISTANCE / MAX_EXACT)
                         * (N_BUCKETS - MAX_EXACT)).astype(np.int64)
    bucket = np.where(n < MAX_EXACT, n, np.minimum(large, N_BUCKETS - 1))
    return [int(np.min(n[bucket >= b])) for b in range(N_BUCKETS)]


T5_THRESHOLDS = _t5_thresholds()


def _params(*sem):
    return pltpu.CompilerParams(dimension_semantics=sem, vmem_limit_bytes=VMEM_LIMIT)


def _rms(x, g):
    ms = jnp.mean(x * x, axis=-1, keepdims=True)
    return x * lax.rsqrt(ms + EPS) * g


def _gelu_tanh(x):
    c = math.sqrt(2.0 / math.pi)
    return 0.5 * x * (1.0 + jnp.tanh(c * (x + 0.044715 * (x * x * x))))


def _sigmoid(x):
    return 1.0 / (1.0 + jnp.exp(-x))


def _pack_bf16_pairs(x):
    c = x.shape[1] // 2
    bits = lax.bitcast_convert_type(x.astype(BF16).astype(F32), U32)
    return (bits[:, :c] >> 16) | (bits[:, c:] & jnp.uint32(0xFFFF0000))


def _unpack_bf16_pairs(w):
    lo = lax.bitcast_convert_type(w << 16, F32)
    hi = lax.bitcast_convert_type(w & jnp.uint32(0xFFFF0000), F32)
    return lo, hi


def _norm_proj_kernel(*refs, act, has_extra):
    if has_extra:
        x_ref, g_ref, w_ref, we_ref, o_ref, oe_ref, h_scr = refs
    else:
        x_ref, g_ref, w_ref, o_ref, h_scr = refs

    @pl.when(pl.program_id(1) == 0)
    def _():
        hb = _rms(x_ref[...], g_ref[...]).astype(BF16)
        h_scr[...] = hb
        if has_extra:
            ye = lax.dot_general(hb, we_ref[...].astype(BF16), NT_DIMS, preferred_element_type=F32)
            oe_ref[...] = ye[:, :oe_ref.shape[1]]

    if has_extra:
        y = lax.dot_general(h_scr[...], w_ref[...].astype(BF16), NT_DIMS,
                            preferred_element_type=F32)
    else:
        y = jnp.dot(h_scr[...], w_ref[...].astype(BF16), preferred_element_type=F32)
    if act == "gelu":
        y = _gelu_tanh(y)
    o_ref[...] = y.astype(o_ref.dtype)


def norm_proj(x, g, w, li, n_cols, *, act=None, tm=1024, tn=1024):
    t, d = x.shape
    tm = min(tm, t)
    n_extra = w.shape[2] - n_cols
    assert n_cols % tn == 0 and n_cols % LANES == 0 and 0 <= n_extra <= LANES
    grid = (t // tm, n_cols // tn)
    in_specs = [pl.BlockSpec((tm, d), lambda i, j: (i, 0)),
                pl.BlockSpec((1, d), lambda i, j: (0, 0))]
    out_specs = [pl.BlockSpec((tm, tn), lambda i, j: (i, j))]
    out_shape = [jax.ShapeDtypeStruct((t, n_cols), BF16)]
    args = [x, g.reshape(1, d)]
    if n_extra:
        wt = jnp.swapaxes(w, 1, 2)
        in_specs.append(pl.BlockSpec((None, tn, d), lambda i, j: (li, j, 0)))
        in_specs.append(pl.BlockSpec((None, LANES, d), lambda i, j: (li, n_cols // LANES, 0)))
        out_specs.append(pl.BlockSpec((tm, n_extra), lambda i, j: (i, 0)))
        out_shape.append(jax.ShapeDtypeStruct((t, n_extra), F32))
        args += [wt, wt]
    else:
        in_specs.append(pl.BlockSpec((None, d, tn), lambda i, j: (li, 0, j)))
        args.append(w)
    outs = pl.pallas_call(
        functools.partial(_norm_proj_kernel, act=act, has_extra=bool(n_extra)),
        grid=grid, in_specs=in_specs, out_specs=out_specs, out_shape=out_shape,
        scratch_shapes=[pltpu.VMEM((tm, d), BF16)],
        compiler_params=_params("parallel", "arbitrary"),
        name="norm_proj_" + (act or "lin"),
    )(*args)
    return outs if n_extra else outs[0]


def _out_proj_kernel(a0_ref, a1_ref, w_ref, r_ref, o_ref):
    kh = a0_ref.shape[1]
    w = w_ref[...].astype(BF16)
    y = jnp.dot(a0_ref[...], w[:kh], preferred_element_type=F32)
    y = y + jnp.dot(a1_ref[...], w[kh:], preferred_element_type=F32)
    o_ref[...] = r_ref[...] + y


def out_proj(a0, a0_blk, a1, a1_blk, w, li, res, *, tm=1024, tn=1024):
    t, n = res.shape
    k = w.shape[1]
    kh = k // 2
    tm = min(tm, t)
    return pl.pallas_call(
        _out_proj_kernel,
        grid=(t // tm, n // tn),
        in_specs=[pl.BlockSpec((tm, kh), lambda i, j: (i, a0_blk)),
                  pl.BlockSpec((tm, kh), lambda i, j: (i, a1_blk)),
                  pl.BlockSpec((None, k, tn), lambda i, j: (li, 0, j)),
                  pl.BlockSpec((tm, tn), lambda i, j: (i, j))],
        out_specs=pl.BlockSpec((tm, tn), lambda i, j: (i, j)),
        out_shape=jax.ShapeDtypeStruct((t, n), F32),
        compiler_params=_params("parallel", "arbitrary"),
        name="out_proj",
    )(a0, a1, w, res)


def _diff_attn_kernel(tbl_ref, q_ref, k_ref, v_ref, lam_ref, sg_ref, o_ref,
                      m_scr, acc_scr, bias_scr, q_scr, v_scr, *, t, lam_init):
    h = pl.program_id(1)
    qi = pl.program_id(2)
    scale = A_QK_DIM ** -0.5

    @pl.when(qi == 0)
    def _build_bias():
        v_scr[:, :A_V_DIM] = v_ref[...]
        v_scr[:, A_V_DIM:] = jnp.ones((v_scr.shape[0], A_V_DIM), BF16)
        row = lax.broadcasted_iota(I32, (t, t), 0)
        col = lax.broadcasted_iota(I32, (t, t), 1)
        for d in range(2):
            rel = row - col + d * t
            acc = jnp.full((t, t), tbl_ref[h], F32)
            for b in range(1, N_BUCKETS):
                acc = jnp.where(rel >= T5_THRESHOLDS[b], tbl_ref[b * A_HEADS + h], acc)
            if d == 0:
                acc = jnp.where(rel >= 0, acc, NEG)
            bias_scr[:, (1 - d) * t:(2 - d) * t] = acc

    m_scr[...] = jnp.full(m_scr.shape, NEG, F32)
    acc_scr[...] = jnp.zeros(acc_scr.shape, F32)
    q = q_ref[...] * scale
    lane = lax.broadcasted_iota(I32, q.shape, 1)
    q_scr[0] = jnp.where(lane < A_QK_DIM, q, jnp.zeros_like(q))
    q_scr[1] = jnp.where(lane >= A_QK_DIM, q, jnp.zeros_like(q))

    def step(ki, n_tiles, bias_cols, bias_const):
        rows = pl.ds(pl.multiple_of(ki * t, t), n_tiles * t)
        k = k_ref[rows, :]
        v_aug = v_scr[rows, :]
        for m in range(2):
            q_split = Q_SPLIT_WIDE if n_tiles > 1 else Q_SPLIT_SINGLE
            for half in range(q_split):
                hs = t // q_split
                rs = slice(half * hs, (half + 1) * hs)
                s = lax.dot_general(q_scr[m, rs, :], k, NT_DIMS, preferred_element_type=F32)
                if bias_cols is not None:
                    s = s + bias_scr[rs, bias_cols]
                m_old = m_scr[m, rs, :]
                m_new = jnp.maximum(m_old, jnp.max(s, axis=-1, keepdims=True) + bias_const)
                alpha = jnp.exp(m_old - m_new)
                p = jnp.exp((s - (m_new - bias_const)).astype(BF16))
                acc_scr[m, rs, :] = alpha * acc_scr[m, rs, :] + jnp.dot(
                    p, v_aug, preferred_element_type=F32)
                m_scr[m, rs, :] = m_new

    far_bias = tbl_ref[(N_BUCKETS - 1) * A_HEADS + h]

    n_far = jnp.maximum(qi - 1, 0)
    n_groups = n_far // FAR_TILES
    rest = n_far - n_groups * FAR_TILES

    def far(i, carry):
        step(i * FAR_TILES, FAR_TILES, None, far_bias)
        return carry

    lax.fori_loop(0, n_groups, far, 0)

    @pl.when(rest >= 2)
    def _far_pair():
        step(n_groups * FAR_TILES, 2, None, far_bias)

    @pl.when(rest % 2 == 1)
    def _far_single():
        step(n_far - 1, 1, None, far_bias)

    @pl.when(qi >= 1)
    def _near_and_diag():
        step(qi - 1, 2, slice(0, 2 * t), 0.0)

    @pl.when(qi == 0)
    def _diag_only():
        step(0, 1, slice(t, 2 * t), 0.0)

    lv = lam_ref[...]
    s1 = jnp.sum(lv[0:1] * lv[1:2], axis=-1, keepdims=True)
    s2 = jnp.sum(lv[2:3] * lv[3:4], axis=-1, keepdims=True)
    lam = jnp.exp(s1) - jnp.exp(s2) + lam_init
    a0 = acc_scr[0]
    a1 = acc_scr[1]
    o = (a0[:, :A_V_DIM] / a0[:, A_V_DIM:A_V_DIM + 1]
         - lam * (a1[:, :A_V_DIM] / a1[:, A_V_DIM:A_V_DIM + 1]))
    o = _rms(o, sg_ref[...]) * (1.0 - lam_init)
    o_ref[...] = o.astype(o_ref.dtype)


def diff_attention(proj, rel_bias, lam_vec, subln_g, lam_init, bsz, s_len, *, t=512):
    t = min(t, s_len)
    assert t >= MAX_DISTANCE and s_len % t == 0 and FAR_TILES == 4
    nq = s_len // t
    q_map = lambda b, h, qi: (b * nq + qi, h)
    kv_map = lambda off: (lambda b, h, qi: (b, off + h))
    const = lambda b, h, qi: (0, 0)
    return pl.pallas_call(
        functools.partial(_diff_attn_kernel, t=t, lam_init=lam_init),
        grid=(bsz, A_HEADS, nq),
        in_specs=[pl.BlockSpec(memory_space=pltpu.SMEM),
                  pl.BlockSpec((t, A_V_DIM), q_map),
                  pl.BlockSpec((s_len, A_V_DIM), kv_map(A_HEADS)),
                  pl.BlockSpec((s_len, A_V_DIM), kv_map(2 * A_HEADS)),
                  pl.BlockSpec((4, A_QK_DIM), const),
                  pl.BlockSpec((1, A_V_DIM), const)],
        out_specs=pl.BlockSpec((t, A_V_DIM), q_map),
        out_shape=jax.ShapeDtypeStruct((bsz * s_len, A_WIDTH), BF16),
        scratch_shapes=[pltpu.VMEM((2, t, 1), F32),
                        pltpu.VMEM((2, t, 2 * A_V_DIM), F32),
                        pltpu.VMEM((t, 2 * t), F32),
                        pltpu.VMEM((2, t, A_V_DIM), BF16),
                        pltpu.VMEM((s_len, 2 * A_V_DIM), BF16)],
        compiler_params=_params("parallel", "arbitrary", "arbitrary"),
        name="diff_attention",
    )(rel_bias.reshape(-1), proj, proj, proj, lam_vec, subln_g.reshape(1, A_V_DIM))


def _gla_kernel(q_ref, k_ref, v_ref, r_ref, gb_ref, wgk_ref, bgk_ref, gn_ref, o_ref, st_scr, *, rows):
    c_len = GLA_CHUNK
    scale = B_KEY_DIM ** -0.5

    @pl.when(pl.program_id(1) == 0)
    def _():
        st_scr[...] = jnp.zeros(st_scr.shape, F32)

    z = jnp.dot(gb_ref[...], wgk_ref[...], preferred_element_type=F32, precision=HIGHEST)
    z = z + bgk_ref[...]
    g_log = (jnp.minimum(z, 0.0) - jnp.log(1.0 + jnp.exp(-jnp.abs(z)))) * (1.0 / GATE_TEMP)
    row = lax.broadcasted_iota(I32, (rows, rows), 0)
    col = lax.broadcasted_iota(I32, (rows, rows), 1)
    ltri = ((row >= col) & ((row // c_len) == (col // c_len))).astype(F32)
    bcum = jnp.dot(ltri, g_log, preferred_element_type=F32, precision=HIGHEST)
    causal = (lax.broadcasted_iota(I32, (c_len, c_len), 0)
              >= lax.broadcasted_iota(I32, (c_len, c_len), 1))
    gn = gn_ref[...]

    for c in range(rows // c_len):
        rs = slice(c * c_len, (c + 1) * c_len)
        for hh in range(B_HEADS):
            ks = slice(hh * B_KEY_DIM, (hh + 1) * B_KEY_DIM)
            vs = slice(hh * B_VAL_DIM, (hh + 1) * B_VAL_DIM)
            b = bcum[rs, ks]
            b_last = b[c_len - 1:c_len, :]
            b_mid = b[c_len // 2:c_len // 2 + 1, :]
            q = q_ref[rs, ks].astype(F32) * scale
            k = k_ref[rs, ks].astype(F32)
            v = v_ref[rs, vs]
            st = st_scr[hh]
            q_in = (q * jnp.exp(b)).astype(BF16)
            q_ia = (q * jnp.exp(b - b_mid)).astype(BF16)
            k_ia = (k * jnp.exp(b_mid - b)).astype(BF16)
            att = lax.dot_general(q_ia, k_ia, NT_DIMS, preferred_element_type=F32)
            att = jnp.where(causal, att, 0.0)
            o = jnp.dot(att.astype(BF16), v, preferred_element_type=F32)
            o = o + lax.dot_general(q_in, st.astype(BF16), NT_DIMS, preferred_element_type=F32)
            k_st = (k * jnp.exp(b_last - b)).astype(BF16)
            st_scr[hh] = st * jnp.exp(b_last) + lax.dot_general(
                v, k_st, TN_DIMS, preferred_element_type=F32)
            on = _rms(o, gn)
            r = r_ref[rs, vs].astype(F32)
            o_ref[rs, vs] = (on * (r * _sigmoid(r))).astype(o_ref.dtype)


def gla_mixer(proj, gb, w_gk, b_gk, gnorm_g, bsz, s_len, *, rows=256):
    rows = min(rows, s_len)
    nr = s_len // rows
    kw = B_HEADS * B_KEY_DIM
    rmap = lambda blk: (lambda b, i: (b * nr + i, blk))
    return pl.pallas_call(
        functools.partial(_gla_kernel, rows=rows),
        grid=(bsz, nr),
        in_specs=[pl.BlockSpec((rows, kw), rmap(3 * A_WIDTH // kw)),
                  pl.BlockSpec((rows, kw), rmap(3 * A_WIDTH // kw + 1)),
                  pl.BlockSpec((rows, B_WIDTH), rmap((3 * A_WIDTH + 2 * kw) // B_WIDTH)),
                  pl.BlockSpec((rows, B_WIDTH), rmap((3 * A_WIDTH + 2 * kw) // B_WIDTH + 1)),
                  pl.BlockSpec((rows, GATE_RANK), rmap(0)),
                  pl.BlockSpec((GATE_RANK, kw), lambda b, i: (0, 0)),
                  pl.BlockSpec((1, kw), lambda b, i: (0, 0)),
                  pl.BlockSpec((1, B_VAL_DIM), lambda b, i: (0, 0))],
        out_specs=pl.BlockSpec((rows, B_WIDTH), rmap(0)),
        out_shape=jax.ShapeDtypeStruct((bsz * s_len, B_WIDTH), BF16),
        scratch_shapes=[pltpu.VMEM((B_HEADS, B_VAL_DIM, B_KEY_DIM), F32)],
        compiler_params=_params("parallel", "arbitrary"),
        name="gla",
    )(proj, proj, proj, proj, gb, w_gk, b_gk.reshape(1, kw), gnorm_g.reshape(1, B_VAL_DIM))


def _spatial_kernel(u_ref, v_ref, lg_ref, lb_ref, ws_ref, bs_ref, o_ref, *, rows):
    v = v_ref[...].astype(F32)
    mu = jnp.mean(v, axis=-1, keepdims=True)
    vc = v - mu
    var = jnp.mean(vc * vc, axis=-1, keepdims=True)
    vn = (vc * lax.rsqrt(var + EPS) * lg_ref[...] + lb_ref[...]).astype(BF16)
    gd = v.shape[1] // C_GROUPS
    mask = (lax.broadcasted_iota(I32, (C_CHUNK, C_CHUNK), 0)
            >= lax.broadcasted_iota(I32, (C_CHUNK, C_CHUNK), 1))
    for g in range(C_GROUPS):
        w = jnp.where(mask, ws_ref[g], 0.0).astype(BF16)
        bias = bs_ref[:, g:g + 1]
        cs = slice(g * gd, (g + 1) * gd)
        for c in range(rows // C_CHUNK):
            rs = slice(c * C_CHUNK, (c + 1) * C_CHUNK)
            z = jnp.dot(w, vn[rs, cs], preferred_element_type=F32) + bias
            o_ref[rs, cs] = (u_ref[rs, cs].astype(F32) * z).astype(o_ref.dtype)


def spatial_gate(proj, ln_g, ln_b, w_s, b_s, *, rows=512):
    t = proj.shape[0]
    cw = proj.shape[1] // 2
    rows = min(rows, t)
    return pl.pallas_call(
        functools.partial(_spatial_kernel, rows=rows),
        grid=(t // rows,),
        in_specs=[pl.BlockSpec((rows, cw), lambda i: (i, 0)),
                  pl.BlockSpec((rows, cw), lambda i: (i, 1)),
                  pl.BlockSpec((1, cw), lambda i: (0, 0)),
                  pl.BlockSpec((1, cw), lambda i: (0, 0)),
                  pl.BlockSpec((C_GROUPS, C_CHUNK, C_CHUNK), lambda i: (0, 0, 0)),
                  pl.BlockSpec((C_CHUNK, C_GROUPS), lambda i: (0, 0))],
        out_specs=pl.BlockSpec((rows, cw), lambda i: (i, 0)),
        out_shape=jax.ShapeDtypeStruct((t, cw), BF16),
        compiler_params=_params("parallel"),
        name="spatial_gate",
    )(proj, proj, ln_g.reshape(1, cw), ln_b.reshape(1, cw), w_s, b_s.T)


def _router_kernel(x_ref, g_ref, wr_ref, h_ref, ri_ref, rg_ref, cnt_ref, carry_scr, *, tr):
    @pl.when(pl.program_id(0) == 0)
    def _():
        carry_scr[...] = jnp.zeros(carry_scr.shape, F32)

    h = _rms(x_ref[...], g_ref[...])
    h_ref[...] = _pack_bf16_pairs(h)
    h_hi = h.astype(BF16)
    h_lo = (h - h_hi.astype(F32)).astype(BF16)
    w_hi = wr_ref[0]
    w_lo = wr_ref[1]
    logits = (jnp.dot(h_hi, w_hi, preferred_element_type=F32)
              + (jnp.dot(h_hi, w_lo, preferred_element_type=F32)
                 + jnp.dot(h_lo, w_hi, preferred_element_type=F32)))
    lane = lax.broadcasted_iota(I32, (tr, LANES), 1)
    lanef = lane.astype(F32)
    big = float(LANES)

    is_g = lane < N_GROUPS
    lg = jnp.where(is_g, logits, NEG)
    gmax = jnp.max(lg, axis=-1, keepdims=True)
    p_grp = 1.0 / jnp.sum(jnp.exp(lg - gmax), axis=-1, keepdims=True)
    gidx = jnp.min(jnp.where(is_g & (lg == gmax), lanef, big), axis=-1, keepdims=True)

    lo = ROUTER_LANE0 + EXPERTS_PER_GROUP * gidx
    in_e = (lanef >= lo) & (lanef < lo + EXPERTS_PER_GROUP)
    le = jnp.where(in_e, logits, NEG)
    emax = jnp.max(le, axis=-1, keepdims=True)
    i1 = jnp.min(jnp.where(in_e & (le == emax), lanef, big), axis=-1, keepdims=True)
    oh1 = lanef == i1
    le2 = jnp.where(oh1, NEG, le)
    emax2 = jnp.max(le2, axis=-1, keepdims=True)
    i2 = jnp.min(jnp.where(in_e & (le2 == emax2) & (~oh1), lanef, big), axis=-1, keepdims=True)
    oh2 = lanef == i2
    ratio = jnp.exp(emax2 - emax)
    g1 = p_grp / (1.0 + ratio)
    g2 = p_grp * ratio / (1.0 + ratio)

    onehot = (oh1 | oh2).astype(BF16)
    tri = (lax.broadcasted_iota(I32, (tr, tr), 0) > lax.broadcasted_iota(I32, (tr, tr), 1))
    before = jnp.dot(tri.astype(BF16), onehot, preferred_element_type=F32) + carry_scr[...]
    rank1 = jnp.sum(jnp.where(oh1, before, 0.0), axis=-1, keepdims=True)
    rank2 = jnp.sum(jnp.where(oh2, before, 0.0), axis=-1, keepdims=True)
    carry = carry_scr[...] + jnp.sum(onehot.astype(F32), axis=0, keepdims=True)
    carry_scr[...] = carry

    e1 = i1 - ROUTER_LANE0
    e2 = i2 - ROUTER_LANE0
    ri = jnp.where(lane == 0, e1, jnp.where(lane == 1, e2,
                                             jnp.where(lane == 2, rank1,
                                                       jnp.where(lane == 3, rank2, 0.0))))
    ri_ref[...] = ri.astype(I32)
    rg_ref[...] = jnp.where(lane == 0, g1, jnp.where(lane == 1, g2, 0.0))
    cnt_ref[...] = jnp.broadcast_to(carry, cnt_ref.shape).astype(I32)


def moe_router(x, g, w_rg, w_re, *, tr=512):
    t, d = x.shape
    tr = min(tr, t)
    w_r = jnp.pad(jnp.concatenate([w_rg, w_re], axis=1),
                  ((0, 0), (0, LANES - N_GROUPS - N_EXPERTS)))
    w_hi = w_r.astype(BF16)
    w_r = jnp.stack([w_hi, (w_r - w_hi.astype(F32)).astype(BF16)])
    return pl.pallas_call(
        functools.partial(_router_kernel, tr=tr),
        grid=(t // tr,),
        in_specs=[pl.BlockSpec((tr, d), lambda i: (i, 0)),
                  pl.BlockSpec((1, d), lambda i: (0, 0)),
                  pl.BlockSpec((2, d, LANES), lambda i: (0, 0, 0))],
        out_specs=[pl.BlockSpec((tr, d // 2), lambda i: (i, 0)),
                   pl.BlockSpec((tr, LANES), lambda i: (i, 0)),
                   pl.BlockSpec((tr, LANES), lambda i: (i, 0)),
                   pl.BlockSpec((8, LANES), lambda i: (0, 0))],
        out_shape=[jax.ShapeDtypeStruct((t, d // 2), U32),
                   jax.ShapeDtypeStruct((t, LANES), I32),
                   jax.ShapeDtypeStruct((t, LANES), F32),
                   jax.ShapeDtypeStruct((8, LANES), I32)],
        scratch_shapes=[pltpu.VMEM((1, LANES), F32)],
        compiler_params=_params("arbitrary"),
        name="moe_router",
    )(x, g.reshape(1, d), w_r)


DMA_UNROLL = 8


def _dispatch_kernel(pos_ref, zf_ref, h_ref, xs_ref, zero_scr, sem, zsem, *, td, nb):
    i = pl.program_id(0)
    base = i * td

    @pl.when(i == 0)
    def _():
        zero_scr[...] = jnp.zeros(zero_scr.shape, zero_scr.dtype)

        def zero_copy(j):
            return pltpu.make_async_copy(zero_scr, xs_ref.at[pl.ds(j * MOE_BLOCK, MOE_BLOCK)], zsem)

        def zstart(j, carry):
            @pl.when(zf_ref[j] == 1)
            def _():
                zero_copy(j).start()
            return carry

        def zwait(j, carry):
            @pl.when(zf_ref[j] == 1)
            def _():
                zero_copy(j).wait()
            return carry

        lax.fori_loop(0, nb, zstart, 0)
        lax.fori_loop(0, nb, zwait, 0)

    def issue(r, carry):
        for kk in range(TOP_K):
            pos = pos_ref[(base + r) * TOP_K + kk]
            pltpu.make_async_copy(h_ref.at[pl.ds(r, 1)], xs_ref.at[pl.ds(pos, 1)], sem).start()
        return carry

    lax.fori_loop(0, td, issue, 0, unroll=DMA_UNROLL)
    for kk in range(TOP_K):
        pltpu.make_async_copy(h_ref, xs_ref.at[pl.ds(0, td)], sem).wait()


def moe_dispatch(h, pos, zero_flag, n_rows, *, td=512):
    t, d = h.shape
    td = min(td, t)
    nb = n_rows // MOE_BLOCK
    grid_spec = pltpu.PrefetchScalarGridSpec(
        num_scalar_prefetch=2, grid=(t // td,),
        in_specs=[pl.BlockSpec((td, d), lambda i, *_: (i, 0))],
        out_specs=pl.BlockSpec(memory_space=pl.ANY),
        scratch_shapes=[pltpu.VMEM((MOE_BLOCK, d), h.dtype),
                        pltpu.SemaphoreType.DMA(()), pltpu.SemaphoreType.DMA(())])
    return pl.pallas_call(
        functools.partial(_dispatch_kernel, td=td, nb=nb),
        grid_spec=grid_spec,
        out_shape=jax.ShapeDtypeStruct((n_rows, d), h.dtype),
        compiler_params=_params("arbitrary"),
        name="moe_dispatch",
    )(pos, zero_flag, h)


def _expert_kernel(blk_e_ref, first_ref, ord_ref, next_ref, nused_ref, xs_ref, wg_hbm, wu_hbm, wd_hbm,
                   y_ref, wg_f, wu_f, wd_f, wg_s, wu_s, wd_s, sems, *, li):
    j = pl.program_id(0)
    used = j < nused_ref[0]

    def weight_copies(e, slot):
        return [pltpu.make_async_copy(src.at[li, e], dst.at[slot], sems.at[slot])
                for src, dst in ((wg_hbm, wg_f), (wu_hbm, wu_f), (wd_hbm, wd_f))]

    @pl.when(j == 0)
    def _():
        for c in weight_copies(blk_e_ref[0], 0):
            c.start()

    @pl.when(used & (first_ref[j] == 1))
    def _():
        slot = ord_ref[j] % 2
        for c in weight_copies(blk_e_ref[j], slot):
            c.wait()
        nxt = next_ref[j]

        @pl.when(nxt < N_EXPERTS)
        def _():
            for c in weight_copies(nxt, 1 - slot):
                c.start()

        wg_s[...] = wg_f[slot].astype(BF16)
        wu_s[...] = wu_f[slot].astype(BF16)
        wd_s[...] = wd_f[slot].astype(BF16)

    @pl.when(used)
    def _():
        x_lo, x_hi = _unpack_bf16_pairs(xs_ref[...])
        x_lo = x_lo.astype(BF16)
        x_hi = x_hi.astype(BF16)
        dh = x_lo.shape[1]

        def proj(w_s):
            return (jnp.dot(x_lo, w_s[:dh, :], preferred_element_type=F32)
                    + jnp.dot(x_hi, w_s[dh:, :], preferred_element_type=F32))

        gate = proj(wg_s)
        up = proj(wu_s)
        act = (gate * _sigmoid(gate) * up).astype(BF16)
        y_ref[...] = _pack_bf16_pairs(jnp.dot(act, wd_s[...], preferred_element_type=F32))

    @pl.when(jnp.logical_not(used))
    def _():
        y_ref[...] = jnp.zeros(y_ref.shape, y_ref.dtype)


def moe_experts(xs, blk_e, blk_first, blk_ord, blk_next, n_used, w_gate, w_up, w_down, li):
    n_rows, dp = xs.shape
    d = w_gate.shape[2]
    ff = w_gate.shape[3]
    nb = n_rows // MOE_BLOCK
    grid_spec = pltpu.PrefetchScalarGridSpec(
        num_scalar_prefetch=5, grid=(nb,),
        in_specs=[pl.BlockSpec((MOE_BLOCK, dp), lambda j, be, bf, bo, bn, nu: (jnp.minimum(j, nu[0] - 1), 0)),
                  pl.BlockSpec(memory_space=pl.ANY),
                  pl.BlockSpec(memory_space=pl.ANY),
                  pl.BlockSpec(memory_space=pl.ANY)],
        out_specs=pl.BlockSpec((MOE_BLOCK, dp), lambda j, *_: (j, 0)),
        scratch_shapes=[pltpu.VMEM((2, d, ff), F32), pltpu.VMEM((2, d, ff), F32),
                        pltpu.VMEM((2, ff, d), F32),
                        pltpu.VMEM((d, ff), BF16), pltpu.VMEM((d, ff), BF16),
                        pltpu.VMEM((ff, d), BF16),
                        pltpu.SemaphoreType.DMA((2,))])
    return pl.pallas_call(
        functools.partial(_expert_kernel, li=li),
        grid_spec=grid_spec,
        out_shape=jax.ShapeDtypeStruct((n_rows, dp), U32),
        compiler_params=_params("arbitrary"),
        name="moe_experts",
    )(blk_e, blk_first, blk_ord, blk_next, n_used, xs, w_gate, w_up, w_down)


def _combine_kernel(pos_ref, x_ref, rg_ref, gf_ref, y_ref, o_ref, ybuf, sems, *, tc, final_norm):
    i = pl.program_id(0)
    slot = i % 2

    def issue_tile(tile, dst_slot):
        base = tile * tc

        def issue(r, carry):
            for kk in range(TOP_K):
                pos = pos_ref[(base + r) * TOP_K + kk]
                pltpu.make_async_copy(y_ref.at[pl.ds(pos, 1)], ybuf.at[dst_slot, kk, pl.ds(r, 1)],
                                      sems.at[dst_slot]).start()
            return carry

        lax.fori_loop(0, tc, issue, 0, unroll=DMA_UNROLL)

    @pl.when(i == 0)
    def _():
        issue_tile(0, 0)

    @pl.when(i + 1 < pl.num_programs(0))
    def _():
        issue_tile(i + 1, 1 - slot)

    for kk in range(TOP_K):
        pltpu.make_async_copy(y_ref.at[pl.ds(0, tc)], ybuf.at[slot, kk], sems.at[slot]).wait()
    gates = rg_ref[...]
    g0 = gates[:, 0:1]
    g1 = gates[:, 1:2]
    y0_lo, y0_hi = _unpack_bf16_pairs(ybuf[slot, 0])
    y1_lo, y1_hi = _unpack_bf16_pairs(ybuf[slot, 1])
    dh = y0_lo.shape[1]
    out_lo = x_ref[:, :dh] + (g0 * y0_lo + g1 * y1_lo)
    out_hi = x_ref[:, dh:] + (g0 * y0_hi + g1 * y1_hi)
    if final_norm:
        ms = (jnp.sum(out_lo * out_lo, axis=-1, keepdims=True)
              + jnp.sum(out_hi * out_hi, axis=-1, keepdims=True)) * (1.0 / (2 * dh))
        inv = lax.rsqrt(ms + EPS)
        out_lo = out_lo * inv * gf_ref[:, :dh]
        out_hi = out_hi * inv * gf_ref[:, dh:]
    o_ref[:, :dh] = out_lo
    o_ref[:, dh:] = out_hi


def moe_combine(x, y, gates, pos, g_final, final_norm, *, tc=512):
    t, d = x.shape
    tc = min(tc, t)
    grid_spec = pltpu.PrefetchScalarGridSpec(
        num_scalar_prefetch=1, grid=(t // tc,),
        in_specs=[pl.BlockSpec((tc, d), lambda i, *_: (i, 0)),
                  pl.BlockSpec((tc, LANES), lambda i, *_: (i, 0)),
                  pl.BlockSpec((1, d), lambda i, *_: (0, 0)),
                  pl.BlockSpec(memory_space=pl.ANY)],
        out_specs=pl.BlockSpec((tc, d), lambda i, *_: (i, 0)),
        scratch_shapes=[pltpu.VMEM((2, TOP_K, tc, d // 2), U32), pltpu.SemaphoreType.DMA((2,))])
    return pl.pallas_call(
        functools.partial(_combine_kernel, tc=tc, final_norm=final_norm),
        grid_spec=grid_spec,
        out_shape=jax.ShapeDtypeStruct((t, d), F32),
        compiler_params=_params("arbitrary"),
        name="moe_combine",
    )(pos, x, gates, g_final.reshape(1, d), y)


def moe_layer(x, g_norm, w_rg, w_re, w_gate, w_up, w_down, li, g_final, final_norm):
    t, d = x.shape
    h, ri, gates, cnt = moe_router(x, g_norm, w_rg, w_re)
    e_flat = ri[:, 0:TOP_K].reshape(-1)
    rank_flat = ri[:, TOP_K:2 * TOP_K].reshape(-1)
    counts = cnt[0, ROUTER_LANE0:ROUTER_LANE0 + N_EXPERTS]
    nblk = (counts + MOE_BLOCK - 1) // MOE_BLOCK
    blk_end = jnp.cumsum(nblk)
    pad_start = ((blk_end - nblk) * MOE_BLOCK).astype(I32)
    n_used = blk_end[-1:].astype(I32)
    nb = (t * TOP_K) // MOE_BLOCK + N_EXPERTS
    blk_ids = jnp.arange(nb, dtype=I32)
    blk = jnp.minimum(blk_ids, n_used[0] - 1)
    blk_e = jnp.sum(blk_end[None, :] <= blk[:, None], axis=1).astype(I32)
    is_last = (blk_ids + 1 == blk_end[blk_e]) | (blk_ids >= n_used[0])
    zero_flag = is_last.astype(I32)
    e_ids = jnp.arange(N_EXPERTS, dtype=I32)
    seg_start = jnp.sum(jnp.where(e_flat[:, None] == e_ids[None, :], pad_start[None, :], 0), axis=1)
    pos = (seg_start + rank_flat).astype(I32)
    prev_e = jnp.concatenate([jnp.full((1,), -1, I32), blk_e[:-1]])
    blk_first = ((blk_e != prev_e) & (blk_ids < n_used[0])).astype(I32)
    blk_ord = (jnp.cumsum(blk_first) - 1).astype(I32)
    nonempty_id = jnp.where(nblk > 0, e_ids, N_EXPERTS)
    next_ge = lax.cummin(nonempty_id, reverse=True)
    next_after = jnp.concatenate([next_ge[1:], jnp.full((1,), N_EXPERTS, I32)])
    blk_next = next_after[blk_e].astype(I32)
    xs = moe_dispatch(h, pos, zero_flag, nb * MOE_BLOCK)
    y = moe_experts(xs, blk_e, blk_first, blk_ord, blk_next, n_used, w_gate, w_up, w_down, li)
    return moe_combine(x, y, gates, pos, g_final, final_norm)


def kernel(x, rel_bias, norm_mix, norm_ffn, norm_final, w_in_ab, lam_ab, subln_ab, w_gk_ab, b_gk_ab,
           gnorm_ab, w_out_ab, w_in_c, ln_v_g, ln_v_b, w_spatial, b_spatial, w_out_c,
           w_router_group, w_router_expert, w_exp_gate, w_exp_up, w_exp_down):
    bsz, s_len, d = x.shape
    depth = norm_mix.shape[0]
    xf = x.reshape(bsz * s_len, d)
    for layer in range(depth):
        i = layer // 2
        if layer % 2 == 0:
            lam_init = 0.8 - 0.6 * math.exp(-0.3 * layer)
            n_main = 3 * A_WIDTH + 2 * B_HEADS * B_KEY_DIM + 2 * B_WIDTH
            proj, gb = norm_proj(xf, norm_mix[layer], w_in_ab, i, n_main)
            o_a = diff_attention(proj, rel_bias, lam_ab[i], subln_ab[i], lam_init, bsz, s_len)
            o_b = gla_mixer(proj, gb, w_gk_ab[i], b_gk_ab[i], gnorm_ab[i], bsz, s_len)
            xf = out_proj(o_a, 0, o_b, 0, w_out_ab, i, xf)
        else:
            proj = norm_proj(xf, norm_mix[layer], w_in_c, i, w_in_c.shape[2], act="gelu")
            uz = spatial_gate(proj, ln_v_g[i], ln_v_b[i], w_spatial[i], b_spatial[i])
            xf = out_proj(uz, 0, uz, 1, w_out_c, i, xf)
        xf = moe_layer(xf, norm_ffn[layer], w_router_group[layer], w_router_expert[layer],
                       w_exp_gate, w_exp_up, w_exp_down, layer,
                       norm_final, layer == depth - 1)
    return xf.reshape(bsz, s_len, d)
```

```python
import functools
import math

import jax
import jax.numpy as jnp
import numpy as np
from jax import lax
from jax.experimental import pallas as pl
from jax.experimental.pallas import tpu as pltpu

F32 = jnp.float32
BF16 = jnp.bfloat16
I32 = jnp.int32
U32 = jnp.uint32
HIGHEST = lax.Precision.HIGHEST

EPS = 1e-6
NEG = -1e30
A_HEADS = 8
A_QK_DIM = 64
A_V_DIM = 128
A_WIDTH = A_HEADS * A_V_DIM
N_BUCKETS = 32
MAX_EXACT = 16
MAX_DISTANCE = 128
B_HEADS = 4
B_KEY_DIM = 128
B_VAL_DIM = 256
B_WIDTH = B_HEADS * B_VAL_DIM
GATE_RANK = 16
GATE_TEMP = 16.0
GLA_CHUNK = 64
C_GROUPS = 8
C_CHUNK = 128
N_GROUPS = 4
EXPERTS_PER_GROUP = 8
N_EXPERTS = N_GROUPS * EXPERTS_PER_GROUP
TOP_K = 2
LANES = 128
ROUTER_LANE0 = N_GROUPS
MOE_BLOCK = 256
FAR_TILES = 4
Q_SPLIT_SINGLE = 2
Q_SPLIT_WIDE = 4
VMEM_LIMIT = 56 * 1024 * 1024

NT_DIMS = (((1,), (1,)), ((), ()))
TN_DIMS = (((0,), (0,)), ((), ()))


def _t5_thresholds():
    n = np.arange(0, 2 * MAX_DISTANCE)
    nf = np.maximum(n, 1).astype(np.float64)
    large = MAX_EXACT + (np.log(nf / MAX_EXACT) / math.log(MAX_DISTANCE / MAX_EXACT)
                         * (N_BUCKETS - MAX_EXACT)).astype(np.int64)
    bucket = np.where(n < MAX_EXACT, n, np.minimum(large, N_BUCKETS - 1))
    return [int(np.min(n[bucket >= b])) for b in range(N_BUCKETS)]


T5_THRESHOLDS = _t5_thresholds()


def _params(*sem):
    return pltpu.CompilerParams(dimension_semantics=sem, vmem_limit_bytes=VMEM_LIMIT)


def _rms(x, g):
    ms = jnp.mean(x * x, axis=-1, keepdims=True)
    return x * lax.rsqrt(ms + EPS) * g


def _gelu_tanh(x):
    c = math.sqrt(2.0 / math.pi)
    return 0.5 * x * (1.0 + jnp.tanh(c * (x + 0.044715 * (x * x * x))))


def _sigmoid(x):
    return 1.0 / (1.0 + jnp.exp(-x))


def _pack_bf16_pairs(x):
    c = x.shape[1] // 2
    bits = lax.bitcast_convert_type(x.astype(BF16).astype(F32), U32)
    return (bits[:, :c] >> 16) | (bits[:, c:] & jnp.uint32(0xFFFF0000))


def _unpack_bf16_pairs(w):
    lo = lax.bitcast_convert_type(w << 16, F32)
    hi = lax.bitcast_convert_type(w & jnp.uint32(0xFFFF0000), F32)
    return lo, hi


def _norm_proj_kernel(*refs, act, has_extra):
    if has_extra:
        x_ref, g_ref, w_ref, we_ref, o_ref, oe_ref, h_scr = refs
    else:
        x_ref, g_ref, w_ref, o_ref, h_scr = refs

    @pl.when(pl.program_id(1) == 0)
    def _():
        hb = _rms(x_ref[...], g_ref[...]).astype(BF16)
        h_scr[...] = hb
        if has_extra:
            ye = lax.dot_general(hb, we_ref[...].astype(BF16), NT_DIMS, preferred_element_type=F32)
            oe_ref[...] = ye[:, :oe_ref.shape[1]]

    if has_extra:
        y = lax.dot_general(h_scr[...], w_ref[...].astype(BF16), NT_DIMS,
                            preferred_element_type=F32)
    else:
        y = jnp.dot(h_scr[...], w_ref[...].astype(BF16), preferred_element_type=F32)
    if act == "gelu":
        y = _gelu_tanh(y)
    o_ref[...] = y.astype(o_ref.dtype)


def norm_proj(x, g, w, li, n_cols, *, act=None, tm=1024, tn=1024):
    t, d = x.shape
    tm = min(tm, t)
    n_extra = w.shape[2] - n_cols
    assert n_cols % tn == 0 and n_cols % LANES == 0 and 0 <= n_extra <= LANES
    grid = (t // tm, n_cols // tn)
    in_specs = [pl.BlockSpec((tm, d), lambda i, j: (i, 0)),
                pl.BlockSpec((1, d), lambda i, j: (0, 0))]
    out_specs = [pl.BlockSpec((tm, tn), lambda i, j: (i, j))]
    out_shape = [jax.ShapeDtypeStruct((t, n_cols), BF16)]
    args = [x, g.reshape(1, d)]
    if n_extra:
        wt = jnp.swapaxes(w, 1, 2)
        in_specs.append(pl.BlockSpec((None, tn, d), lambda i, j: (li, j, 0)))
        in_specs.append(pl.BlockSpec((None, LANES, d), lambda i, j: (li, n_cols // LANES, 0)))
        out_specs.append(pl.BlockSpec((tm, n_extra), lambda i, j: (i, 0)))
        out_shape.append(jax.ShapeDtypeStruct((t, n_extra), F32))
        args += [wt, wt]
    else:
        in_specs.append(pl.BlockSpec((None, d, tn), lambda i, j: (li, 0, j)))
        args.append(w)
    outs = pl.pallas_call(
        functools.partial(_norm_proj_kernel, act=act, has_extra=bool(n_extra)),
        grid=grid, in_specs=in_specs, out_specs=out_specs, out_shape=out_shape,
        scratch_shapes=[pltpu.VMEM((tm, d), BF16)],
        compiler_params=_params("parallel", "arbitrary"),
        name="norm_proj_" + (act or "lin"),
    )(*args)
    return outs if n_extra else outs[0]


def _out_proj_kernel(a0_ref, a1_ref, w_ref, r_ref, o_ref):
    kh = a0_ref.shape[1]
    w = w_ref[...].astype(BF16)
    y = jnp.dot(a0_ref[...], w[:kh], preferred_element_type=F32)
    y = y + jnp.dot(a1_ref[...], w[kh:], preferred_element_type=F32)
    o_ref[...] = r_ref[...] + y


def out_proj(a0, a0_blk, a1, a1_blk, w, li, res, *, tm=1024, tn=1024):
    t, n = res.shape
    k = w.shape[1]
    kh = k // 2
    tm = min(tm, t)
    return pl.pallas_call(
        _out_proj_kernel,
        grid=(t // tm, n // tn),
        in_specs=[pl.BlockSpec((tm, kh), lambda i, j: (i, a0_blk)),
                  pl.BlockSpec((tm, kh), lambda i, j: (i, a1_blk)),
                  pl.BlockSpec((None, k, tn), lambda i, j: (li, 0, j)),
                  pl.BlockSpec((tm, tn), lambda i, j: (i, j))],
        out_specs=pl.BlockSpec((tm, tn), lambda i, j: (i, j)),
        out_shape=jax.ShapeDtypeStruct((t, n), F32),
        compiler_params=_params("parallel", "arbitrary"),
        name="out_proj",
    )(a0, a1, w, res)


def _diff_attn_kernel(tbl_ref, q_ref, k_ref, v_ref, lam_ref, sg_ref, o_ref,
                      m_scr, acc_scr, bias_scr, q_scr, v_scr, *, t, lam_init):
    h = pl.program_id(0)
    qi = pl.program_id(2)
    scale = A_QK_DIM ** -0.5

    @pl.when(qi == 0)
    def _augment_v():
        v_scr[:, :A_V_DIM] = v_ref[...]
        v_scr[:, A_V_DIM:] = jnp.ones((v_scr.shape[0], A_V_DIM), BF16)

    @pl.when((pl.program_id(1) == 0) & (qi == 0))
    def _build_bias():
        row = lax.broadcasted_iota(I32, (t, t), 0)
        col = lax.broadcasted_iota(I32, (t, t), 1)
        for d in range(2):
            rel = row - col + d * t
            acc = jnp.full((t, t), tbl_ref[h], F32)
            for b in range(1, N_BUCKETS):
                acc = jnp.where(rel >= T5_THRESHOLDS[b], tbl_ref[b * A_HEADS + h], acc)
            if d == 0:
                acc = jnp.where(rel >= 0, acc, NEG)
            bias_scr[:, (1 - d) * t:(2 - d) * t] = acc

    m_scr[...] = jnp.full(m_scr.shape, NEG, F32)
    acc_scr[...] = jnp.zeros(acc_scr.shape, F32)
    q = q_ref[...] * scale
    lane = lax.broadcasted_iota(I32, q.shape, 1)
    q_scr[0] = jnp.where(lane < A_QK_DIM, q, jnp.zeros_like(q))
    q_scr[1] = jnp.where(lane >= A_QK_DIM, q, jnp.zeros_like(q))

    def step(ki, n_tiles, bias_cols, bias_const):
        rows = pl.ds(pl.multiple_of(ki * t, t), n_tiles * t)
        k = k_ref[rows, :]
        v_aug = v_scr[rows, :]
        for m in range(2):
            q_split = Q_SPLIT_WIDE if n_tiles > 1 else Q_SPLIT_SINGLE
            for half in range(q_split):
                hs = t // q_split
                rs = slice(half * hs, (half + 1) * hs)
                s = lax.dot_general(q_scr[m, rs, :], k, NT_DIMS, preferred_element_type=F32)
                if bias_cols is not None:
                    s = s + bias_scr[rs, bias_cols]
                m_old = m_scr[m, rs, :]
                m_new = jnp.maximum(m_old, jnp.max(s, axis=-1, keepdims=True) + bias_const)
                alpha = jnp.exp(m_old - m_new)
                p = jnp.exp((s - (m_new - bias_const)).astype(BF16))
                acc_scr[m, rs, :] = alpha * acc_scr[m, rs, :] + jnp.dot(
                    p, v_aug, preferred_element_type=F32)
                m_scr[m, rs, :] = m_new

    far_bias = tbl_ref[(N_BUCKETS - 1) * A_HEADS + h]

    n_far = jnp.maximum(qi - 1, 0)
    n_groups = n_far // FAR_TILES
    rest = n_far - n_groups * FAR_TILES

    def far(i, carry):
        step(i * FAR_TILES, FAR_TILES, None, far_bias)
        return carry

    lax.fori_loop(0, n_groups, far, 0)

    @pl.when(rest >= 2)
    def _far_pair():
        step(n_groups * FAR_TILES, 2, None, far_bias)

    @pl.when(rest % 2 == 1)
    def _far_single():
        step(n_far - 1, 1, None, far_bias)

    @pl.when(qi >= 1)
    def _near_and_diag():
        step(qi - 1, 2, slice(0, 2 * t), 0.0)

    @pl.when(qi == 0)
    def _diag_only():
        step(0, 1, slice(t, 2 * t), 0.0)

    lv = lam_ref[...]
    s1 = jnp.sum(lv[0:1] * lv[1:2], axis=-1, keepdims=True)
    s2 = jnp.sum(lv[2:3] * lv[3:4], axis=-1, keepdims=True)
    lam = jnp.exp(s1) - jnp.exp(s2) + lam_init
    a0 = acc_scr[0]
    a1 = acc_scr[1]
    o = (a0[:, :A_V_DIM] / a0[:, A_V_DIM:A_V_DIM + 1]
         - lam * (a1[:, :A_V_DIM] / a1[:, A_V_DIM:A_V_DIM + 1]))
    o = _rms(o, sg_ref[...]) * (1.0 - lam_init)
    o_ref[...] = o.astype(o_ref.dtype)


def diff_attention(proj, rel_bias, lam_vec, subln_g, lam_init, bsz, s_len, *, t=512):
    t = min(t, s_len)
    assert t >= MAX_DISTANCE and s_len % t == 0 and FAR_TILES == 4
    nq = s_len // t
    q_map = lambda h, b, qi: (b * nq + qi, h)
    kv_map = lambda off: (lambda h, b, qi: (b, off + h))
    const = lambda h, b, qi: (0, 0)
    return pl.pallas_call(
        functools.partial(_diff_attn_kernel, t=t, lam_init=lam_init),
        grid=(A_HEADS, bsz, nq),
        in_specs=[pl.BlockSpec(memory_space=pltpu.SMEM),
                  pl.BlockSpec((t, A_V_DIM), q_map),
                  pl.BlockSpec((s_len, A_V_DIM), kv_map(A_HEADS)),
                  pl.BlockSpec((s_len, A_V_DIM), kv_map(2 * A_HEADS)),
                  pl.BlockSpec((4, A_QK_DIM), const),
                  pl.BlockSpec((1, A_V_DIM), const)],
        out_specs=pl.BlockSpec((t, A_V_DIM), q_map),
        out_shape=jax.ShapeDtypeStruct((bsz * s_len, A_WIDTH), BF16),
        scratch_shapes=[pltpu.VMEM((2, t, 1), F32),
                        pltpu.VMEM((2, t, 2 * A_V_DIM), F32),
                        pltpu.VMEM((t, 2 * t), F32),
                        pltpu.VMEM((2, t, A_V_DIM), BF16),
                        pltpu.VMEM((s_len, 2 * A_V_DIM), BF16)],
        compiler_params=_params("parallel", "arbitrary", "arbitrary"),
        name="diff_attention",
    )(rel_bias.reshape(-1), proj, proj, proj, lam_vec, subln_g.reshape(1, A_V_DIM))


def _gla_kernel(q_ref, k_ref, v_ref, r_ref, gb_ref, wgk_ref, bgk_ref, gn_ref, o_ref, st_scr, *, rows):
    c_len = GLA_CHUNK
    scale = B_KEY_DIM ** -0.5

    @pl.when(pl.program_id(1) == 0)
    def _():
        st_scr[...] = jnp.zeros(st_scr.shape, F32)

    z = jnp.dot(gb_ref[...], wgk_ref[...], preferred_element_type=F32, precision=HIGHEST)
    z = z + bgk_ref[...]
    g_log = (jnp.minimum(z, 0.0) - jnp.log(1.0 + jnp.exp(-jnp.abs(z)))) * (1.0 / GATE_TEMP)
    row = lax.broadcasted_iota(I32, (rows, rows), 0)
    col = lax.broadcasted_iota(I32, (rows, rows), 1)
    ltri = ((row >= col) & ((row // c_len) == (col // c_len))).astype(F32)
    ltri = ltri.astype(BF16)
    g_hi = g_log.astype(BF16)
    g_r1 = g_log - g_hi.astype(F32)
    g_mid = g_r1.astype(BF16)
    g_lo = (g_r1 - g_mid.astype(F32)).astype(BF16)
    bcum = (jnp.dot(ltri, g_hi, preferred_element_type=F32)
            + (jnp.dot(ltri, g_mid, preferred_element_type=F32)
               + jnp.dot(ltri, g_lo, preferred_element_type=F32)))
    causal = (lax.broadcasted_iota(I32, (c_len, c_len), 0)
              >= lax.broadcasted_iota(I32, (c_len, c_len), 1))
    gn = gn_ref[...]

    for c in range(rows // c_len):
        rs = slice(c * c_len, (c + 1) * c_len)
        for hh in range(B_HEADS):
            ks = slice(hh * B_KEY_DIM, (hh + 1) * B_KEY_DIM)
            vs = slice(hh * B_VAL_DIM, (hh + 1) * B_VAL_DIM)
            b = bcum[rs, ks]
            b_last = b[c_len - 1:c_len, :]
            b_mid = b[c_len // 2:c_len // 2 + 1, :]
            q = q_ref[rs, ks].astype(F32) * scale
            k = k_ref[rs, ks].astype(F32)
            v = v_ref[rs, vs]
            st = st_scr[hh]
            q_in = (q * jnp.exp(b)).astype(BF16)
            q_ia = (q * jnp.exp(b - b_mid)).astype(BF16)
            k_ia = (k * jnp.exp(b_mid - b)).astype(BF16)
            att = lax.dot_general(q_ia, k_ia, NT_DIMS, preferred_element_type=F32)
            att = jnp.where(causal, att, 0.0)
            o = jnp.dot(att.astype(BF16), v, preferred_element_type=F32)
            o = o + lax.dot_general(q_in, st.astype(BF16), NT_DIMS, preferred_element_type=F32)
            k_st = (k * jnp.exp(b_last - b)).astype(BF16)
            st_scr[hh] = st * jnp.exp(b_last) + lax.dot_general(
                v, k_st, TN_DIMS, preferred_element_type=F32)
            on = _rms(o, gn)
            r = r_ref[rs, vs].astype(F32)
            o_ref[rs, vs] = (on * (r * _sigmoid(r))).astype(o_ref.dtype)


def gla_mixer(proj, gb, w_gk, b_gk, gnorm_g, bsz, s_len, *, rows=256):
    rows = min(rows, s_len)
    nr = s_len // rows
    kw = B_HEADS * B_KEY_DIM
    rmap = lambda blk: (lambda b, i: (b * nr + i, blk))
    return pl.pallas_call(
        functools.partial(_gla_kernel, rows=rows),
        grid=(bsz, nr),
        in_specs=[pl.BlockSpec((rows, kw), rmap(3 * A_WIDTH // kw)),
                  pl.BlockSpec((rows, kw), rmap(3 * A_WIDTH // kw + 1)),
                  pl.BlockSpec((rows, B_WIDTH), rmap((3 * A_WIDTH + 2 * kw) // B_WIDTH)),
                  pl.BlockSpec((rows, B_WIDTH), rmap((3 * A_WIDTH + 2 * kw) // B_WIDTH + 1)),
                  pl.BlockSpec((rows, GATE_RANK), rmap(0)),
                  pl.BlockSpec((GATE_RANK, kw), lambda b, i: (0, 0)),
                  pl.BlockSpec((1, kw), lambda b, i: (0, 0)),
                  pl.BlockSpec((1, B_VAL_DIM), lambda b, i: (0, 0))],
        out_specs=pl.BlockSpec((rows, B_WIDTH), rmap(0)),
        out_shape=jax.ShapeDtypeStruct((bsz * s_len, B_WIDTH), BF16),
        scratch_shapes=[pltpu.VMEM((B_HEADS, B_VAL_DIM, B_KEY_DIM), F32)],
        compiler_params=_params("parallel", "arbitrary"),
        name="gla",
    )(proj, proj, proj, proj, gb, w_gk, b_gk.reshape(1, kw), gnorm_g.reshape(1, B_VAL_DIM))


def _spatial_kernel(u_ref, v_ref, lg_ref, lb_ref, ws_ref, bs_ref, o_ref, *, rows):
    v = v_ref[...].astype(F32)
    mu = jnp.mean(v, axis=-1, keepdims=True)
    vc = v - mu
    var = jnp.mean(vc * vc, axis=-1, keepdims=True)
    vn = (vc * lax.rsqrt(var + EPS) * lg_ref[...] + lb_ref[...]).astype(BF16)
    gd = v.shape[1] // C_GROUPS
    mask = (lax.broadcasted_iota(I32, (C_CHUNK, C_CHUNK), 0)
            >= lax.broadcasted_iota(I32, (C_CHUNK, C_CHUNK), 1))
    for g in range(C_GROUPS):
        w = jnp.where(mask, ws_ref[g], 0.0).astype(BF16)
        bias = bs_ref[:, g:g + 1]
        cs = slice(g * gd, (g + 1) * gd)
        for c in range(rows // C_CHUNK):
            rs = slice(c * C_CHUNK, (c + 1) * C_CHUNK)
            z = jnp.dot(w, vn[rs, cs], preferred_element_type=F32) + bias
            o_ref[rs, cs] = (u_ref[rs, cs].astype(F32) * z).astype(o_ref.dtype)


def spatial_gate(proj, ln_g, ln_b, w_s, b_s, *, rows=512):
    t = proj.shape[0]
    cw = proj.shape[1] // 2
    rows = min(rows, t)
    return pl.pallas_call(
        functools.partial(_spatial_kernel, rows=rows),
        grid=(t // rows,),
        in_specs=[pl.BlockSpec((rows, cw), lambda i: (i, 0)),
                  pl.BlockSpec((rows, cw), lambda i: (i, 1)),
                  pl.BlockSpec((1, cw), lambda i: (0, 0)),
                  pl.BlockSpec((1, cw), lambda i: (0, 0)),
                  pl.BlockSpec((C_GROUPS, C_CHUNK, C_CHUNK), lambda i: (0, 0, 0)),
                  pl.BlockSpec((C_CHUNK, C_GROUPS), lambda i: (0, 0))],
        out_specs=pl.BlockSpec((rows, cw), lambda i: (i, 0)),
        out_shape=jax.ShapeDtypeStruct((t, cw), BF16),
        compiler_params=_params("parallel"),
        name="spatial_gate",
    )(proj, proj, ln_g.reshape(1, cw), ln_b.reshape(1, cw), w_s, b_s.T)


def _router_kernel(x_ref, g_ref, wr_ref, h_ref, ri_ref, rg_ref, cnt_ref, carry_scr, *, tr):
    @pl.when(pl.program_id(0) == 0)
    def _():
        carry_scr[...] = jnp.zeros(carry_scr.shape, F32)

    h = _rms(x_ref[...], g_ref[...])
    h_ref[...] = _pack_bf16_pairs(h)
    h_hi = h.astype(BF16)
    h_lo = (h - h_hi.astype(F32)).astype(BF16)
    w_hi = wr_ref[0]
    w_lo = wr_ref[1]
    logits = (jnp.dot(h_hi, w_hi, preferred_element_type=F32)
              + (jnp.dot(h_hi, w_lo, preferred_element_type=F32)
                 + jnp.dot(h_lo, w_hi, preferred_element_type=F32)))
    lane = lax.broadcasted_iota(I32, (tr, LANES), 1)
    lanef = lane.astype(F32)
    big = float(LANES)

    is_g = lane < N_GROUPS
    lg = jnp.where(is_g, logits, NEG)
    gmax = jnp.max(lg, axis=-1, keepdims=True)
    p_grp = 1.0 / jnp.sum(jnp.exp(lg - gmax), axis=-1, keepdims=True)
    gidx = jnp.min(jnp.where(is_g & (lg == gmax), lanef, big), axis=-1, keepdims=True)

    lo = ROUTER_LANE0 + EXPERTS_PER_GROUP * gidx
    in_e = (lanef >= lo) & (lanef < lo + EXPERTS_PER_GROUP)
    le = jnp.where(in_e, logits, NEG)
    emax = jnp.max(le, axis=-1, keepdims=True)
    i1 = jnp.min(jnp.where(in_e & (le == emax), lanef, big), axis=-1, keepdims=True)
    oh1 = lanef == i1
    le2 = jnp.where(oh1, NEG, le)
    emax2 = jnp.max(le2, axis=-1, keepdims=True)
    i2 = jnp.min(jnp.where(in_e & (le2 == emax2) & (~oh1), lanef, big), axis=-1, keepdims=True)
    oh2 = lanef == i2
    ratio = jnp.exp(emax2 - emax)
    g1 = p_grp / (1.0 + ratio)
    g2 = p_grp * ratio / (1.0 + ratio)

    onehot = (oh1 | oh2).astype(BF16)
    tri = (lax.broadcasted_iota(I32, (tr, tr), 0) > lax.broadcasted_iota(I32, (tr, tr), 1))
    before = jnp.dot(tri.astype(BF16), onehot, preferred_element_type=F32) + carry_scr[...]
    rank1 = jnp.sum(jnp.where(oh1, before, 0.0), axis=-1, keepdims=True)
    rank2 = jnp.sum(jnp.where(oh2, before, 0.0), axis=-1, keepdims=True)
    carry = carry_scr[...] + jnp.sum(onehot.astype(F32), axis=0, keepdims=True)
    carry_scr[...] = carry

    e1 = i1 - ROUTER_LANE0
    e2 = i2 - ROUTER_LANE0
    ri = jnp.where(lane == 0, e1, jnp.where(lane == 1, e2,
                                             jnp.where(lane == 2, rank1,
                                                       jnp.where(lane == 3, rank2, 0.0))))
    ri_ref[...] = ri.astype(I32)
    rg_ref[...] = jnp.where(lane == 0, g1, jnp.where(lane == 1, g2, 0.0))
    cnt_ref[...] = jnp.broadcast_to(carry, cnt_ref.shape).astype(I32)


def moe_router(x, g, w_rg, w_re, *, tr=512):
    t, d = x.shape
    tr = min(tr, t)
    w_r = jnp.pad(jnp.concatenate([w_rg, w_re], axis=1),
                  ((0, 0), (0, LANES - N_GROUPS - N_EXPERTS)))
    w_hi = w_r.astype(BF16)
    w_r = jnp.stack([w_hi, (w_r - w_hi.astype(F32)).astype(BF16)])
    return pl.pallas_call(
        functools.partial(_router_kernel, tr=tr),
        grid=(t // tr,),
        in_specs=[pl.BlockSpec((tr, d), lambda i: (i, 0)),
                  pl.BlockSpec((1, d), lambda i: (0, 0)),
                  pl.BlockSpec((2, d, LANES), lambda i: (0, 0, 0))],
        out_specs=[pl.BlockSpec((tr, d // 2), lambda i: (i, 0)),
                   pl.BlockSpec((tr, LANES), lambda i: (i, 0)),
                   pl.BlockSpec((tr, LANES), lambda i: (i, 0)),
                   pl.BlockSpec((8, LANES), lambda i: (0, 0))],
        out_shape=[jax.ShapeDtypeStruct((t, d // 2), U32),
                   jax.ShapeDtypeStruct((t, LANES), I32),
                   jax.ShapeDtypeStruct((t, LANES), F32),
                   jax.ShapeDtypeStruct((8, LANES), I32)],
        scratch_shapes=[pltpu.VMEM((1, LANES), F32)],
        compiler_params=_params("arbitrary"),
        name="moe_router",
    )(x, g.reshape(1, d), w_r)


DMA_UNROLL = 8


def _dispatch_kernel(pos_ref, zf_ref, h_ref, xs_ref, zero_scr, sem, zsem, *, td, nb):
    i = pl.program_id(0)
    base = i * td

    @pl.when(i == 0)
    def _():
        zero_scr[...] = jnp.zeros(zero_scr.shape, zero_scr.dtype)

        def zero_copy(j):
            return pltpu.make_async_copy(zero_scr, xs_ref.at[pl.ds(j * MOE_BLOCK, MOE_BLOCK)], zsem)

        def zstart(j, carry):
            @pl.when(zf_ref[j] == 1)
            def _():
                zero_copy(j).start()
            return carry

        def zwait(j, carry):
            @pl.when(zf_ref[j] == 1)
            def _():
                zero_copy(j).wait()
            return carry

        lax.fori_loop(0, nb, zstart, 0)
        lax.fori_loop(0, nb, zwait, 0)

    def issue(r, carry):
        for kk in range(TOP_K):
            pos = pos_ref[(base + r) * TOP_K + kk]
            pltpu.make_async_copy(h_ref.at[pl.ds(r, 1)], xs_ref.at[pl.ds(pos, 1)], sem).start()
        return carry

    lax.fori_loop(0, td, issue, 0, unroll=DMA_UNROLL)
    for kk in range(TOP_K):
        pltpu.make_async_copy(h_ref, xs_ref.at[pl.ds(0, td)], sem).wait()


def moe_dispatch(h, pos, zero_flag, n_rows, *, td=512):
    t, d = h.shape
    td = min(td, t)
    nb = n_rows // MOE_BLOCK
    grid_spec = pltpu.PrefetchScalarGridSpec(
        num_scalar_prefetch=2, grid=(t // td,),
        in_specs=[pl.BlockSpec((td, d), lambda i, *_: (i, 0))],
        out_specs=pl.BlockSpec(memory_space=pl.ANY),
        scratch_shapes=[pltpu.VMEM((MOE_BLOCK, d), h.dtype),
                        pltpu.SemaphoreType.DMA(()), pltpu.SemaphoreType.DMA(())])
    return pl.pallas_call(
        functools.partial(_dispatch_kernel, td=td, nb=nb),
        grid_spec=grid_spec,
        out_shape=jax.ShapeDtypeStruct((n_rows, d), h.dtype),
        compiler_params=_params("arbitrary"),
        name="moe_dispatch",
    )(pos, zero_flag, h)


def _expert_kernel(blk_e_ref, first_ref, ord_ref, next_ref, nused_ref, xs_ref, wg_hbm, wu_hbm, wd_hbm,
                   y_ref, wg_f, wu_f, wd_f, wg_s, wu_s, wd_s, sems, *, li):
    j = pl.program_id(0)
    used = j < nused_ref[0]

    def weight_copies(e, slot):
        return [pltpu.make_async_copy(src.at[li, e], dst.at[slot], sems.at[slot])
                for src, dst in ((wg_hbm, wg_f), (wu_hbm, wu_f), (wd_hbm, wd_f))]

    @pl.when(j == 0)
    def _():
        for c in weight_copies(blk_e_ref[0], 0):
            c.start()

    @pl.when(used & (first_ref[j] == 1))
    def _():
        slot = ord_ref[j] % 2
        for c in weight_copies(blk_e_ref[j], slot):
            c.wait()
        nxt = next_ref[j]

        @pl.when(nxt < N_EXPERTS)
        def _():
            for c in weight_copies(nxt, 1 - slot):
                c.start()

        wg_s[...] = wg_f[slot].astype(BF16)
        wu_s[...] = wu_f[slot].astype(BF16)
        wd_s[...] = wd_f[slot].astype(BF16)

    @pl.when(used)
    def _():
        x_lo, x_hi = _unpack_bf16_pairs(xs_ref[...])
        x_lo = x_lo.astype(BF16)
        x_hi = x_hi.astype(BF16)
        dh = x_lo.shape[1]

        def proj(w_s):
            return (jnp.dot(x_lo, w_s[:dh, :], preferred_element_type=F32)
                    + jnp.dot(x_hi, w_s[dh:, :], preferred_element_type=F32))

        gate = proj(wg_s)
        up = proj(wu_s)
        act = (gate * _sigmoid(gate) * up).astype(BF16)
        y_ref[...] = _pack_bf16_pairs(jnp.dot(act, wd_s[...], preferred_element_type=F32))

    @pl.when(jnp.logical_not(used))
    def _():
        y_ref[...] = jnp.zeros(y_ref.shape, y_ref.dtype)


def moe_experts(xs, blk_e, blk_first, blk_ord, blk_next, n_used, w_gate, w_up, w_down, li):
    n_rows, dp = xs.shape
    d = w_gate.shape[2]
    ff = w_gate.shape[3]
    nb = n_rows // MOE_BLOCK
    grid_spec = pltpu.PrefetchScalarGridSpec(
        num_scalar_prefetch=5, grid=(nb,),
        in_specs=[pl.BlockSpec((MOE_BLOCK, dp), lambda j, be, bf, bo, bn, nu: (jnp.minimum(j, nu[0] - 1), 0)),
                  pl.BlockSpec(memory_space=pl.ANY),
                  pl.BlockSpec(memory_space=pl.ANY),
                  pl.BlockSpec(memory_space=pl.ANY)],
        out_specs=pl.BlockSpec((MOE_BLOCK, dp), lambda j, *_: (j, 0)),
        scratch_shapes=[pltpu.VMEM((2, d, ff), F32), pltpu.VMEM((2, d, ff), F32),
                        pltpu.VMEM((2, ff, d), F32),
                        pltpu.VMEM((d, ff), BF16), pltpu.VMEM((d, ff), BF16),
                        pltpu.VMEM((ff, d), BF16),
                        pltpu.SemaphoreType.DMA((2,))])
    return pl.pallas_call(
        functools.partial(_expert_kernel, li=li),
        grid_spec=grid_spec,
        out_shape=jax.ShapeDtypeStruct((n_rows, dp), U32),
        compiler_params=_params("arbitrary"),
        name="moe_experts",
    )(blk_e, blk_first, blk_ord, blk_next, n_used, xs, w_gate, w_up, w_down)


def _combine_kernel(pos_ref, x_ref, rg_ref, gf_ref, y_ref, o_ref, ybuf, sems, *, tc, final_norm):
    i = pl.program_id(0)
    slot = i % 2

    def issue_tile(tile, dst_slot):
        base = tile * tc

        def issue(r, carry):
            for kk in range(TOP_K):
                pos = pos_ref[(base + r) * TOP_K + kk]
                pltpu.make_async_copy(y_ref.at[pl.ds(pos, 1)], ybuf.at[dst_slot, kk, pl.ds(r, 1)],
                                      sems.at[dst_slot]).start()
            return carry

        lax.fori_loop(0, tc, issue, 0, unroll=DMA_UNROLL)

    @pl.when(i == 0)
    def _():
        issue_tile(0, 0)

    @pl.when(i + 1 < pl.num_programs(0))
    def _():
        issue_tile(i + 1, 1 - slot)

    for kk in range(TOP_K):
        pltpu.make_async_copy(y_ref.at[pl.ds(0, tc)], ybuf.at[slot, kk], sems.at[slot]).wait()
    gates = rg_ref[...]
    g0 = gates[:, 0:1]
    g1 = gates[:, 1:2]
    y0_lo, y0_hi = _unpack_bf16_pairs(ybuf[slot, 0])
    y1_lo, y1_hi = _unpack_bf16_pairs(ybuf[slot, 1])
    dh = y0_lo.shape[1]
    out_lo = x_ref[:, :dh] + (g0 * y0_lo + g1 * y1_lo)
    out_hi = x_ref[:, dh:] + (g0 * y0_hi + g1 * y1_hi)
    if final_norm:
        ms = (jnp.sum(out_lo * out_lo, axis=-1, keepdims=True)
              + jnp.sum(out_hi * out_hi, axis=-1, keepdims=True)) * (1.0 / (2 * dh))
        inv = lax.rsqrt(ms + EPS)
        out_lo = out_lo * inv * gf_ref[:, :dh]
        out_hi = out_hi * inv * gf_ref[:, dh:]
    o_ref[:, :dh] = out_lo
    o_ref[:, dh:] = out_hi


def moe_combine(x, y, gates, pos, g_final, final_norm, *, tc=512):
    t, d = x.shape
    tc = min(tc, t)
    grid_spec = pltpu.PrefetchScalarGridSpec(
        num_scalar_prefetch=1, grid=(t // tc,),
        in_specs=[pl.BlockSpec((tc, d), lambda i, *_: (i, 0)),
                  pl.BlockSpec((tc, LANES), lambda i, *_: (i, 0)),
                  pl.BlockSpec((1, d), lambda i, *_: (0, 0)),
                  pl.BlockSpec(memory_space=pl.ANY)],
        out_specs=pl.BlockSpec((tc, d), lambda i, *_: (i, 0)),
        scratch_shapes=[pltpu.VMEM((2, TOP_K, tc, d // 2), U32), pltpu.SemaphoreType.DMA((2,))])
    return pl.pallas_call(
        functools.partial(_combine_kernel, tc=tc, final_norm=final_norm),
        grid_spec=grid_spec,
        out_shape=jax.ShapeDtypeStruct((t, d), F32),
        compiler_params=_params("arbitrary"),
        name="moe_combine",
    )(pos, x, gates, g_final.reshape(1, d), y)


def moe_layer(x, g_norm, w_rg, w_re, w_gate, w_up, w_down, li, g_final, final_norm):
    t, d = x.shape
    h, ri, gates, cnt = moe_router(x, g_norm, w_rg, w_re)
    e_flat = ri[:, 0:TOP_K].reshape(-1)
    rank_flat = ri[:, TOP_K:2 * TOP_K].reshape(-1)
    counts = cnt[0, ROUTER_LANE0:ROUTER_LANE0 + N_EXPERTS]
    nblk = (counts + MOE_BLOCK - 1) // MOE_BLOCK
    blk_end = jnp.cumsum(nblk)
    pad_start = ((blk_end - nblk) * MOE_BLOCK).astype(I32)
    n_used = blk_end[-1:].astype(I32)
    nb = (t * TOP_K) // MOE_BLOCK + N_EXPERTS
    blk_ids = jnp.arange(nb, dtype=I32)
    blk = jnp.minimum(blk_ids, n_used[0] - 1)
    blk_e = jnp.sum(blk_end[None, :] <= blk[:, None], axis=1).astype(I32)
    is_last = (blk_ids + 1 == blk_end[blk_e]) | (blk_ids >= n_used[0])
    zero_flag = is_last.astype(I32)
    e_ids = jnp.arange(N_EXPERTS, dtype=I32)
    seg_start = jnp.sum(jnp.where(e_flat[:, None] == e_ids[None, :], pad_start[None, :], 0), axis=1)
    pos = (seg_start + rank_flat).astype(I32)
    prev_e = jnp.concatenate([jnp.full((1,), -1, I32), blk_e[:-1]])
    blk_first = ((blk_e != prev_e) & (blk_ids < n_used[0])).astype(I32)
    blk_ord = (jnp.cumsum(blk_first) - 1).astype(I32)
    nonempty_id = jnp.where(nblk > 0, e_ids, N_EXPERTS)
    next_ge = lax.cummin(nonempty_id, reverse=True)
    next_after = jnp.concatenate([next_ge[1:], jnp.full((1,), N_EXPERTS, I32)])
    blk_next = next_after[blk_e].astype(I32)
    xs = moe_dispatch(h, pos, zero_flag, nb * MOE_BLOCK)
    y = moe_experts(xs, blk_e, blk_first, blk_ord, blk_next, n_used, w_gate, w_up, w_down, li)
    return moe_combine(x, y, gates, pos, g_final, final_norm)


def kernel(x, rel_bias, norm_mix, norm_ffn, norm_final, w_in_ab, lam_ab, subln_ab, w_gk_ab, b_gk_ab,
           gnorm_ab, w_out_ab, w_in_c, ln_v_g, ln_v_b, w_spatial, b_spatial, w_out_c,
           w_router_group, w_router_expert, w_exp_gate, w_exp_up, w_exp_down):
    bsz, s_len, d = x.shape
    depth = norm_mix.shape[0]
    xf = x.reshape(bsz * s_len, d)
    for layer in range(depth):
        i = layer // 2
        if layer % 2 == 0:
            lam_init = 0.8 - 0.6 * math.exp(-0.3 * layer)
            n_main = 3 * A_WIDTH + 2 * B_HEADS * B_KEY_DIM + 2 * B_WIDTH
            proj, gb = norm_proj(xf, norm_mix[layer], w_in_ab, i, n_main)
            o_a = diff_attention(proj, rel_bias, lam_ab[i], subln_ab[i], lam_init, bsz, s_len)
            o_b = gla_mixer(proj, gb, w_gk_ab[i], b_gk_ab[i], gnorm_ab[i], bsz, s_len)
            xf = out_proj(o_a, 0, o_b, 0, w_out_ab, i, xf)
        else:
            proj = norm_proj(xf, norm_mix[layer], w_in_c, i, w_in_c.shape[2], act="gelu")
            uz = spatial_gate(proj, ln_v_g[i], ln_v_b[i], w_spatial[i], b_spatial[i])
            xf = out_proj(uz, 0, uz, 1, w_out_c, i, xf)
        xf = moe_layer(xf, norm_ffn[layer], w_router_group[layer], w_router_expert[layer],
                       w_exp_gate, w_exp_up, w_exp_down, layer,
                       norm_final, layer == depth - 1)
    return xf.reshape(bsz, s_len, d)
```

```python
import functools
import math

import jax
import jax.numpy as jnp
import numpy as np
from jax import lax
from jax.experimental import pallas as pl
from jax.experimental.pallas import tpu as pltpu

F32 = jnp.float32
BF16 = jnp.bfloat16
I32 = jnp.int32
U32 = jnp.uint32
HIGHEST = lax.Precision.HIGHEST

EPS = 1e-6
NEG = -1e30
A_HEADS = 8
A_QK_DIM = 64
A_V_DIM = 128
A_WIDTH = A_HEADS * A_V_DIM
N_BUCKETS = 32
MAX_EXACT = 16
MAX_DISTANCE = 128
B_HEADS = 4
B_KEY_DIM = 128
B_VAL_DIM = 256
B_WIDTH = B_HEADS * B_VAL_DIM
GATE_RANK = 16
GATE_TEMP = 16.0
GLA_CHUNK = 64
C_GROUPS = 8
C_CHUNK = 128
N_GROUPS = 4
EXPERTS_PER_GROUP = 8
N_EXPERTS = N_GROUPS * EXPERTS_PER_GROUP
TOP_K = 2
LANES = 128
ROUTER_LANE0 = N_GROUPS
MOE_BLOCK = 256
FAR_TILES = 4
Q_SPLIT_SINGLE = 2
Q_SPLIT_WIDE = 4
VMEM_LIMIT = 56 * 1024 * 1024

NT_DIMS = (((1,), (1,)), ((), ()))
TN_DIMS = (((0,), (0,)), ((), ()))


def _t5_thresholds():
    n = np.arange(0, 2 * MAX_DISTANCE)
    nf = np.maximum(n, 1).astype(np.float64)
    large = MAX_EXACT + (np.log(nf / MAX_EXACT) / math.log(MAX_DISTANCE / MAX_EXACT)
                         * (N_BUCKETS - MAX_EXACT)).astype(np.int64)
    bucket = np.where(n < MAX_EXACT, n, np.minimum(large, N_BUCKETS - 1))
    return [int(np.min(n[bucket >= b])) for b in range(N_BUCKETS)]


T5_THRESHOLDS = _t5_thresholds()


def _params(*sem):
    return pltpu.CompilerParams(dimension_semantics=sem, vmem_limit_bytes=VMEM_LIMIT)


def _rms(x, g):
    ms = jnp.mean(x * x, axis=-1, keepdims=True)
    return x * lax.rsqrt(ms + EPS) * g


def _gelu_tanh(x):
    c = math.sqrt(2.0 / math.pi)
    return 0.5 * x * (1.0 + jnp.tanh(c * (x + 0.044715 * (x * x * x))))


def _sigmoid(x):
    return 1.0 / (1.0 + jnp.exp(-x))


def _pack_bf16_pairs(x):
    c = x.shape[1] // 2
    bits = lax.bitcast_convert_type(x.astype(BF16).astype(F32), U32)
    return (bits[:, :c] >> 16) | (bits[:, c:] & jnp.uint32(0xFFFF0000))


def _unpack_bf16_pairs(w):
    lo = lax.bitcast_convert_type(w << 16, F32)
    hi = lax.bitcast_convert_type(w & jnp.uint32(0xFFFF0000), F32)
    return lo, hi


def _norm_proj_kernel(*refs, act, has_extra):
    if has_extra:
        x_ref, g_ref, w_ref, we_ref, o_ref, oe_ref, h_scr = refs
    else:
        x_ref, g_ref, w_ref, o_ref, h_scr = refs

    @pl.when(pl.program_id(1) == 0)
    def _():
        hb = _rms(x_ref[...], g_ref[...]).astype(BF16)
        h_scr[...] = hb
        if has_extra:
            ye = lax.dot_general(hb, we_ref[...].astype(BF16), NT_DIMS, preferred_element_type=F32)
            oe_ref[...] = ye[:, :oe_ref.shape[1]]

    if has_extra:
        y = lax.dot_general(h_scr[...], w_ref[...].astype(BF16), NT_DIMS,
                            preferred_element_type=F32)
    else:
        y = jnp.dot(h_scr[...], w_ref[...].astype(BF16), preferred_element_type=F32)
    if act == "gelu":
        y = _gelu_tanh(y)
    o_ref[...] = y.astype(o_ref.dtype)


def norm_proj(x, g, w, li, n_cols, *, act=None, tm=1024, tn=1024):
    t, d = x.shape
    tm = min(tm, t)
    n_extra = w.shape[2] - n_cols
    assert n_cols % tn == 0 and n_cols % LANES == 0 and 0 <= n_extra <= LANES
    grid = (t // tm, n_cols // tn)
    in_specs = [pl.BlockSpec((tm, d), lambda i, j: (i, 0)),
                pl.BlockSpec((1, d), lambda i, j: (0, 0))]
    out_specs = [pl.BlockSpec((tm, tn), lambda i, j: (i, j))]
    out_shape = [jax.ShapeDtypeStruct((t, n_cols), BF16)]
    args = [x, g.reshape(1, d)]
    if n_extra:
        wt = jnp.swapaxes(w, 1, 2)
        in_specs.append(pl.BlockSpec((None, tn, d), lambda i, j: (li, j, 0)))
        in_specs.append(pl.BlockSpec((None, LANES, d), lambda i, j: (li, n_cols // LANES, 0)))
        out_specs.append(pl.BlockSpec((tm, n_extra), lambda i, j: (i, 0)))
        out_shape.append(jax.ShapeDtypeStruct((t, n_extra), F32))
        args += [wt, wt]
    else:
        in_specs.append(pl.BlockSpec((None, d, tn), lambda i, j: (li, 0, j)))
        args.append(w)
    outs = pl.pallas_call(
        functools.partial(_norm_proj_kernel, act=act, has_extra=bool(n_extra)),
        grid=grid, in_specs=in_specs, out_specs=out_specs, out_shape=out_shape,
        scratch_shapes=[pltpu.VMEM((tm, d), BF16)],
        compiler_params=_params("parallel", "arbitrary"),
        name="norm_proj_" + (act or "lin"),
    )(*args)
    return outs if n_extra else outs[0]


def _out_proj_kernel(a0_ref, a1_ref, w_ref, r_ref, o_ref):
    kh = a0_ref.shape[1]
    w = w_ref[...].astype(BF16)
    y = jnp.dot(a0_ref[...], w[:kh], preferred_element_type=F32)
    y = y + jnp.dot(a1_ref[...], w[kh:], preferred_element_type=F32)
    o_ref[...] = r_ref[...] + y


def out_proj(a0, a0_blk, a1, a1_blk, w, li, res, *, tm=1024, tn=1024):
    t, n = res.shape
    k = w.shape[1]
    kh = k // 2
    tm = min(tm, t)
    return pl.pallas_call(
        _out_proj_kernel,
        grid=(t // tm, n // tn),
        in_specs=[pl.BlockSpec((tm, kh), lambda i, j: (i, a0_blk)),
                  pl.BlockSpec((tm, kh), lambda i, j: (i, a1_blk)),
                  pl.BlockSpec((None, k, tn), lambda i, j: (li, 0, j)),
                  pl.BlockSpec((tm, tn), lambda i, j: (i, j))],
        out_specs=pl.BlockSpec((tm, tn), lambda i, j: (i, j)),
        out_shape=jax.ShapeDtypeStruct((t, n), F32),
        compiler_params=_params("parallel", "arbitrary"),
        name="out_proj",
    )(a0, a1, w, res)


def _diff_attn_kernel(tbl_ref, q_ref, k_ref, v_ref, lam_ref, sg_ref, o_ref,
                      m_scr, acc_scr, bias_scr, q_scr, v_scr, *, t, lam_init):
    h = pl.program_id(0)
    qi = pl.program_id(2)
    scale = A_QK_DIM ** -0.5

    @pl.when(qi == 0)
    def _augment_v():
        v_scr[:, :A_V_DIM] = v_ref[...]
        v_scr[:, A_V_DIM:] = jnp.ones((v_scr.shape[0], A_V_DIM), BF16)

    @pl.when((pl.program_id(1) == 0) & (qi == 0))
    def _build_bias():
        row = lax.broadcasted_iota(I32, (t, t), 0)
        col = lax.broadcasted_iota(I32, (t, t), 1)
        for d in range(2):
            rel = row - col + d * t
            acc = jnp.full((t, t), tbl_ref[h], F32)
            for b in range(1, N_BUCKETS):
                acc = jnp.where(rel >= T5_THRESHOLDS[b], tbl_ref[b * A_HEADS + h], acc)
            if d == 0:
                acc = jnp.where(rel >= 0, acc, NEG)
            bias_scr[:, (1 - d) * t:(2 - d) * t] = acc

    m_scr[...] = jnp.full(m_scr.shape, NEG, F32)
    acc_scr[...] = jnp.zeros(acc_scr.shape, F32)
    q = q_ref[...] * scale
    lane = lax.broadcasted_iota(I32, q.shape, 1)
    q_scr[0] = jnp.where(lane < A_QK_DIM, q, jnp.zeros_like(q))
    q_scr[1] = jnp.where(lane >= A_QK_DIM, q, jnp.zeros_like(q))

    def step(ki, n_tiles, bias_cols, bias_const):
        rows = pl.ds(pl.multiple_of(ki * t, t), n_tiles * t)
        k = k_ref[rows, :]
        v_aug = v_scr[rows, :]
        for m in range(2):
            q_split = Q_SPLIT_WIDE if n_tiles > 1 else Q_SPLIT_SINGLE
            for half in range(q_split):
                hs = t // q_split
                rs = slice(half * hs, (half + 1) * hs)
                s = lax.dot_general(q_scr[m, rs, :], k, NT_DIMS, preferred_element_type=F32)
                if bias_cols is not None:
                    s = s + bias_scr[rs, bias_cols]
                m_old = m_scr[m, rs, :]
                m_new = jnp.maximum(m_old, jnp.max(s, axis=-1, keepdims=True) + bias_const)
                alpha = jnp.exp(m_old - m_new)
                p = jnp.exp((s - (m_new - bias_const)).astype(BF16))
                acc_scr[m, rs, :] = alpha * acc_scr[m, rs, :] + jnp.dot(
                    p, v_aug, preferred_element_type=F32)
                m_scr[m, rs, :] = m_new

    far_bias = tbl_ref[(N_BUCKETS - 1) * A_HEADS + h]

    n_far = jnp.maximum(qi - 1, 0)
    n_groups = n_far // FAR_TILES
    rest = n_far - n_groups * FAR_TILES

    def far(i, carry):
        step(i * FAR_TILES, FAR_TILES, None, far_bias)
        return carry

    lax.fori_loop(0, n_groups, far, 0)

    @pl.when(rest >= 2)
    def _far_pair():
        step(n_groups * FAR_TILES, 2, None, far_bias)

    @pl.when(rest % 2 == 1)
    def _far_single():
        step(n_far - 1, 1, None, far_bias)

    @pl.when(qi >= 1)
    def _near_and_diag():
        step(qi - 1, 2, slice(0, 2 * t), 0.0)

    @pl.when(qi == 0)
    def _diag_only():
        step(0, 1, slice(t, 2 * t), 0.0)

    lv = lam_ref[...]
    s1 = jnp.sum(lv[0:1] * lv[1:2], axis=-1, keepdims=True)
    s2 = jnp.sum(lv[2:3] * lv[3:4], axis=-1, keepdims=True)
    lam = jnp.exp(s1) - jnp.exp(s2) + lam_init
    a0 = acc_scr[0]
    a1 = acc_scr[1]
    o = (a0[:, :A_V_DIM] / a0[:, A_V_DIM:A_V_DIM + 1]
         - lam * (a1[:, :A_V_DIM] / a1[:, A_V_DIM:A_V_DIM + 1]))
    o = _rms(o, sg_ref[...]) * (1.0 - lam_init)
    o_ref[...] = o.astype(o_ref.dtype)


def diff_attention(proj, rel_bias, lam_vec, subln_g, lam_init, bsz, s_len, *, t=512):
    t = min(t, s_len)
    assert t >= MAX_DISTANCE and s_len % t == 0 and FAR_TILES == 4
    nq = s_len // t
    q_map = lambda h, b, qi: (b * nq + qi, h)
    kv_map = lambda off: (lambda h, b, qi: (b, off + h))
    const = lambda h, b, qi: (0, 0)
    return pl.pallas_call(
        functools.partial(_diff_attn_kernel, t=t, lam_init=lam_init),
        grid=(A_HEADS, bsz, nq),
        in_specs=[pl.BlockSpec(memory_space=pltpu.SMEM),
                  pl.BlockSpec((t, A_V_DIM), q_map),
                  pl.BlockSpec((s_len, A_V_DIM), kv_map(A_HEADS)),
                  pl.BlockSpec((s_len, A_V_DIM), kv_map(2 * A_HEADS)),
                  pl.BlockSpec((4, A_QK_DIM), const),
                  pl.BlockSpec((1, A_V_DIM), const)],
        out_specs=pl.BlockSpec((t, A_V_DIM), q_map),
        out_shape=jax.ShapeDtypeStruct((bsz * s_len, A_WIDTH), BF16),
        scratch_shapes=[pltpu.VMEM((2, t, 1), F32),
                        pltpu.VMEM((2, t, 2 * A_V_DIM), F32),
                        pltpu.VMEM((t, 2 * t), F32),
                        pltpu.VMEM((2, t, A_V_DIM), BF16),
                        pltpu.VMEM((s_len, 2 * A_V_DIM), BF16)],
        compiler_params=_params("parallel", "arbitrary", "arbitrary"),
        name="diff_attention",
    )(rel_bias.reshape(-1), proj, proj, proj, lam_vec, subln_g.reshape(1, A_V_DIM))


def _gla_kernel(q_ref, k_ref, v_ref, r_ref, gb_ref, wgk_ref, bgk_ref, gn_ref, o_ref, st_scr, *, rows):
    c_len = GLA_CHUNK
    scale = B_KEY_DIM ** -0.5

    @pl.when(pl.program_id(1) == 0)
    def _():
        st_scr[...] = jnp.zeros(st_scr.shape, F32)

    z = jnp.dot(gb_ref[...], wgk_ref[...], preferred_element_type=F32, precision=HIGHEST)
    z = z + bgk_ref[...]
    g_log = (jnp.minimum(z, 0.0) - jnp.log(1.0 + jnp.exp(-jnp.abs(z)))) * (1.0 / GATE_TEMP)
    row = lax.broadcasted_iota(I32, (rows, rows), 0)
    col = lax.broadcasted_iota(I32, (rows, rows), 1)
    ltri = ((row >= col) & ((row // c_len) == (col // c_len))).astype(F32)
    ltri = ltri.astype(BF16)
    g_hi = g_log.astype(BF16)
    g_r1 = g_log - g_hi.astype(F32)
    g_mid = g_r1.astype(BF16)
    g_lo = (g_r1 - g_mid.astype(F32)).astype(BF16)
    bcum = (jnp.dot(ltri, g_hi, preferred_element_type=F32)
            + (jnp.dot(ltri, g_mid, preferred_element_type=F32)
               + jnp.dot(ltri, g_lo, preferred_element_type=F32)))
    causal = (lax.broadcasted_iota(I32, (c_len, c_len), 0)
              >= lax.broadcasted_iota(I32, (c_len, c_len), 1))
    gn = gn_ref[...]

    for c in range(rows // c_len):
        rs = slice(c * c_len, (c + 1) * c_len)
        for hh in range(B_HEADS):
            ks = slice(hh * B_KEY_DIM, (hh + 1) * B_KEY_DIM)
            vs = slice(hh * B_VAL_DIM, (hh + 1) * B_VAL_DIM)
            b = bcum[rs, ks]
            b_last = b[c_len - 1:c_len, :]
            b_mid = b[c_len // 2:c_len // 2 + 1, :]
            q = q_ref[rs, ks].astype(F32) * scale
            k = k_ref[rs, ks].astype(F32)
            v = v_ref[rs, vs]
            st = st_scr[hh]
            q_in = (q * jnp.exp(b)).astype(BF16)
            q_ia = (q * jnp.exp(b - b_mid)).astype(BF16)
            k_ia = (k * jnp.exp(b_mid - b)).astype(BF16)
            att = lax.dot_general(q_ia, k_ia, NT_DIMS, preferred_element_type=F32)
            att = jnp.where(causal, att, 0.0)
            o = jnp.dot(att.astype(BF16), v, preferred_element_type=F32)
            o = o + lax.dot_general(q_in, st.astype(BF16), NT_DIMS, preferred_element_type=F32)
            k_st = (k * jnp.exp(b_last - b)).astype(BF16)
            st_scr[hh] = st * jnp.exp(b_last) + lax.dot_general(
                v, k_st, TN_DIMS, preferred_element_type=F32)
            on = _rms(o, gn)
            r = r_ref[rs, vs].astype(F32)
            o_ref[rs, vs] = (on * (r * _sigmoid(r))).astype(o_ref.dtype)


def gla_mixer(proj, gb, w_gk, b_gk, gnorm_g, bsz, s_len, *, rows=256):
    rows = min(rows, s_len)
    nr = s_len // rows
    kw = B_HEADS * B_KEY_DIM
    rmap = lambda blk: (lambda b, i: (b * nr + i, blk))
    return pl.pallas_call(
        functools.partial(_gla_kernel, rows=rows),
        grid=(bsz, nr),
        in_specs=[pl.BlockSpec((rows, kw), rmap(3 * A_WIDTH // kw)),
                  pl.BlockSpec((rows, kw), rmap(3 * A_WIDTH // kw + 1)),
                  pl.BlockSpec((rows, B_WIDTH), rmap((3 * A_WIDTH + 2 * kw) // B_WIDTH)),
                  pl.BlockSpec((rows, B_WIDTH), rmap((3 * A_WIDTH + 2 * kw) // B_WIDTH + 1)),
                  pl.BlockSpec((rows, GATE_RANK), rmap(0)),
                  pl.BlockSpec((GATE_RANK, kw), lambda b, i: (0, 0)),
                  pl.BlockSpec((1, kw), lambda b, i: (0, 0)),
                  pl.BlockSpec((1, B_VAL_DIM), lambda b, i: (0, 0))],
        out_specs=pl.BlockSpec((rows, B_WIDTH), rmap(0)),
        out_shape=jax.ShapeDtypeStruct((bsz * s_len, B_WIDTH), BF16),
        scratch_shapes=[pltpu.VMEM((B_HEADS, B_VAL_DIM, B_KEY_DIM), F32)],
        compiler_params=_params("parallel", "arbitrary"),
        name="gla",
    )(proj, proj, proj, proj, gb, w_gk, b_gk.reshape(1, kw), gnorm_g.reshape(1, B_VAL_DIM))


def _spatial_kernel(u_ref, v_ref, lg_ref, lb_ref, ws_ref, bs_ref, o_ref, *, rows):
    v = v_ref[...].astype(F32)
    mu = jnp.mean(v, axis=-1, keepdims=True)
    vc = v - mu
    var = jnp.mean(vc * vc, axis=-1, keepdims=True)
    vn = (vc * lax.rsqrt(var + EPS) * lg_ref[...] + lb_ref[...]).astype(BF16)
    gd = v.shape[1] // C_GROUPS
    mask = (lax.broadcasted_iota(I32, (C_CHUNK, C_CHUNK), 0)
            >= lax.broadcasted_iota(I32, (C_CHUNK, C_CHUNK), 1))
    for g in range(C_GROUPS):
        w = jnp.where(mask, ws_ref[g], 0.0).astype(BF16)
        bias = bs_ref[:, g:g + 1]
        cs = slice(g * gd, (g + 1) * gd)
        for c in range(rows // C_CHUNK):
            rs = slice(c * C_CHUNK, (c + 1) * C_CHUNK)
            z = jnp.dot(w, vn[rs, cs], preferred_element_type=F32) + bias
            o_ref[rs, cs] = (u_ref[rs, cs].astype(F32) * z).astype(o_ref.dtype)


def spatial_gate(proj, ln_g, ln_b, w_s, b_s, *, rows=512):
    t = proj.shape[0]
    cw = proj.shape[1] // 2
    rows = min(rows, t)
    return pl.pallas_call(
        functools.partial(_spatial_kernel, rows=rows),
        grid=(t // rows,),
        in_specs=[pl.BlockSpec((rows, cw), lambda i: (i, 0)),
                  pl.BlockSpec((rows, cw), lambda i: (i, 1)),
                  pl.BlockSpec((1, cw), lambda i: (0, 0)),
                  pl.BlockSpec((1, cw), lambda i: (0, 0)),
                  pl.BlockSpec((C_GROUPS, C_CHUNK, C_CHUNK), lambda i: (0, 0, 0)),
                  pl.BlockSpec((C_CHUNK, C_GROUPS), lambda i: (0, 0))],
        out_specs=pl.BlockSpec((rows, cw), lambda i: (i, 0)),
        out_shape=jax.ShapeDtypeStruct((t, cw), BF16),
        compiler_params=_params("parallel"),
        name="spatial_gate",
    )(proj, proj, ln_g.reshape(1, cw), ln_b.reshape(1, cw), w_s, b_s.T)


def _router_kernel(x_ref, g_ref, wr_ref, h_ref, ri_ref, rg_ref, cnt_ref, carry_scr, *, tr):
    @pl.when(pl.program_id(0) == 0)
    def _():
        carry_scr[...] = jnp.zeros(carry_scr.shape, F32)

    h = _rms(x_ref[...], g_ref[...])
    h_ref[...] = _pack_bf16_pairs(h)
    h_hi = h.astype(BF16)
    h_lo = (h - h_hi.astype(F32)).astype(BF16)
    w_hi = wr_ref[0]
    w_lo = wr_ref[1]
    logits = (jnp.dot(h_hi, w_hi, preferred_element_type=F32)
              + (jnp.dot(h_hi, w_lo, preferred_element_type=F32)
                 + jnp.dot(h_lo, w_hi, preferred_element_type=F32)))
    lane = lax.broadcasted_iota(I32, (tr, LANES), 1)
    lanef = lane.astype(F32)
    big = float(LANES)

    is_g = lane < N_GROUPS
    lg = jnp.where(is_g, logits, NEG)
    gmax = jnp.max(lg, axis=-1, keepdims=True)
    p_grp = 1.0 / jnp.sum(jnp.exp(lg - gmax), axis=-1, keepdims=True)
    gidx = jnp.min(jnp.where(is_g & (lg == gmax), lanef, big), axis=-1, keepdims=True)

    lo = ROUTER_LANE0 + EXPERTS_PER_GROUP * gidx
    in_e = (lanef >= lo) & (lanef < lo + EXPERTS_PER_GROUP)
    le = jnp.where(in_e, logits, NEG)
    emax = jnp.max(le, axis=-1, keepdims=True)
    i1 = jnp.min(jnp.where(in_e & (le == emax), lanef, big), axis=-1, keepdims=True)
    oh1 = lanef == i1
    le2 = jnp.where(oh1, NEG, le)
    emax2 = jnp.max(le2, axis=-1, keepdims=True)
    i2 = jnp.min(jnp.where(in_e & (le2 == emax2) & (~oh1), lanef, big), axis=-1, keepdims=True)
    oh2 = lanef == i2
    ratio = jnp.exp(emax2 - emax)
    g1 = p_grp / (1.0 + ratio)
    g2 = p_grp * ratio / (1.0 + ratio)

    onehot = (oh1 | oh2).astype(BF16)
    tri = (lax.broadcasted_iota(I32, (tr, tr), 0) > lax.broadcasted_iota(I32, (tr, tr), 1))
    before = jnp.dot(tri.astype(BF16), onehot, preferred_element_type=F32) + carry_scr[...]
    rank1 = jnp.sum(jnp.where(oh1, before, 0.0), axis=-1, keepdims=True)
    rank2 = jnp.sum(jnp.where(oh2, before, 0.0), axis=-1, keepdims=True)
    carry = carry_scr[...] + jnp.sum(onehot.astype(F32), axis=0, keepdims=True)
    carry_scr[...] = carry

    e1 = i1 - ROUTER_LANE0
    e2 = i2 - ROUTER_LANE0
    ri = jnp.where(lane == 0, e1, jnp.where(lane == 1, e2,
                                             jnp.where(lane == 2, rank1,
                                                       jnp.where(lane == 3, rank2, 0.0))))
    ri_ref[...] = ri.astype(I32)
    rg_ref[...] = jnp.where(lane == 0, g1, jnp.where(lane == 1, g2, 0.0))
    cnt_ref[...] = jnp.broadcast_to(carry, cnt_ref.shape).astype(I32)


def moe_router(x, g, w_rg, w_re, *, tr=512):
    t, d = x.shape
    tr = min(tr, t)
    w_r = jnp.pad(jnp.concatenate([w_rg, w_re], axis=1),
                  ((0, 0), (0, LANES - N_GROUPS - N_EXPERTS)))
    w_hi = w_r.astype(BF16)
    w_r = jnp.stack([w_hi, (w_r - w_hi.astype(F32)).astype(BF16)])
    return pl.pallas_call(
        functools.partial(_router_kernel, tr=tr),
        grid=(t // tr,),
        in_specs=[pl.BlockSpec((tr, d), lambda i: (i, 0)),
                  pl.BlockSpec((1, d), lambda i: (0, 0)),
                  pl.BlockSpec((2, d, LANES), lambda i: (0, 0, 0))],
        out_specs=[pl.BlockSpec((tr, d // 2), lambda i: (i, 0)),
                   pl.BlockSpec((tr, LANES), lambda i: (i, 0)),
                   pl.BlockSpec((tr, LANES), lambda i: (i, 0)),
                   pl.BlockSpec((8, LANES), lambda i: (0, 0))],
        out_shape=[jax.ShapeDtypeStruct((t, d // 2), U32),
                   jax.ShapeDtypeStruct((t, LANES), I32),
                   jax.ShapeDtypeStruct((t, LANES), F32),
                   jax.ShapeDtypeStruct((8, LANES), I32)],
        scratch_shapes=[pltpu.VMEM((1, LANES), F32)],
        compiler_params=_params("arbitrary"),
        name="moe_router",
    )(x, g.reshape(1, d), w_r)


DMA_UNROLL = 8


def _dispatch_kernel(pos_ref, zf_ref, h_ref, xs_ref, zero_scr, sem, zsem, *, td, nb):
    i = pl.program_id(0)
    base = i * td

    @pl.when(i == 0)
    def _():
        zero_scr[...] = jnp.zeros(zero_scr.shape, zero_scr.dtype)

        def zero_copy(j):
            return pltpu.make_async_copy(zero_scr, xs_ref.at[pl.ds(j * MOE_BLOCK, MOE_BLOCK)], zsem)

        def zstart(j, carry):
            @pl.when(zf_ref[j] == 1)
            def _():
                zero_copy(j).start()
            return carry

        def zwait(j, carry):
            @pl.when(zf_ref[j] == 1)
            def _():
                zero_copy(j).wait()
            return carry

        lax.fori_loop(0, nb, zstart, 0)
        lax.fori_loop(0, nb, zwait, 0)

    def issue(r, carry):
        for kk in range(TOP_K):
            pos = pos_ref[(base + r) * TOP_K + kk]
            pltpu.make_async_copy(h_ref.at[pl.ds(r, 1)], xs_ref.at[pl.ds(pos, 1)], sem).start(
                priority=kk % 2)
        return carry

    lax.fori_loop(0, td, issue, 0, unroll=DMA_UNROLL)
    for kk in range(TOP_K):
        pltpu.make_async_copy(h_ref, xs_ref.at[pl.ds(0, td)], sem).wait()


def moe_dispatch(h, pos, zero_flag, n_rows, *, td=512):
    t, d = h.shape
    td = min(td, t)
    nb = n_rows // MOE_BLOCK
    grid_spec = pltpu.PrefetchScalarGridSpec(
        num_scalar_prefetch=2, grid=(t // td,),
        in_specs=[pl.BlockSpec((td, d), lambda i, *_: (i, 0))],
        out_specs=pl.BlockSpec(memory_space=pl.ANY),
        scratch_shapes=[pltpu.VMEM((MOE_BLOCK, d), h.dtype),
                        pltpu.SemaphoreType.DMA(()), pltpu.SemaphoreType.DMA(())])
    return pl.pallas_call(
        functools.partial(_dispatch_kernel, td=td, nb=nb),
        grid_spec=grid_spec,
        out_shape=jax.ShapeDtypeStruct((n_rows, d), h.dtype),
        compiler_params=_params("arbitrary"),
        name="moe_dispatch",
    )(pos, zero_flag, h)


def _expert_kernel(blk_e_ref, first_ref, ord_ref, next_ref, nused_ref, xs_ref, wg_hbm, wu_hbm, wd_hbm,
                   y_ref, wg_f, wu_f, wd_f, wg_s, wu_s, wd_s, sems, *, li):
    j = pl.program_id(0)
    used = j < nused_ref[0]

    def weight_copies(e, slot):
        return [pltpu.make_async_copy(src.at[li, e], dst.at[slot], sems.at[slot])
                for src, dst in ((wg_hbm, wg_f), (wu_hbm, wu_f), (wd_hbm, wd_f))]

    @pl.when(j == 0)
    def _():
        for c in weight_copies(blk_e_ref[0], 0):
            c.start()

    @pl.when(used & (first_ref[j] == 1))
    def _():
        slot = ord_ref[j] % 2
        for c in weight_copies(blk_e_ref[j], slot):
            c.wait()
        nxt = next_ref[j]

        @pl.when(nxt < N_EXPERTS)
        def _():
            for c in weight_copies(nxt, 1 - slot):
                c.start()

        wg_s[...] = wg_f[slot].astype(BF16)
        wu_s[...] = wu_f[slot].astype(BF16)
        wd_s[...] = wd_f[slot].astype(BF16)

    @pl.when(used)
    def _():
        x_lo, x_hi = _unpack_bf16_pairs(xs_ref[...])
        x_lo = x_lo.astype(BF16)
        x_hi = x_hi.astype(BF16)
        dh = x_lo.shape[1]

        def proj(w_s):
            return (jnp.dot(x_lo, w_s[:dh, :], preferred_element_type=F32)
                    + jnp.dot(x_hi, w_s[dh:, :], preferred_element_type=F32))

        gate = proj(wg_s)
        up = proj(wu_s)
        act = (gate * _sigmoid(gate) * up).astype(BF16)
        y_ref[...] = _pack_bf16_pairs(jnp.dot(act, wd_s[...], preferred_element_type=F32))

    @pl.when(jnp.logical_not(used))
    def _():
        y_ref[...] = jnp.zeros(y_ref.shape, y_ref.dtype)


def moe_experts(xs, blk_e, blk_first, blk_ord, blk_next, n_used, w_gate, w_up, w_down, li):
    n_rows, dp = xs.shape
    d = w_gate.shape[2]
    ff = w_gate.shape[3]
    nb = n_rows // MOE_BLOCK
    grid_spec = pltpu.PrefetchScalarGridSpec(
        num_scalar_prefetch=5, grid=(nb,),
        in_specs=[pl.BlockSpec((MOE_BLOCK, dp), lambda j, be, bf, bo, bn, nu: (jnp.minimum(j, nu[0] - 1), 0)),
                  pl.BlockSpec(memory_space=pl.ANY),
                  pl.BlockSpec(memory_space=pl.ANY),
                  pl.BlockSpec(memory_space=pl.ANY)],
        out_specs=pl.BlockSpec((MOE_BLOCK, dp), lambda j, *_: (j, 0)),
        scratch_shapes=[pltpu.VMEM((2, d, ff), F32), pltpu.VMEM((2, d, ff), F32),
                        pltpu.VMEM((2, ff, d), F32),
                        pltpu.VMEM((d, ff), BF16), pltpu.VMEM((d, ff), BF16),
                        pltpu.VMEM((ff, d), BF16),
                        pltpu.SemaphoreType.DMA((2,))])
    return pl.pallas_call(
        functools.partial(_expert_kernel, li=li),
        grid_spec=grid_spec,
        out_shape=jax.ShapeDtypeStruct((n_rows, dp), U32),
        compiler_params=_params("arbitrary"),
        name="moe_experts",
    )(blk_e, blk_first, blk_ord, blk_next, n_used, xs, w_gate, w_up, w_down)


def _combine_kernel(pos_ref, x_ref, rg_ref, gf_ref, y_ref, o_ref, ybuf, sems, *, tc, final_norm):
    i = pl.program_id(0)
    slot = i % 2

    def issue_tile(tile, dst_slot):
        base = tile * tc

        def issue(r, carry):
            for kk in range(TOP_K):
                pos = pos_ref[(base + r) * TOP_K + kk]
                pltpu.make_async_copy(y_ref.at[pl.ds(pos, 1)], ybuf.at[dst_slot, kk, pl.ds(r, 1)],
                                      sems.at[dst_slot]).start(priority=kk % 2)
            return carry

        lax.fori_loop(0, tc, issue, 0, unroll=DMA_UNROLL)

    @pl.when(i == 0)
    def _():
        issue_tile(0, 0)

    @pl.when(i + 1 < pl.num_programs(0))
    def _():
        issue_tile(i + 1, 1 - slot)

    for kk in range(TOP_K):
        pltpu.make_async_copy(y_ref.at[pl.ds(0, tc)], ybuf.at[slot, kk], sems.at[slot]).wait()
    gates = rg_ref[...]
    g0 = gates[:, 0:1]
    g1 = gates[:, 1:2]
    y0_lo, y0_hi = _unpack_bf16_pairs(ybuf[slot, 0])
    y1_lo, y1_hi = _unpack_bf16_pairs(ybuf[slot, 1])
    dh = y0_lo.shape[1]
    out_lo = x_ref[:, :dh] + (g0 * y0_lo + g1 * y1_lo)
    out_hi = x_ref[:, dh:] + (g0 * y0_hi + g1 * y1_hi)
    if final_norm:
        ms = (jnp.sum(out_lo * out_lo, axis=-1, keepdims=True)
              + jnp.sum(out_hi * out_hi, axis=-1, keepdims=True)) * (1.0 / (2 * dh))
        inv = lax.rsqrt(ms + EPS)
        out_lo = out_lo * inv * gf_ref[:, :dh]
        out_hi = out_hi * inv * gf_ref[:, dh:]
    o_ref[:, :dh] = out_lo
    o_ref[:, dh:] = out_hi


def moe_combine(x, y, gates, pos, g_final, final_norm, *, tc=512):
    t, d = x.shape
    tc = min(tc, t)
    grid_spec = pltpu.PrefetchScalarGridSpec(
        num_scalar_prefetch=1, grid=(t // tc,),
        in_specs=[pl.BlockSpec((tc, d), lambda i, *_: (i, 0)),
                  pl.BlockSpec((tc, LANES), lambda i, *_: (i, 0)),
                  pl.BlockSpec((1, d), lambda i, *_: (0, 0)),
                  pl.BlockSpec(memory_space=pl.ANY)],
        out_specs=pl.BlockSpec((tc, d), lambda i, *_: (i, 0)),
        scratch_shapes=[pltpu.VMEM((2, TOP_K, tc, d // 2), U32), pltpu.SemaphoreType.DMA((2,))])
    return pl.pallas_call(
        functools.partial(_combine_kernel, tc=tc, final_norm=final_norm),
        grid_spec=grid_spec,
        out_shape=jax.ShapeDtypeStruct((t, d), F32),
        compiler_params=_params("arbitrary"),
        name="moe_combine",
    )(pos, x, gates, g_final.reshape(1, d), y)


def moe_layer(x, g_norm, w_rg, w_re, w_gate, w_up, w_down, li, g_final, final_norm):
    t, d = x.shape
    h, ri, gates, cnt = moe_router(x, g_norm, w_rg, w_re)
    e_flat = ri[:, 0:TOP_K].reshape(-1)
    rank_flat = ri[:, TOP_K:2 * TOP_K].reshape(-1)
    counts = cnt[0, ROUTER_LANE0:ROUTER_LANE0 + N_EXPERTS]
    nblk = (counts + MOE_BLOCK - 1) // MOE_BLOCK
    blk_end = jnp.cumsum(nblk)
    pad_start = ((blk_end - nblk) * MOE_BLOCK).astype(I32)
    n_used = blk_end[-1:].astype(I32)
    nb = (t * TOP_K) // MOE_BLOCK + N_EXPERTS
    blk_ids = jnp.arange(nb, dtype=I32)
    blk = jnp.minimum(blk_ids, n_used[0] - 1)
    blk_e = jnp.sum(blk_end[None, :] <= blk[:, None], axis=1).astype(I32)
    is_last = (blk_ids + 1 == blk_end[blk_e]) | (blk_ids >= n_used[0])
    zero_flag = is_last.astype(I32)
    e_ids = jnp.arange(N_EXPERTS, dtype=I32)
    seg_start = jnp.sum(jnp.where(e_flat[:, None] == e_ids[None, :], pad_start[None, :], 0), axis=1)
    pos = (seg_start + rank_flat).astype(I32)
    prev_e = jnp.concatenate([jnp.full((1,), -1, I32), blk_e[:-1]])
    blk_first = ((blk_e != prev_e) & (blk_ids < n_used[0])).astype(I32)
    blk_ord = (jnp.cumsum(blk_first) - 1).astype(I32)
    nonempty_id = jnp.where(nblk > 0, e_ids, N_EXPERTS)
    next_ge = lax.cummin(nonempty_id, reverse=True)
    next_after = jnp.concatenate([next_ge[1:], jnp.full((1,), N_EXPERTS, I32)])
    blk_next = next_after[blk_e].astype(I32)
    xs = moe_dispatch(h, pos, zero_flag, nb * MOE_BLOCK)
    y = moe_experts(xs, blk_e, blk_first, blk_ord, blk_next, n_used, w_gate, w_up, w_down, li)
    return moe_combine(x, y, gates, pos, g_final, final_norm)


def kernel(x, rel_bias, norm_mix, norm_ffn, norm_final, w_in_ab, lam_ab, subln_ab, w_gk_ab, b_gk_ab,
           gnorm_ab, w_out_ab, w_in_c, ln_v_g, ln_v_b, w_spatial, b_spatial, w_out_c,
           w_router_group, w_router_expert, w_exp_gate, w_exp_up, w_exp_down):
    bsz, s_len, d = x.shape
    depth = norm_mix.shape[0]
    xf = x.reshape(bsz * s_len, d)
    for layer in range(depth):
        i = layer // 2
        if layer % 2 == 0:
            lam_init = 0.8 - 0.6 * math.exp(-0.3 * layer)
            n_main = 3 * A_WIDTH + 2 * B_HEADS * B_KEY_DIM + 2 * B_WIDTH
            proj, gb = norm_proj(xf, norm_mix[layer], w_in_ab, i, n_main)
            o_a = diff_attention(proj, rel_bias, lam_ab[i], subln_ab[i], lam_init, bsz, s_len)
            o_b = gla_mixer(proj, gb, w_gk_ab[i], b_gk_ab[i], gnorm_ab[i], bsz, s_len)
            xf = out_proj(o_a, 0, o_b, 0, w_out_ab, i, xf)
        else:
            proj = norm_proj(xf, norm_mix[layer], w_in_c, i, w_in_c.shape[2], act="gelu")
            uz = spatial_gate(proj, ln_v_g[i], ln_v_b[i], w_spatial[i], b_spatial[i])
            xf = out_proj(uz, 0, uz, 1, w_out_c, i, xf)
        xf = moe_layer(xf, norm_ffn[layer], w_router_group[layer], w_router_expert[layer],
                       w_exp_gate, w_exp_up, w_exp_down, layer,
                       norm_final, layer == depth - 1)
    return xf.reshape(bsz, s_len, d)
```
